```python
import jax, jax.numpy as jnp
from jax import lax
import numpy as np

D_MODEL = 1024
BATCH = 8
SEQ = 4096
DEPTH = 1

CHUNK = 64
N_MEM = 256
HEAD_DIM = 64
RWKV_HEADS = 8
RWKV_DIM = RWKV_HEADS * HEAD_DIM
DECAY_LORA = 64
ICLR_LORA = 64
GATE_LORA = 128
POOL_WINDOWS = (2, 4, 8, 16)
POOL_GROUPS = len(POOL_WINDOWS)
POOL_GROUP_DIM = 64
POOL_DIM = POOL_GROUPS * POOL_GROUP_DIM
MEM_HEADS = 4
MEM_DIM = MEM_HEADS * HEAD_DIM
N_BRANCH = 3
RWKV_IN = 3 * RWKV_DIM + DECAY_LORA + ICLR_LORA + GATE_LORA
MIX_IN = RWKV_IN + POOL_DIM + MEM_DIM
D_FF = 2816
CONV_W = 3
NORM_EPS = 1e-6
GN_EPS = 64e-5

kernel_name = "hybrid_rwkv7_pool_memxattn_convffn"


def rms_norm(x, g):
    xf = x.astype(jnp.float32)
    y = xf * lax.rsqrt(jnp.mean(xf * xf, axis=-1, keepdims=True) + NORM_EPS)
    return (y * g.astype(jnp.float32)).astype(x.dtype)


def shift_prev(p):
    return jnp.pad(p[:, :-1], ((0, 0), (1, 0), (0, 0)))


def rwkv7_scan(r, w, k, v, kk, a):
    B, T, H, N = r.shape
    xs = tuple(jnp.moveaxis(t, 1, 0) for t in (r, w, k, v, kk, a))

    def step(S, inp):
        r_t, w_t, k_t, v_t, kk_t, a_t = inp
        s_kk = jnp.einsum('bhvk,bhk->bhv', S, kk_t)
        S = (S * w_t[:, :, None, :]
             - jnp.einsum('bhv,bhk->bhvk', s_kk, kk_t * a_t)
             + jnp.einsum('bhv,bhk->bhvk', v_t, k_t))
        y_t = jnp.einsum('bhvk,bhk->bhv', S, r_t)
        return S, y_t

    S0 = jnp.zeros((B, H, N, N), jnp.float32)
    _, ys = lax.scan(step, S0, xs)
    return jnp.moveaxis(ys, 0, 1)


def rwkv7_branch(p, mu, w0, w_lora_b, a0, a_lora_b, g_lora_b, k_k, k_a, r_k, ln_w, ln_b):
    B, T, _ = p.shape
    f32 = jnp.float32
    p = p + (shift_prev(p) - p) * mu
    splits = [RWKV_DIM, 2 * RWKV_DIM, 3 * RWKV_DIM,
              3 * RWKV_DIM + DECAY_LORA, 3 * RWKV_DIM + DECAY_LORA + ICLR_LORA]
    r, k, v, wd, ad, gd = jnp.split(p, splits, axis=-1)
    w_log = -jax.nn.softplus(-(w0 + jnp.tanh(wd) @ w_lora_b).astype(f32)) - 0.5
    decay = jnp.exp(-jnp.exp(w_log))
    a = jax.nn.sigmoid((a0 + ad @ a_lora_b).astype(f32))
    g = jax.nn.sigmoid(gd) @ g_lora_b

    def heads(t):
        return t.astype(f32).reshape(B, T, RWKV_HEADS, HEAD_DIM)

    kk = heads(k * k_k)
    kk = kk * lax.rsqrt(jnp.sum(kk * kk, axis=-1, keepdims=True) + 1e-12)
    kf = k.astype(f32) * (1.0 + (a - 1.0) * k_a.astype(f32))
    rh, kh, vh, wh, ah = heads(r), heads(kf), heads(v), heads(decay), heads(a)
    y = rwkv7_scan(rh, wh, kh, vh, kk, ah)
    mean = jnp.mean(y, axis=-1, keepdims=True)
    var = jnp.mean(jnp.square(y - mean), axis=-1, keepdims=True)
    yn = ((y - mean) * lax.rsqrt(var + GN_EPS)).reshape(B, T, RWKV_DIM)
    yn = yn * ln_w.astype(f32) + ln_b.astype(f32)
    bonus = jnp.sum(rh * kh * r_k.astype(f32), axis=-1, keepdims=True) * vh
    out = (yn + bonus.reshape(B, T, RWKV_DIM)) * g.astype(f32)
    return out.astype(p.dtype)


def pool_branch(p, pool_w, pool_scale):
    B, T, _ = p.shape
    pf = p.astype(jnp.float32)
    pos = jnp.arange(1, T + 1)
    outs = []
    for gi, win in enumerate(POOL_WINDOWS):
        xg = pf[..., gi * POOL_GROUP_DIM:(gi + 1) * POOL_GROUP_DIM]
        cs = jnp.cumsum(xg, axis=1)
        cs_lag = jnp.pad(cs[:, :T - win], ((0, 0), (win, 0), (0, 0)))
        cnt = jnp.minimum(pos, win).astype(jnp.float32)[None, :, None]
        outs.append((cs - cs_lag) / cnt - xg)
    d = jnp.stack(outs, axis=2)
    y = jnp.einsum('btgc,gcd->btgd', d, pool_w.astype(jnp.float32)).reshape(B, T, POOL_DIM)
    return (y * pool_scale.astype(jnp.float32)).astype(p.dtype)


def memory_branch(q, mem_n, w_mem_kv):
    B, T, _ = q.shape
    M = mem_n.shape[1]
    k, v = jnp.split(mem_n @ w_mem_kv, 2, axis=-1)
    qh = q.reshape(B, T, MEM_HEADS, HEAD_DIM)
    kh = k.reshape(B, M, MEM_HEADS, HEAD_DIM)
    vh = v.reshape(B, M, MEM_HEADS, HEAD_DIM)
    s = jnp.einsum('bthd,bmhd->bhtm', qh, kh).astype(jnp.float32) * (HEAD_DIM ** -0.5)
    prob = jax.nn.softmax(s, axis=-1).astype(vh.dtype)
    o = jnp.einsum('bhtm,bmhd->bthd', prob, vh)
    return o.reshape(B, T, MEM_DIM)


def conv_ffn(h, w_in, conv_w, conv_b, w_out):
    u, gv = jnp.split(h @ w_in, 2, axis=-1)
    T = u.shape[1]
    up = jnp.pad(u, ((0, 0), (CONV_W - 1, 0), (0, 0)))
    uc = conv_w[0] * up[:, 0:T]
    for j in range(1, CONV_W):
        uc = uc + conv_w[j] * up[:, j:j + T]
    uc = uc + conv_b
    return (jax.nn.gelu(uc, approximate=False) * gv) @ w_out


def setup_inputs(seed: int = 0) -> dict:
    key = jax.random.key(seed)
    ks = jax.random.split(key, 32)
    L, D, F = DEPTH, D_MODEL, D_FF

    def nrm(k, shape, scale):
        return jax.random.normal(k, shape, jnp.float32) * scale

    def gain(k, shape, s=0.02):
        return 1.0 + nrm(k, shape, s)

    return {
        "x": nrm(ks[0], (BATCH, SEQ, D), 1.0),
        "mem": nrm(ks[1], (BATCH, N_MEM, D), 1.0),
        "norm_mix_g": gain(ks[2], (L, D)),
        "w_in_mix": nrm(ks[3], (L, D, MIX_IN), D ** -0.5),
        "mu_shift": jax.random.uniform(ks[4], (L, RWKV_IN), jnp.float32),
        "w0": jax.random.uniform(ks[5], (L, RWKV_DIM), jnp.float32, -4.0, 1.0),
        "w_lora_b": nrm(ks[6], (L, DECAY_LORA, RWKV_DIM), 0.5 * DECAY_LORA ** -0.5),
        "a0": nrm(ks[7], (L, RWKV_DIM), 0.1),
        "a_lora_b": nrm(ks[8], (L, ICLR_LORA, RWKV_DIM), ICLR_LORA ** -0.5),
        "g_lora_b": nrm(ks[9], (L, GATE_LORA, RWKV_DIM), GATE_LORA ** -0.5),
        "k_k": gain(ks[10], (L, RWKV_DIM), 0.1),
        "k_a": gain(ks[11], (L, RWKV_DIM), 0.1),
        "r_k": nrm(ks[12], (L, RWKV_HEADS, HEAD_DIM), 0.1),
        "ln_x_w": gain(ks[13], (L, RWKV_DIM)),
        "ln_x_b": nrm(ks[14], (L, RWKV_DIM), 0.02),
        "pool_w": nrm(ks[15], (L, POOL_GROUPS, POOL_GROUP_DIM, POOL_GROUP_DIM), POOL_GROUP_DIM ** -0.5),
        "pool_scale": gain(ks[16], (L, POOL_DIM), 0.1),
        "norm_mem_g": gain(ks[17], (L, D)),
        "w_mem_kv": nrm(ks[18], (L, D, 2 * MEM_DIM), D ** -0.5),
        "w_up_rwkv": nrm(ks[19], (L, RWKV_DIM, D), RWKV_DIM ** -0.5),
        "w_up_pool": nrm(ks[20], (L, POOL_DIM, D), POOL_DIM ** -0.5),
        "w_up_mem": nrm(ks[21], (L, MEM_DIM, D), MEM_DIM ** -0.5),
        "w_gate": nrm(ks[22], (L, D, N_BRANCH * D), D ** -0.5),
        "b_gate": nrm(ks[23], (L, N_BRANCH * D), 0.02),
        "w_o": nrm(ks[24], (L, D, D), D ** -0.5),
        "norm_ffn_g": gain(ks[25], (L, D)),
        "w_ffn_in": nrm(ks[26], (L, D, 2 * F), D ** -0.5),
        "ffn_conv_w": nrm(ks[27], (L, CONV_W, F), CONV_W ** -0.5),
        "ffn_conv_b": nrm(ks[28], (L, F), 0.02),
        "w_ffn_out": nrm(ks[29], (L, F, D), F ** -0.5),
        "norm_final_g": gain(ks[30], (D,)),
    }


def reference(x, mem, norm_mix_g, w_in_mix, mu_shift, w0, w_lora_b, a0, a_lora_b, g_lora_b,
              k_k, k_a, r_k, ln_x_w, ln_x_b, pool_w, pool_scale, norm_mem_g, w_mem_kv,
              w_up_rwkv, w_up_pool, w_up_mem, w_gate, b_gate, w_o, norm_ffn_g, w_ffn_in,
              ffn_conv_w, ffn_conv_b, w_ffn_out, norm_final_g):
    B, T, D = x.shape
    for l in range(DEPTH):
        h = rms_norm(x, norm_mix_g[l])
        p = h @ w_in_mix[l]
        p_rwkv = p[..., :RWKV_IN]
        p_pool = p[..., RWKV_IN:RWKV_IN + POOL_DIM]
        q_mem = p[..., RWKV_IN + POOL_DIM:]
        y_a = rwkv7_branch(p_rwkv, mu_shift[l], w0[l], w_lora_b[l], a0[l], a_lora_b[l],
                           g_lora_b[l], k_k[l], k_a[l], r_k[l], ln_x_w[l], ln_x_b[l])
        y_b = pool_branch(p_pool, pool_w[l], pool_scale[l])
        y_c = memory_branch(q_mem, rms_norm(mem, norm_mem_g[l]), w_mem_kv[l])
        gates = jax.nn.sigmoid((h @ w_gate[l] + b_gate[l]).astype(jnp.float32))
        gates = gates.astype(x.dtype).reshape(B, T, N_BRANCH, D)
        merged = (gates[:, :, 0] * (y_a @ w_up_rwkv[l])
                  + gates[:, :, 1] * (y_b @ w_up_pool[l])
                  + gates[:, :, 2] * (y_c @ w_up_mem[l]))
        x = x + merged @ w_o[l]
        x = x + conv_ffn(rms_norm(x, norm_ffn_g[l]), w_ffn_in[l], ffn_conv_w[l],
                         ffn_conv_b[l], w_ffn_out[l])
    return rms_norm(x, norm_final_g)
```

```python
import functools
import math

import jax
import jax.numpy as jnp
from jax import lax
from jax.experimental import pallas as pl
from jax.experimental.pallas import tpu as pltpu

F32 = jnp.float32
BF16 = jnp.bfloat16

D_MODEL = 1024
HEAD_DIM = 64
RWKV_HEADS = 8
RWKV_DIM = RWKV_HEADS * HEAD_DIM
DECAY_LORA = 64
ICLR_LORA = 64
GATE_LORA = 128
RWKV_IN = 3 * RWKV_DIM + DECAY_LORA + ICLR_LORA + GATE_LORA
POOL_WINDOWS = (2, 4, 8, 16)
POOL_GROUP_DIM = 64
POOL_DIM = len(POOL_WINDOWS) * POOL_GROUP_DIM
POOL_HALO = 16
MEM_HEADS = 4
MEM_DIM = MEM_HEADS * HEAD_DIM
MIX_IN = RWKV_IN + POOL_DIM + MEM_DIM
D_FF = 2816
NORM_EPS = 1e-6
GN_EPS = 64e-5
CHUNK = 64
FF_CHUNK = 256
V7X_VMEM_LIMIT_BYTES = 56 * 1024 * 1024

TM_PROJ = 512
TM_MERGE = 512
TM_FFN = 512
SCAN_BATCH = 2


def _dot(a, b):
    return jnp.dot(a, b, preferred_element_type=F32)


def _dot_nt(a, b):
    return lax.dot_general(a, b, (((1,), (1,)), ((), ())), preferred_element_type=F32)


def _dot_tn(a, b):
    return lax.dot_general(a, b, (((0,), (0,)), ((), ())), preferred_element_type=F32)


def _rms_norm(x, g):
    return x * lax.rsqrt(jnp.mean(x * x, axis=-1, keepdims=True) + NORM_EPS) * g


def _split_dot(x, w_bf16):
    hi = x.astype(BF16)
    lo = (x - hi.astype(F32)).astype(BF16)
    return _dot(hi, w_bf16) + _dot(lo, w_bf16)


def _split3_dot_left(w_bf16, x):
    hi = x.astype(BF16)
    r1 = x - hi.astype(F32)
    mid = r1.astype(BF16)
    lo = (r1 - mid.astype(F32)).astype(BF16)
    return _dot(w_bf16, hi) + _dot(w_bf16, mid) + _dot(w_bf16, lo)


def _mem_kv_kernel(mem_ref, g_ref, w_ref, kv_ref):
    m = _rms_norm(mem_ref[...], g_ref[...])
    kv_ref[...] = _dot(m.astype(BF16), w_ref[...]).astype(BF16)


def _mix_proj_kernel(x_ref, g_ref, wmix_ref, mu_ref, w0_ref, wl_ref, a0_ref, al_ref, gl_ref,
                     kk_ref, ka_ref, hsum_ref, poolw_ref, pools_ref, kv_ref,
                     r_out, k_out, v_out, kkn_out, b_out, lw_out, g_out, yb_out, yc_out,
                     prev_ref, halo_ref):
    t = pl.program_id(1)
    tm = x_ref.shape[0]
    first = t == 0

    h = _rms_norm(x_ref[...], g_ref[...]).astype(BF16)
    p = _dot(h, wmix_ref[...])

    ps = p[:, :RWKV_IN]
    row = lax.broadcasted_iota(jnp.int32, (tm, 1), 0)
    carry = jnp.where(first, 0.0, prev_ref[...])
    prev = jnp.where(row == 0, carry, pltpu.roll(ps, 1, axis=0))
    prev_ref[...] = ps[tm - 1:tm, :]
    ps = ps + (prev - ps) * mu_ref[...]

    r = ps[:, :RWKV_DIM]
    k = ps[:, RWKV_DIM:2 * RWKV_DIM]
    v = ps[:, 2 * RWKV_DIM:3 * RWKV_DIM]
    wa = ps[:, 3 * RWKV_DIM:3 * RWKV_DIM + DECAY_LORA + ICLR_LORA]
    gd = ps[:, 3 * RWKV_DIM + DECAY_LORA + ICLR_LORA:RWKV_IN]

    z = w0_ref[...] + _dot(jnp.tanh(wa).astype(BF16), wl_ref[...])
    lw_out[...] = (-math.exp(-0.5)) * jax.nn.sigmoid(z)
    a = jax.nn.sigmoid(a0_ref[...] + _dot(wa.astype(BF16), al_ref[...]))
    g_out[...] = _dot(jax.nn.sigmoid(gd).astype(BF16), gl_ref[...])

    kkr = k * kk_ref[...]
    kkn = kkr * lax.rsqrt(_split_dot(kkr * kkr, hsum_ref[...]) + 1e-12)
    r_out[...] = r
    v_out[...] = v
    k_out[...] = k * (1.0 + (a - 1.0) * ka_ref[...])
    kkn_out[...] = kkn
    b_out[...] = kkn * a

    pp = p[:, RWKV_IN:RWKV_IN + POOL_DIM]
    halo = jnp.where(first, 0.0, halo_ref[...])
    halo_ref[...] = pp[tm - POOL_HALO:, :]
    ext = jnp.concatenate([halo, pp], axis=0)
    lane_group = lax.broadcasted_iota(jnp.int32, (1, POOL_DIM), 1) // POOL_GROUP_DIM
    win = jnp.zeros((1, POOL_DIM), jnp.int32)
    sel = jnp.zeros((tm, POOL_DIM), F32)
    acc = ext
    shift = 1
    for gi, w in enumerate(POOL_WINDOWS):
        while shift < w:
            acc = acc + pltpu.roll(acc, shift, axis=0)
            shift *= 2
        sel = jnp.where(lane_group == gi, acc[POOL_HALO:, :], sel)
        win = jnp.where(lane_group == gi, w, win)
    pos = t * tm + row + 1
    cnt = jnp.minimum(pos, win).astype(F32)
    dpool = sel / cnt - pp
    yb_out[...] = _dot(dpool.astype(BF16), poolw_ref[...]) * pools_ref[...]

    q = p[:, RWKV_IN + POOL_DIM:]
    kmem = kv_ref[:, :MEM_DIM]
    vmem = kv_ref[:, MEM_DIM:]
    lane_head = lax.broadcasted_iota(jnp.int32, (1, MEM_DIM), 1) // HEAD_DIM
    yc = jnp.zeros((tm, MEM_DIM), F32)
    for hd in range(MEM_HEADS):
        in_head = lane_head == hd
        qh = jnp.where(in_head, q, 0.0).astype(BF16)
        s = _dot_nt(qh, kmem) * (HEAD_DIM ** -0.5)
        e = jnp.exp(s - jnp.max(s, axis=-1, keepdims=True))
        prob = e / jnp.sum(e, axis=-1, keepdims=True)
        yc = jnp.where(in_head, _dot(prob.astype(BF16), vmem), yc)
    yc_out[...] = yc


def _chunk_masks():
    c = CHUNK
    ri = lax.broadcasted_iota(jnp.int32, (2 * c, 2 * c), 0)
    ci = lax.broadcasted_iota(jnp.int32, (2 * c, 2 * c), 1)
    rt = ri % c
    ct = ci % c
    aa_mask = (rt > ct) | ((ri >= c) & (rt == ct))
    r64 = lax.broadcasted_iota(jnp.int32, (c, c), 0)
    c64 = lax.broadcasted_iota(jnp.int32, (c, c), 1)
    eye = r64 == c64
    levels = []
    blk = 2
    while blk < c:
        same_outer = (r64 // (2 * blk)) == (c64 // (2 * blk))
        diff_inner = (r64 // blk) != (c64 // blk)
        levels.append(same_outer & diff_inner & (r64 > c64))
        blk *= 2
    diag2 = ((r64 // 2) == (c64 // 2)) & (r64 > c64)
    tri_incl = (r64 >= c64)
    return aa_mask, eye, diag2, levels, tri_incl


def _rwkv_scan_kernel(r_ref, k_ref, v_ref, kk_ref, b_ref, lw_ref, g_ref, rk_ref, lnw_ref, lnb_ref,
                      y_ref, state_ref):
    c = CHUNK
    n = HEAD_DIM
    chunk_idx = pl.program_id(1)

    @pl.when(chunk_idx == 0)
    def _():
        state_ref[...] = jnp.zeros_like(state_ref)

    aa_mask, eye, diag2, levels, tri_incl = _chunk_masks()
    tri = jnp.where(tri_incl, 1.0, 0.0).astype(BF16)
    eye_f = jnp.where(eye, 1.0, 0.0)
    zeros_cn = jnp.zeros((c, n), F32)

    for bi in range(r_ref.shape[0]):
        lw = lw_ref[bi]
        cum = _split3_dot_left(tri, lw)
        tot = cum[c - 1:c, :]
        e_fwd = jnp.exp(cum)
        e_inv = jnp.exp(-cum)
        e_prev = jnp.exp(cum - lw)
        e_end = jnp.exp(tot - cum)
        e_tot = jnp.exp(tot)
        r_all = r_ref[bi]
        k_all = k_ref[bi]
        v_all = v_ref[bi]
        b_all = b_ref[bi]
        g_all = g_ref[bi]
        rt_all = r_all * e_fwd
        kkt_all = kk_ref[bi] * e_prev
        bt_all = b_all * e_inv
        kt_all = k_all * e_inv
        bd_all = b_all * e_end
        kd_all = k_all * e_end
        bonus_all = r_all * k_all * rk_ref[...]

        outs = []
        for hd in range(RWKV_HEADS):
            sl = slice(hd * n, (hd + 1) * n)
            rt, kkt, bt, kt = rt_all[:, sl], kkt_all[:, sl], bt_all[:, sl], kt_all[:, sl]
            vh = v_all[:, sl]
            lhs = jnp.concatenate([kkt, rt], axis=0).astype(BF16)
            rhs = jnp.concatenate([bt, kt], axis=0).astype(BF16)
            aa = jnp.where(aa_mask, _dot_nt(lhs, rhs), 0.0)
            v0 = jnp.concatenate([zeros_cn, vh], axis=0).astype(BF16)
            av = _dot(aa.astype(BF16), v0)
            a_ab = aa[:c, :c]
            a_rb = aa[c:, :c]

            tinv = eye_f - jnp.where(diag2, a_ab, 0.0)
            for lvl in levels:
                off = jnp.where(lvl, a_ab, 0.0).astype(BF16)
                tb = tinv.astype(BF16)
                tinv = tinv - _dot(_dot(tb, off).astype(BF16), tb)

            zz = jnp.concatenate([kkt, av[:c]], axis=1).astype(BF16)
            w = _dot(tinv.astype(BF16), zz)
            w_b = w.astype(BF16)
            pq = jnp.concatenate([rt, av[c:]], axis=1) - _dot(a_rb.astype(BF16), w_b)
            neg_bd_kd = jnp.concatenate([-bd_all[:, sl], kd_all[:, sl]], axis=0).astype(BF16)
            wv = jnp.concatenate([w_b, jnp.concatenate([zeros_cn, vh], axis=1).astype(BF16)], axis=0)
            mg = _dot_tn(neg_bd_kd, wv)
            m_mat = mg[:, :n] + jnp.where(eye, e_tot[:, sl], 0.0)

            st = state_ref[bi * RWKV_HEADS + hd]
            st_b = st.astype(BF16)
            y = _dot(pq[:, :n].astype(BF16), st_b) + pq[:, n:]
            state_ref[bi * RWKV_HEADS + hd] = _dot(m_mat.astype(BF16), st_b) + mg[:, n:]

            mean = jnp.mean(y, axis=-1, keepdims=True)
            yc = y - mean
            var = jnp.mean(yc * yc, axis=-1, keepdims=True)
            yn = yc * lax.rsqrt(var + GN_EPS) * lnw_ref[:, sl] + lnb_ref[:, sl]
            bonus = jnp.sum(bonus_all[:, sl], axis=-1, keepdims=True) * vh
            outs.append((yn + bonus) * g_all[:, sl])
        y_ref[bi] = jnp.concatenate(outs, axis=1)


def _merge_kernel(x_ref, ya_ref, yb_ref, yc_ref, g_ref, wg_ref, bg_ref, wr_ref, wp_ref, wm_ref, wo_ref,
                  out_ref):
    x = x_ref[...]
    h = _rms_norm(x, g_ref[...]).astype(BF16)
    d = D_MODEL
    merged = None
    for i, (y_ref, w_ref) in enumerate(((ya_ref, wr_ref), (yb_ref, wp_ref), (yc_ref, wm_ref))):
        gate = jax.nn.sigmoid(_dot(h, wg_ref[:, i * d:(i + 1) * d]) + bg_ref[:, i * d:(i + 1) * d])
        term = gate * _dot(y_ref[...].astype(BF16), w_ref[...])
        merged = term if merged is None else merged + term
    out_ref[...] = x + _dot(merged.astype(BF16), wo_ref[...])


def _conv_ffn_kernel(x_ref, g_ref, win_ref, cw_ref, cb_ref, wout_ref, gf_ref, out_ref, tail_ref):
    t = pl.program_id(1)
    tm = x_ref.shape[0]
    x = x_ref[...]
    h = _rms_norm(x, g_ref[...]).astype(BF16)
    row = lax.broadcasted_iota(jnp.int32, (tm, 1), 0)
    first = t == 0
    acc = x
    for j in range(D_FF // FF_CHUNK):
        cs = slice(j * FF_CHUNK, (j + 1) * FF_CHUNK)
        u = _dot(h, win_ref[:, cs])
        gv = _dot(h, win_ref[:, D_FF + j * FF_CHUNK:D_FF + (j + 1) * FF_CHUNK])
        tail = jnp.where(first, 0.0, tail_ref[:, cs])
        tail_ref[:, cs] = u[tm - 8:, :]
        u1 = jnp.where(row == 0, tail[7:8, :], pltpu.roll(u, 1, axis=0))
        u2 = jnp.where(row == 0, tail[6:7, :], jnp.where(row == 1, tail[7:8, :], pltpu.roll(u, 2, axis=0)))
        uc = cw_ref[0:1, cs] * u2 + cw_ref[1:2, cs] * u1 + cw_ref[2:3, cs] * u + cb_ref[:, cs]
        act = 0.5 * uc * (1.0 + lax.erf(uc * (2.0 ** -0.5))) * gv
        acc = acc + _dot(act.astype(BF16), wout_ref[cs, :])
    out_ref[...] = _rms_norm(acc, gf_ref[...])


def _const_spec(shape):
    nd = len(shape)
    return pl.BlockSpec(shape, lambda *_: (0,) * nd)


def kernel(x, mem, norm_mix_g, w_in_mix, mu_shift, w0, w_lora_b, a0, a_lora_b, g_lora_b, k_k, k_a, r_k,
           ln_x_w, ln_x_b, pool_w, pool_scale, norm_mem_g, w_mem_kv, w_up_rwkv, w_up_pool, w_up_mem,
           w_gate, b_gate, w_o, norm_ffn_g, w_ffn_in, ffn_conv_w, ffn_conv_b, w_ffn_out, norm_final_g):
    bsz, seq, d = x.shape
    n_mem = mem.shape[1]
    assert d == D_MODEL and seq % TM_PROJ == 0 and seq % CHUNK == 0 and bsz % SCAN_BATCH == 0
    assert norm_mix_g.shape[0] == 1, "single-layer block"
    l = 0
    row = lambda a: a.reshape(1, -1).astype(F32)

    wmix = w_in_mix[l].astype(BF16)
    zpad = jnp.zeros((DECAY_LORA, RWKV_DIM), F32)
    wl_pad = jnp.concatenate([w_lora_b[l], zpad], axis=0).astype(BF16)
    al_pad = jnp.concatenate([zpad, a_lora_b[l]], axis=0).astype(BF16)
    gl = g_lora_b[l].astype(BF16)
    head_id = jnp.arange(RWKV_DIM) // HEAD_DIM
    hsum = (head_id[:, None] == head_id[None, :]).astype(BF16)
    pool_bd = jax.scipy.linalg.block_diag(*[pool_w[l, i] for i in range(len(POOL_WINDOWS))]).astype(BF16)

    kv = pl.pallas_call(
        _mem_kv_kernel,
        out_shape=jax.ShapeDtypeStruct((bsz, n_mem, 2 * MEM_DIM), BF16),
        grid=(bsz,),
        in_specs=[pl.BlockSpec((None, n_mem, d), lambda b: (b, 0, 0)),
                  _const_spec((1, d)), _const_spec((d, 2 * MEM_DIM))],
        out_specs=pl.BlockSpec((None, n_mem, 2 * MEM_DIM), lambda b: (b, 0, 0)),
        compiler_params=pltpu.CompilerParams(dimension_semantics=("arbitrary",)),
        name="mem_kv",
    )(mem, row(norm_mem_g[l]), w_mem_kv[l].astype(BF16))

    tok = lambda w: pl.BlockSpec((None, TM_PROJ, w), lambda b, t: (b, t, 0))
    f32_out = lambda w: jax.ShapeDtypeStruct((bsz, seq, w), F32)
    outs = pl.pallas_call(
        _mix_proj_kernel,
        out_shape=[f32_out(RWKV_DIM)] * 7 + [f32_out(POOL_DIM), f32_out(MEM_DIM)],
        grid=(bsz, seq // TM_PROJ),
        in_specs=[tok(d), _const_spec((1, d)), _const_spec((d, MIX_IN)), _const_spec((1, RWKV_IN)),
                  _const_spec((1, RWKV_DIM)), _const_spec((DECAY_LORA + ICLR_LORA, RWKV_DIM)),
                  _const_spec((1, RWKV_DIM)), _const_spec((DECAY_LORA + ICLR_LORA, RWKV_DIM)),
                  _const_spec((GATE_LORA, RWKV_DIM)), _const_spec((1, RWKV_DIM)), _const_spec((1, RWKV_DIM)),
                  _const_spec((RWKV_DIM, RWKV_DIM)), _const_spec((POOL_DIM, POOL_DIM)),
                  _const_spec((1, POOL_DIM)),
                  pl.BlockSpec((None, n_mem, 2 * MEM_DIM), lambda b, t: (b, 0, 0))],
        out_specs=[tok(RWKV_DIM)] * 7 + [tok(POOL_DIM), tok(MEM_DIM)],
        scratch_shapes=[pltpu.VMEM((1, RWKV_IN), F32), pltpu.VMEM((POOL_HALO, POOL_DIM), F32)],
        compiler_params=pltpu.CompilerParams(dimension_semantics=("arbitrary", "arbitrary"),
                                             vmem_limit_bytes=V7X_VMEM_LIMIT_BYTES),
        name="mix_proj",
    )(x, row(norm_mix_g[l]), wmix, row(mu_shift[l]), row(w0[l]), wl_pad, row(a0[l]), al_pad, gl,
      row(k_k[l]), row(k_a[l]), hsum, pool_bd, row(pool_scale[l]), kv)
    r_s, k_s, v_s, kk_s, b_s, lw_s, g_s, y_b, y_c = outs

    sb = SCAN_BATCH
    blk = pl.BlockSpec((sb, CHUNK, RWKV_DIM), lambda b, c: (b, c, 0))
    y_a = pl.pallas_call(
        _rwkv_scan_kernel,
        out_shape=f32_out(RWKV_DIM),
        grid=(bsz // sb, seq // CHUNK),
        in_specs=[blk] * 7 + [_const_spec((1, RWKV_DIM))] * 3,
        out_specs=blk,
        scratch_shapes=[pltpu.VMEM((sb * RWKV_HEADS, HEAD_DIM, HEAD_DIM), F32)],
        compiler_params=pltpu.CompilerParams(dimension_semantics=("arbitrary", "arbitrary")),
        name="rwkv_scan",
    )(r_s, k_s, v_s, kk_s, b_s, lw_s, g_s, row(r_k[l]), row(ln_x_w[l]), row(ln_x_b[l]))

    n_tok = bsz * seq
    flat = lambda a: a.reshape(n_tok, a.shape[-1])
    tokm = lambda w: pl.BlockSpec((TM_MERGE, w), lambda i: (i, 0))
    x1 = pl.pallas_call(
        _merge_kernel,
        out_shape=jax.ShapeDtypeStruct((n_tok, d), F32),
        grid=(n_tok // TM_MERGE,),
        in_specs=[tokm(d), tokm(RWKV_DIM), tokm(POOL_DIM), tokm(MEM_DIM), _const_spec((1, d)),
                  _const_spec((d, 3 * d)), _const_spec((1, 3 * d)), _const_spec((RWKV_DIM, d)),
                  _const_spec((POOL_DIM, d)), _const_spec((MEM_DIM, d)), _const_spec((d, d))],
        out_specs=tokm(d),
        compiler_params=pltpu.CompilerParams(dimension_semantics=("arbitrary",),
                                             vmem_limit_bytes=V7X_VMEM_LIMIT_BYTES),
        name="merge",
    )(flat(x), flat(y_a), flat(y_b), flat(y_c), row(norm_mix_g[l]), w_gate[l].astype(BF16),
      row(b_gate[l]), w_up_rwkv[l].astype(BF16), w_up_pool[l].astype(BF16), w_up_mem[l].astype(BF16),
      w_o[l].astype(BF16))

    tokf = pl.BlockSpec((None, TM_FFN, d), lambda b, t: (b, t, 0))
    single = pl.Buffered(1)
    out = pl.pallas_call(
        _conv_ffn_kernel,
        out_shape=jax.ShapeDtypeStruct((bsz, seq, d), F32),
        grid=(bsz, seq // TM_FFN),
        in_specs=[tokf, _const_spec((1, d)),
                  pl.BlockSpec((d, 2 * D_FF), lambda b, t: (0, 0), pipeline_mode=single),
                  _const_spec((3, D_FF)), _const_spec((1, D_FF)),
                  pl.BlockSpec((D_FF, d), lambda b, t: (0, 0), pipeline_mode=single),
                  _const_spec((1, d))],
        out_specs=tokf,
        scratch_shapes=[pltpu.VMEM((8, D_FF), F32)],
        compiler_params=pltpu.CompilerParams(dimension_semantics=("arbitrary", "arbitrary"),
                                             vmem_limit_bytes=V7X_VMEM_LIMIT_BYTES),
        name="conv_ffn",
    )(x1.reshape(bsz, seq, d), row(norm_ffn_g[l]), w_ffn_in[l].astype(BF16), ffn_conv_w[l].astype(F32),
      row(ffn_conv_b[l]), w_ffn_out[l].astype(BF16), row(norm_final_g))
    return out
```

```python
import functools
import math

import jax
import jax.numpy as jnp
from jax import lax
from jax.experimental import pallas as pl
from jax.experimental.pallas import tpu as pltpu

F32 = jnp.float32
BF16 = jnp.bfloat16

D_MODEL = 1024
HEAD_DIM = 64
RWKV_HEADS = 8
RWKV_DIM = RWKV_HEADS * HEAD_DIM
DECAY_LORA = 64
ICLR_LORA = 64
GATE_LORA = 128
RWKV_IN = 3 * RWKV_DIM + DECAY_LORA + ICLR_LORA + GATE_LORA
POOL_WINDOWS = (2, 4, 8, 16)
POOL_GROUP_DIM = 64
POOL_DIM = len(POOL_WINDOWS) * POOL_GROUP_DIM
POOL_HALO = 16
MEM_HEADS = 4
MEM_DIM = MEM_HEADS * HEAD_DIM
MIX_IN = RWKV_IN + POOL_DIM + MEM_DIM
D_FF = 2816
NORM_EPS = 1e-6
GN_EPS = 64e-5
CHUNK = 64
FF_CHUNK = 256
V7X_VMEM_LIMIT_BYTES = 56 * 1024 * 1024

TM_PROJ = 512
TM_MERGE = 512
TM_FFN = 512
SCAN_BATCH = 2


def _dot(a, b):
    return jnp.dot(a, b, preferred_element_type=F32)


def _dot_nt(a, b):
    return lax.dot_general(a, b, (((1,), (1,)), ((), ())), preferred_element_type=F32)


def _dot_tn(a, b):
    return lax.dot_general(a, b, (((0,), (0,)), ((), ())), preferred_element_type=F32)


def _rms_norm(x, g):
    return x * lax.rsqrt(jnp.mean(x * x, axis=-1, keepdims=True) + NORM_EPS) * g


def _split_dot(x, w_bf16):
    hi = x.astype(BF16)
    lo = (x - hi.astype(F32)).astype(BF16)
    return _dot(hi, w_bf16) + _dot(lo, w_bf16)


def _split3_dot_left(w_bf16, x):
    hi = x.astype(BF16)
    r1 = x - hi.astype(F32)
    mid = r1.astype(BF16)
    lo = (r1 - mid.astype(F32)).astype(BF16)
    return _dot(w_bf16, hi) + _dot(w_bf16, mid) + _dot(w_bf16, lo)


def _mem_kv_kernel(mem_ref, g_ref, w_ref, kv_ref):
    m = _rms_norm(mem_ref[...], g_ref[...])
    kv_ref[...] = _dot(m.astype(BF16), w_ref[...]).astype(BF16)


def _mix_proj_kernel(x_ref, g_ref, wmix_ref, mu_ref, w0_ref, wl_ref, a0_ref, al_ref, gl_ref,
                     kk_ref, ka_ref, hsum_ref, poolw_ref, pools_ref, kv_ref,
                     r_out, k_out, v_out, kkn_out, b_out, lw_out, g_out, yb_out, yc_out,
                     prev_ref, halo_ref):
    t = pl.program_id(1)
    tm = x_ref.shape[0]
    first = t == 0

    h = _rms_norm(x_ref[...], g_ref[...]).astype(BF16)
    p = _dot(h, wmix_ref[...])

    ps = p[:, :RWKV_IN]
    row = lax.broadcasted_iota(jnp.int32, (tm, 1), 0)
    carry = jnp.where(first, 0.0, prev_ref[...])
    prev = jnp.where(row == 0, carry, pltpu.roll(ps, 1, axis=0))
    prev_ref[...] = ps[tm - 1:tm, :]
    ps = ps + (prev - ps) * mu_ref[...]

    r = ps[:, :RWKV_DIM]
    k = ps[:, RWKV_DIM:2 * RWKV_DIM]
    v = ps[:, 2 * RWKV_DIM:3 * RWKV_DIM]
    wa = ps[:, 3 * RWKV_DIM:3 * RWKV_DIM + DECAY_LORA + ICLR_LORA]
    gd = ps[:, 3 * RWKV_DIM + DECAY_LORA + ICLR_LORA:RWKV_IN]

    z = w0_ref[...] + _dot(jnp.tanh(wa).astype(BF16), wl_ref[...])
    lw_out[...] = (-math.exp(-0.5)) * jax.nn.sigmoid(z)
    a = jax.nn.sigmoid(a0_ref[...] + _dot(wa.astype(BF16), al_ref[...]))
    g_out[...] = _dot(jax.nn.sigmoid(gd).astype(BF16), gl_ref[...])

    kkr = k * kk_ref[...]
    kkn = kkr * lax.rsqrt(_split_dot(kkr * kkr, hsum_ref[...]) + 1e-12)
    r_out[...] = r
    v_out[...] = v
    k_out[...] = k * (1.0 + (a - 1.0) * ka_ref[...])
    kkn_out[...] = kkn
    b_out[...] = kkn * a

    pp = p[:, RWKV_IN:RWKV_IN + POOL_DIM]
    halo = jnp.where(first, 0.0, halo_ref[...])
    halo_ref[...] = pp[tm - POOL_HALO:, :]
    ext = jnp.concatenate([halo, pp], axis=0)
    lane_group = lax.broadcasted_iota(jnp.int32, (1, POOL_DIM), 1) // POOL_GROUP_DIM
    win = jnp.zeros((1, POOL_DIM), jnp.int32)
    sel = jnp.zeros((tm, POOL_DIM), F32)
    acc = ext
    shift = 1
    for gi, w in enumerate(POOL_WINDOWS):
        while shift < w:
            acc = acc + pltpu.roll(acc, shift, axis=0)
            shift *= 2
        sel = jnp.where(lane_group == gi, acc[POOL_HALO:, :], sel)
        win = jnp.where(lane_group == gi, w, win)
    pos = t * tm + row + 1
    cnt = jnp.minimum(pos, win).astype(F32)
    dpool = sel / cnt - pp
    yb_out[...] = _dot(dpool.astype(BF16), poolw_ref[...]) * pools_ref[...]

    q = p[:, RWKV_IN + POOL_DIM:]
    kmem = kv_ref[:, :MEM_DIM]
    vmem = kv_ref[:, MEM_DIM:]
    lane_head = lax.broadcasted_iota(jnp.int32, (1, MEM_DIM), 1) // HEAD_DIM
    yc = jnp.zeros((tm, MEM_DIM), F32)
    for hd in range(MEM_HEADS):
        in_head = lane_head == hd
        qh = jnp.where(in_head, q, 0.0).astype(BF16)
        s = _dot_nt(qh, kmem) * (HEAD_DIM ** -0.5)
        e = jnp.exp(s - jnp.max(s, axis=-1, keepdims=True))
        prob = e / jnp.sum(e, axis=-1, keepdims=True)
        yc = jnp.where(in_head, _dot(prob.astype(BF16), vmem), yc)
    yc_out[...] = yc


def _chunk_masks():
    c = CHUNK
    ri = lax.broadcasted_iota(jnp.int32, (2 * c, 2 * c), 0)
    ci = lax.broadcasted_iota(jnp.int32, (2 * c, 2 * c), 1)
    rt = ri % c
    ct = ci % c
    aa_mask = (rt > ct) | ((ri >= c) & (rt == ct))
    r64 = lax.broadcasted_iota(jnp.int32, (c, c), 0)
    c64 = lax.broadcasted_iota(jnp.int32, (c, c), 1)
    eye = r64 == c64
    levels = []
    blk = 2
    while blk < c:
        same_outer = (r64 // (2 * blk)) == (c64 // (2 * blk))
        diff_inner = (r64 // blk) != (c64 // blk)
        levels.append(same_outer & diff_inner & (r64 > c64))
        blk *= 2
    diag2 = ((r64 // 2) == (c64 // 2)) & (r64 > c64)
    tri_incl = (r64 >= c64)
    return aa_mask, eye, diag2, levels, tri_incl


def _rwkv_scan_kernel(r_ref, k_ref, v_ref, kk_ref, b_ref, lw_ref, g_ref, rk_ref, lnw_ref, lnb_ref,
                      y_ref, state_ref):
    c = CHUNK
    n = HEAD_DIM
    chunk_idx = pl.program_id(1)

    @pl.when(chunk_idx == 0)
    def _():
        state_ref[...] = jnp.zeros_like(state_ref)

    aa_mask, eye, diag2, levels, tri_incl = _chunk_masks()
    tri = jnp.where(tri_incl, 1.0, 0.0).astype(BF16)
    eye_f = jnp.where(eye, 1.0, 0.0)
    zeros_cn = jnp.zeros((c, n), F32)

    nb = r_ref.shape[0]
    items = [(bi, hd) for bi in range(nb) for hd in range(RWKV_HEADS)]
    sls = [slice(hd * n, (hd + 1) * n) for _, hd in items]
    ni = len(items)

    pre = []
    for bi in range(nb):
        lw = lw_ref[bi]
        cum = _split3_dot_left(tri, lw)
        tot = cum[c - 1:c, :]
        e_inv = jnp.exp(-cum)
        e_end = jnp.exp(tot - cum)
        r_all = r_ref[bi]
        k_all = k_ref[bi]
        b_all = b_ref[bi]
        pre.append(dict(
            rt=r_all * jnp.exp(cum), kkt=kk_ref[bi] * jnp.exp(cum - lw), bt=b_all * e_inv, kt=k_all * e_inv,
            bd=b_all * e_end, kd=k_all * e_end, e_tot=jnp.exp(tot), v=v_ref[bi], g=g_ref[bi],
            bonus=r_all * k_all * rk_ref[...]))

    rt = [pre[bi]["rt"][:, sl] for (bi, _), sl in zip(items, sls)]
    kkt = [pre[bi]["kkt"][:, sl] for (bi, _), sl in zip(items, sls)]
    vh = [pre[bi]["v"][:, sl] for (bi, _), sl in zip(items, sls)]
    aa = []
    for i, ((bi, _), sl) in enumerate(zip(items, sls)):
        lhs = jnp.concatenate([kkt[i], rt[i]], axis=0).astype(BF16)
        rhs = jnp.concatenate([pre[bi]["bt"][:, sl], pre[bi]["kt"][:, sl]], axis=0).astype(BF16)
        aa.append(jnp.where(aa_mask, _dot_nt(lhs, rhs), 0.0))
    av = [_dot(aa[i].astype(BF16), jnp.concatenate([zeros_cn, vh[i]], axis=0).astype(BF16))
          for i in range(ni)]
    a_ab = [a[:c, :c] for a in aa]

    tinv = [eye_f - jnp.where(diag2, a, 0.0) for a in a_ab]
    for lvl in levels:
        tb = [t_.astype(BF16) for t_ in tinv]
        half = [_dot(tb[i], jnp.where(lvl, a_ab[i], 0.0).astype(BF16)).astype(BF16) for i in range(ni)]
        tinv = [tinv[i] - _dot(half[i], tb[i]) for i in range(ni)]

    w_b = [_dot(tinv[i].astype(BF16), jnp.concatenate([kkt[i], av[i][:c]], axis=1).astype(BF16)).astype(BF16)
           for i in range(ni)]
    pq = [jnp.concatenate([rt[i], av[i][c:]], axis=1) - _dot(aa[i][c:, :c].astype(BF16), w_b[i])
          for i in range(ni)]
    mg = []
    for i, ((bi, _), sl) in enumerate(zip(items, sls)):
        neg_bd_kd = jnp.concatenate([-pre[bi]["bd"][:, sl], pre[bi]["kd"][:, sl]], axis=0).astype(BF16)
        wv = jnp.concatenate([w_b[i], jnp.concatenate([zeros_cn, vh[i]], axis=1).astype(BF16)], axis=0)
        mg.append(_dot_tn(neg_bd_kd, wv))

    st_b = [state_ref[i].astype(BF16) for i in range(ni)]
    ys = [_dot(pq[i][:, :n].astype(BF16), st_b[i]) + pq[i][:, n:] for i in range(ni)]
    for i, ((bi, _), sl) in enumerate(zip(items, sls)):
        m_mat = mg[i][:, :n] + jnp.where(eye, pre[bi]["e_tot"][:, sl], 0.0)
        state_ref[i] = _dot(m_mat.astype(BF16), st_b[i]) + mg[i][:, n:]

    outs = []
    for i, ((bi, _), sl) in enumerate(zip(items, sls)):
        y = ys[i]
        mean = jnp.mean(y, axis=-1, keepdims=True)
        yc = y - mean
        var = jnp.mean(yc * yc, axis=-1, keepdims=True)
        yn = yc * lax.rsqrt(var + GN_EPS) * lnw_ref[:, sl] + lnb_ref[:, sl]
        bonus = jnp.sum(pre[bi]["bonus"][:, sl], axis=-1, keepdims=True) * vh[i]
        outs.append((yn + bonus) * pre[bi]["g"][:, sl])
    for bi in range(nb):
        y_ref[bi] = jnp.concatenate(outs[bi * RWKV_HEADS:(bi + 1) * RWKV_HEADS], axis=1)


def _merge_kernel(x_ref, ya_ref, yb_ref, yc_ref, g_ref, wg_ref, bg_ref, wr_ref, wp_ref, wm_ref, wo_ref,
                  out_ref):
    x = x_ref[...]
    h = _rms_norm(x, g_ref[...]).astype(BF16)
    d = D_MODEL
    merged = None
    for i, (y_ref, w_ref) in enumerate(((ya_ref, wr_ref), (yb_ref, wp_ref), (yc_ref, wm_ref))):
        gate = jax.nn.sigmoid(_dot(h, wg_ref[:, i * d:(i + 1) * d]) + bg_ref[:, i * d:(i + 1) * d])
        term = gate * _dot(y_ref[...].astype(BF16), w_ref[...])
        merged = term if merged is None else merged + term
    out_ref[...] = x + _dot(merged.astype(BF16), wo_ref[...])


def _conv_ffn_kernel(x_ref, g_ref, win_ref, cw_ref, cb_ref, wout_ref, gf_ref, out_ref, tail_ref):
    t = pl.program_id(1)
    tm = x_ref.shape[0]
    x = x_ref[...]
    h = _rms_norm(x, g_ref[...]).astype(BF16)
    row = lax.broadcasted_iota(jnp.int32, (tm, 1), 0)
    first = t == 0
    acc = x
    for j in range(D_FF // FF_CHUNK):
        cs = slice(j * FF_CHUNK, (j + 1) * FF_CHUNK)
        u = _dot(h, win_ref[:, cs])
        gv = _dot(h, win_ref[:, D_FF + j * FF_CHUNK:D_FF + (j + 1) * FF_CHUNK])
        tail = jnp.where(first, 0.0, tail_ref[:, cs])
        tail_ref[:, cs] = u[tm - 8:, :]
        u1 = jnp.where(row == 0, tail[7:8, :], pltpu.roll(u, 1, axis=0))
        u2 = jnp.where(row == 0, tail[6:7, :], jnp.where(row == 1, tail[7:8, :], pltpu.roll(u, 2, axis=0)))
        uc = cw_ref[0:1, cs] * u2 + cw_ref[1:2, cs] * u1 + cw_ref[2:3, cs] * u + cb_ref[:, cs]
        act = 0.5 * uc * (1.0 + lax.erf(uc * (2.0 ** -0.5))) * gv
        acc = acc + _dot(act.astype(BF16), wout_ref[cs, :])
    out_ref[...] = _rms_norm(acc, gf_ref[...])


def _const_spec(shape):
    nd = len(shape)
    return pl.BlockSpec(shape, lambda *_: (0,) * nd)


def kernel(x, mem, norm_mix_g, w_in_mix, mu_shift, w0, w_lora_b, a0, a_lora_b, g_lora_b, k_k, k_a, r_k,
           ln_x_w, ln_x_b, pool_w, pool_scale, norm_mem_g, w_mem_kv, w_up_rwkv, w_up_pool, w_up_mem,
           w_gate, b_gate, w_o, norm_ffn_g, w_ffn_in, ffn_conv_w, ffn_conv_b, w_ffn_out, norm_final_g):
    bsz, seq, d = x.shape
    n_mem = mem.shape[1]
    assert d == D_MODEL and seq % TM_PROJ == 0 and seq % CHUNK == 0 and bsz % SCAN_BATCH == 0
    assert norm_mix_g.shape[0] == 1, "single-layer block"
    l = 0
    row = lambda a: a.reshape(1, -1).astype(F32)

    wmix = w_in_mix[l].astype(BF16)
    zpad = jnp.zeros((DECAY_LORA, RWKV_DIM), F32)
    wl_pad = jnp.concatenate([w_lora_b[l], zpad], axis=0).astype(BF16)
    al_pad = jnp.concatenate([zpad, a_lora_b[l]], axis=0).astype(BF16)
    gl = g_lora_b[l].astype(BF16)
    head_id = jnp.arange(RWKV_DIM) // HEAD_DIM
    hsum = (head_id[:, None] == head_id[None, :]).astype(BF16)
    pool_bd = jax.scipy.linalg.block_diag(*[pool_w[l, i] for i in range(len(POOL_WINDOWS))]).astype(BF16)

    kv = pl.pallas_call(
        _mem_kv_kernel,
        out_shape=jax.ShapeDtypeStruct((bsz, n_mem, 2 * MEM_DIM), BF16),
        grid=(bsz,),
        in_specs=[pl.BlockSpec((None, n_mem, d), lambda b: (b, 0, 0)),
                  _const_spec((1, d)), _const_spec((d, 2 * MEM_DIM))],
        out_specs=pl.BlockSpec((None, n_mem, 2 * MEM_DIM), lambda b: (b, 0, 0)),
        compiler_params=pltpu.CompilerParams(dimension_semantics=("arbitrary",)),
        name="mem_kv",
    )(mem, row(norm_mem_g[l]), w_mem_kv[l].astype(BF16))

    tok = lambda w: pl.BlockSpec((None, TM_PROJ, w), lambda b, t: (b, t, 0))
    f32_out = lambda w: jax.ShapeDtypeStruct((bsz, seq, w), F32)
    outs = pl.pallas_call(
        _mix_proj_kernel,
        out_shape=[f32_out(RWKV_DIM)] * 7 + [f32_out(POOL_DIM), f32_out(MEM_DIM)],
        grid=(bsz, seq // TM_PROJ),
        in_specs=[tok(d), _const_spec((1, d)), _const_spec((d, MIX_IN)), _const_spec((1, RWKV_IN)),
                  _const_spec((1, RWKV_DIM)), _const_spec((DECAY_LORA + ICLR_LORA, RWKV_DIM)),
                  _const_spec((1, RWKV_DIM)), _const_spec((DECAY_LORA + ICLR_LORA, RWKV_DIM)),
                  _const_spec((GATE_LORA, RWKV_DIM)), _const_spec((1, RWKV_DIM)), _const_spec((1, RWKV_DIM)),
                  _const_spec((RWKV_DIM, RWKV_DIM)), _const_spec((POOL_DIM, POOL_DIM)),
                  _const_spec((1, POOL_DIM)),
                  pl.BlockSpec((None, n_mem, 2 * MEM_DIM), lambda b, t: (b, 0, 0))],
        out_specs=[tok(RWKV_DIM)] * 7 + [tok(POOL_DIM), tok(MEM_DIM)],
        scratch_shapes=[pltpu.VMEM((1, RWKV_IN), F32), pltpu.VMEM((POOL_HALO, POOL_DIM), F32)],
        compiler_params=pltpu.CompilerParams(dimension_semantics=("arbitrary", "arbitrary"),
                                             vmem_limit_bytes=V7X_VMEM_LIMIT_BYTES),
        name="mix_proj",
    )(x, row(norm_mix_g[l]), wmix, row(mu_shift[l]), row(w0[l]), wl_pad, row(a0[l]), al_pad, gl,
      row(k_k[l]), row(k_a[l]), hsum, pool_bd, row(pool_scale[l]), kv)
    r_s, k_s, v_s, kk_s, b_s, lw_s, g_s, y_b, y_c = outs

    sb = SCAN_BATCH
    blk = pl.BlockSpec((sb, CHUNK, RWKV_DIM), lambda b, c: (b, c, 0))
    y_a = pl.pallas_call(
        _rwkv_scan_kernel,
        out_shape=f32_out(RWKV_DIM),
        grid=(bsz // sb, seq // CHUNK),
        in_specs=[blk] * 7 + [_const_spec((1, RWKV_DIM))] * 3,
        out_specs=blk,
        scratch_shapes=[pltpu.VMEM((sb * RWKV_HEADS, HEAD_DIM, HEAD_DIM), F32)],
        compiler_params=pltpu.CompilerParams(dimension_semantics=("arbitrary", "arbitrary")),
        name="rwkv_scan",
    )(r_s, k_s, v_s, kk_s, b_s, lw_s, g_s, row(r_k[l]), row(ln_x_w[l]), row(ln_x_b[l]))

    n_tok = bsz * seq
    flat = lambda a: a.reshape(n_tok, a.shape[-1])
    tokm = lambda w: pl.BlockSpec((TM_MERGE, w), lambda i: (i, 0))
    x1 = pl.pallas_call(
        _merge_kernel,
        out_shape=jax.ShapeDtypeStruct((n_tok, d), F32),
        grid=(n_tok // TM_MERGE,),
        in_specs=[tokm(d), tokm(RWKV_DIM), tokm(POOL_DIM), tokm(MEM_DIM), _const_spec((1, d)),
                  _const_spec((d, 3 * d)), _const_spec((1, 3 * d)), _const_spec((RWKV_DIM, d)),
                  _const_spec((POOL_DIM, d)), _const_spec((MEM_DIM, d)), _const_spec((d, d))],
        out_specs=tokm(d),
        compiler_params=pltpu.CompilerParams(dimension_semantics=("arbitrary",),
                                             vmem_limit_bytes=V7X_VMEM_LIMIT_BYTES),
        name="merge",
    )(flat(x), flat(y_a), flat(y_b), flat(y_c), row(norm_mix_g[l]), w_gate[l].astype(BF16),
      row(b_gate[l]), w_up_rwkv[l].astype(BF16), w_up_pool[l].astype(BF16), w_up_mem[l].astype(BF16),
      w_o[l].astype(BF16))

    tokf = pl.BlockSpec((None, TM_FFN, d), lambda b, t: (b, t, 0))
    single = pl.Buffered(1)
    out = pl.pallas_call(
        _conv_ffn_kernel,
        out_shape=jax.ShapeDtypeStruct((bsz, seq, d), F32),
        grid=(bsz, seq // TM_FFN),
        in_specs=[tokf, _const_spec((1, d)),
                  pl.BlockSpec((d, 2 * D_FF), lambda b, t: (0, 0), pipeline_mode=single),
                  _const_spec((3, D_FF)), _const_spec((1, D_FF)),
                  pl.BlockSpec((D_FF, d), lambda b, t: (0, 0), pipeline_mode=single),
                  _const_spec((1, d))],
        out_specs=tokf,
        scratch_shapes=[pltpu.VMEM((8, D_FF), F32)],
        compiler_params=pltpu.CompilerParams(dimension_semantics=("arbitrary", "arbitrary"),
                                             vmem_limit_bytes=V7X_VMEM_LIMIT_BYTES),
        name="conv_ffn",
    )(x1.reshape(bsz, seq, d), row(norm_ffn_g[l]), w_ffn_in[l].astype(BF16), ffn_conv_w[l].astype(F32),
      row(ffn_conv_b[l]), w_ffn_out[l].astype(BF16), row(norm_final_g))
    return out
```

```python
import functools
import math

import jax
import jax.numpy as jnp
from jax import lax
from jax.experimental import pallas as pl
from jax.experimental.pallas import tpu as pltpu

F32 = jnp.float32
BF16 = jnp.bfloat16

D_MODEL = 1024
HEAD_DIM = 64
RWKV_HEADS = 8
RWKV_DIM = RWKV_HEADS * HEAD_DIM
DECAY_LORA = 64
ICLR_LORA = 64
GATE_LORA = 128
RWKV_IN = 3 * RWKV_DIM + DECAY_LORA + ICLR_LORA + GATE_LORA
POOL_WINDOWS = (2, 4, 8, 16)
POOL_GROUP_DIM = 64
POOL_DIM = len(POOL_WINDOWS) * POOL_GROUP_DIM
POOL_HALO = 16
MEM_HEADS = 4
MEM_DIM = MEM_HEADS * HEAD_DIM
MIX_IN = RWKV_IN + POOL_DIM + MEM_DIM
D_FF = 2816
NORM_EPS = 1e-6
GN_EPS = 64e-5
CHUNK = 64
FF_CHUNK = 256
V7X_VMEM_LIMIT_BYTES = 56 * 1024 * 1024
LANES = 128

TM_PROJ = 512
TM_MERGE = 512
TM_FFN = 512
SCAN_BATCH = 2


def _dot(a, b):
    return jnp.dot(a, b, preferred_element_type=F32)


def _dot_nt(a, b):
    return lax.dot_general(a, b, (((1,), (1,)), ((), ())), preferred_element_type=F32)


def _dot_tn(a, b):
    return lax.dot_general(a, b, (((0,), (0,)), ((), ())), preferred_element_type=F32)


def _rms_norm(x, g):
    return x * lax.rsqrt(jnp.mean(x * x, axis=-1, keepdims=True) + NORM_EPS) * g


def _split_dot(x, w_bf16):
    hi = x.astype(BF16)
    lo = (x - hi.astype(F32)).astype(BF16)
    return _dot(hi, w_bf16) + _dot(lo, w_bf16)


def _split3_dot_left(w_bf16, x):
    hi = x.astype(BF16)
    r1 = x - hi.astype(F32)
    mid = r1.astype(BF16)
    lo = (r1 - mid.astype(F32)).astype(BF16)
    return _dot(w_bf16, hi) + _dot(w_bf16, mid) + _dot(w_bf16, lo)


def _mem_kv_kernel(mem_ref, g_ref, w_ref, kv_ref):
    m = _rms_norm(mem_ref[...], g_ref[...])
    kv_ref[...] = _dot(m.astype(BF16), w_ref[...]).astype(BF16)


def _mix_proj_kernel(x_ref, g_ref, wmix_ref, mu_ref, w0_ref, wl_ref, a0_ref, al_ref, gl_ref,
                     kk_ref, ka_ref, hsum_ref, poolw_ref, pools_ref, kv_ref,
                     r_out, k_out, v_out, kkn_out, b_out, lw_out, g_out, yb_out, yc_out,
                     prev_ref, halo_ref):
    t = pl.program_id(1)
    tm = x_ref.shape[0]
    first = t == 0

    h = _rms_norm(x_ref[...], g_ref[...]).astype(BF16)
    p = _dot(h, wmix_ref[...])

    ps = p[:, :RWKV_IN]
    row = lax.broadcasted_iota(jnp.int32, (tm, 1), 0)
    carry = jnp.where(first, 0.0, prev_ref[...])
    prev = jnp.where(row == 0, carry, pltpu.roll(ps, 1, axis=0))
    prev_ref[...] = ps[tm - 1:tm, :]
    ps = ps + (prev - ps) * mu_ref[...]

    r = ps[:, :RWKV_DIM]
    k = ps[:, RWKV_DIM:2 * RWKV_DIM]
    v = ps[:, 2 * RWKV_DIM:3 * RWKV_DIM]
    wa = ps[:, 3 * RWKV_DIM:3 * RWKV_DIM + DECAY_LORA + ICLR_LORA]
    gd = ps[:, 3 * RWKV_DIM + DECAY_LORA + ICLR_LORA:RWKV_IN]

    z = w0_ref[...] + _dot(jnp.tanh(wa).astype(BF16), wl_ref[...])
    lw_out[...] = (-math.exp(-0.5)) * jax.nn.sigmoid(z)
    a = jax.nn.sigmoid(a0_ref[...] + _dot(wa.astype(BF16), al_ref[...]))
    g_out[...] = _dot(jax.nn.sigmoid(gd).astype(BF16), gl_ref[...])

    kkr = k * kk_ref[...]
    kk_sq = kkr * kkr
    ssq = jnp.concatenate([_split_dot(kk_sq[:, j:j + LANES], hsum_ref[...])
                           for j in range(0, RWKV_DIM, LANES)], axis=1)
    kkn = kkr * lax.rsqrt(ssq + 1e-12)
    r_out[...] = r
    v_out[...] = v
    k_out[...] = k * (1.0 + (a - 1.0) * ka_ref[...])
    kkn_out[...] = kkn
    b_out[...] = kkn * a

    pp = p[:, RWKV_IN:RWKV_IN + POOL_DIM]
    halo = jnp.where(first, 0.0, halo_ref[...])
    halo_ref[...] = pp[tm - POOL_HALO:, :]
    ext = jnp.concatenate([halo, pp], axis=0)
    lane_group = lax.broadcasted_iota(jnp.int32, (1, POOL_DIM), 1) // POOL_GROUP_DIM
    win = jnp.zeros((1, POOL_DIM), jnp.int32)
    sel = jnp.zeros((tm, POOL_DIM), F32)
    acc = ext
    shift = 1
    for gi, w in enumerate(POOL_WINDOWS):
        while shift < w:
            acc = acc + pltpu.roll(acc, shift, axis=0)
            shift *= 2
        sel = jnp.where(lane_group == gi, acc[POOL_HALO:, :], sel)
        win = jnp.where(lane_group == gi, w, win)
    pos = t * tm + row + 1
    cnt = jnp.minimum(pos, win).astype(F32)
    dpool = sel / cnt - pp
    yb_out[...] = _dot(dpool.astype(BF16), poolw_ref[...]) * pools_ref[...]

    q = p[:, RWKV_IN + POOL_DIM:]
    kmem = kv_ref[:, :MEM_DIM]
    vmem = kv_ref[:, MEM_DIM:]
    lane_head = lax.broadcasted_iota(jnp.int32, (1, MEM_DIM), 1) // HEAD_DIM
    cols = []
    for j in range(0, MEM_DIM, LANES):
        col = None
        for hd in range(j // HEAD_DIM, (j + LANES) // HEAD_DIM):
            in_head = lane_head == hd
            qh = jnp.where(in_head, q, 0.0).astype(BF16)
            s = _dot_nt(qh, kmem) * (HEAD_DIM ** -0.5)
            e = jnp.exp(s - jnp.max(s, axis=-1, keepdims=True))
            prob = e / jnp.sum(e, axis=-1, keepdims=True)
            vh = jnp.where(in_head[:, j:j + LANES], vmem[:, j:j + LANES], 0.0)
            o = _dot(prob.astype(BF16), vh)
            col = o if col is None else col + o
        cols.append(col)
    yc_out[...] = jnp.concatenate(cols, axis=1)


def _chunk_masks():
    c = CHUNK
    ri = lax.broadcasted_iota(jnp.int32, (2 * c, 2 * c), 0)
    ci = lax.broadcasted_iota(jnp.int32, (2 * c, 2 * c), 1)
    rt = ri % c
    ct = ci % c
    aa_mask = (rt > ct) | ((ri >= c) & (rt == ct))
    r64 = lax.broadcasted_iota(jnp.int32, (c, c), 0)
    c64 = lax.broadcasted_iota(jnp.int32, (c, c), 1)
    eye = r64 == c64
    levels = []
    blk = 2
    while blk < c:
        same_outer = (r64 // (2 * blk)) == (c64 // (2 * blk))
        diff_inner = (r64 // blk) != (c64 // blk)
        levels.append(same_outer & diff_inner & (r64 > c64))
        blk *= 2
    diag2 = ((r64 // 2) == (c64 // 2)) & (r64 > c64)
    tri_incl = (r64 >= c64)
    return aa_mask, eye, diag2, levels, tri_incl


def _rwkv_scan_kernel(r_ref, k_ref, v_ref, kk_ref, b_ref, lw_ref, g_ref, rk_ref, lnw_ref, lnb_ref,
                      y_ref, state_ref):
    c = CHUNK
    n = HEAD_DIM
    chunk_idx = pl.program_id(1)

    @pl.when(chunk_idx == 0)
    def _():
        state_ref[...] = jnp.zeros_like(state_ref)

    aa_mask, eye, diag2, levels, tri_incl = _chunk_masks()
    tri = jnp.where(tri_incl, 1.0, 0.0).astype(BF16)
    eye_f = jnp.where(eye, 1.0, 0.0)
    zeros_cn = jnp.zeros((c, n), F32)

    nb = r_ref.shape[0]
    items = [(bi, hd) for bi in range(nb) for hd in range(RWKV_HEADS)]
    sls = [slice(hd * n, (hd + 1) * n) for _, hd in items]
    ni = len(items)

    pre = []
    for bi in range(nb):
        lw = lw_ref[bi]
        cum = _split3_dot_left(tri, lw)
        tot = cum[c - 1:c, :]
        e_inv = jnp.exp(-cum)
        e_end = jnp.exp(tot - cum)
        r_all = r_ref[bi]
        k_all = k_ref[bi]
        b_all = b_ref[bi]
        pre.append(dict(
            rt=r_all * jnp.exp(cum), kkt=kk_ref[bi] * jnp.exp(cum - lw), bt=b_all * e_inv, kt=k_all * e_inv,
            bd=b_all * e_end, kd=k_all * e_end, e_tot=jnp.exp(tot), v=v_ref[bi], g=g_ref[bi],
            bonus=r_all * k_all * rk_ref[...]))

    rt = [pre[bi]["rt"][:, sl] for (bi, _), sl in zip(items, sls)]
    kkt = [pre[bi]["kkt"][:, sl] for (bi, _), sl in zip(items, sls)]
    vh = [pre[bi]["v"][:, sl] for (bi, _), sl in zip(items, sls)]
    aa = []
    for i, ((bi, _), sl) in enumerate(zip(items, sls)):
        lhs = jnp.concatenate([kkt[i], rt[i]], axis=0).astype(BF16)
        rhs = jnp.concatenate([pre[bi]["bt"][:, sl], pre[bi]["kt"][:, sl]], axis=0).astype(BF16)
        aa.append(jnp.where(aa_mask, _dot_nt(lhs, rhs), 0.0))
    av = [_dot(aa[i].astype(BF16), jnp.concatenate([zeros_cn, vh[i]], axis=0).astype(BF16))
          for i in range(ni)]
    a_ab = [a[:c, :c] for a in aa]

    tinv = [eye_f - jnp.where(diag2, a, 0.0) for a in a_ab]
    for lvl in levels:
        tb = [t_.astype(BF16) for t_ in tinv]
        half = [_dot(tb[i], jnp.where(lvl, a_ab[i], 0.0).astype(BF16)).astype(BF16) for i in range(ni)]
        tinv = [tinv[i] - _dot(half[i], tb[i]) for i in range(ni)]

    w_b = [_dot(tinv[i].astype(BF16), jnp.concatenate([kkt[i], av[i][:c]], axis=1).astype(BF16)).astype(BF16)
           for i in range(ni)]
    pq = [jnp.concatenate([rt[i], av[i][c:]], axis=1) - _dot(aa[i][c:, :c].astype(BF16), w_b[i])
          for i in range(ni)]
    mg = []
    for i, ((bi, _), sl) in enumerate(zip(items, sls)):
        neg_bd_kd = jnp.concatenate([-pre[bi]["bd"][:, sl], pre[bi]["kd"][:, sl]], axis=0).astype(BF16)
        wv = jnp.concatenate([w_b[i], jnp.concatenate([zeros_cn, vh[i]], axis=1).astype(BF16)], axis=0)
        mg.append(_dot_tn(neg_bd_kd, wv))

    st_b = [state_ref[i].astype(BF16) for i in range(ni)]
    ys = [_dot(pq[i][:, :n].astype(BF16), st_b[i]) + pq[i][:, n:] for i in range(ni)]
    for i, ((bi, _), sl) in enumerate(zip(items, sls)):
        m_mat = mg[i][:, :n] + jnp.where(eye, pre[bi]["e_tot"][:, sl], 0.0)
        state_ref[i] = _dot(m_mat.astype(BF16), st_b[i]) + mg[i][:, n:]

    outs = []
    for i, ((bi, _), sl) in enumerate(zip(items, sls)):
        y = ys[i]
        mean = jnp.mean(y, axis=-1, keepdims=True)
        yc = y - mean
        var = jnp.mean(yc * yc, axis=-1, keepdims=True)
        yn = yc * lax.rsqrt(var + GN_EPS) * lnw_ref[:, sl] + lnb_ref[:, sl]
        bonus = jnp.sum(pre[bi]["bonus"][:, sl], axis=-1, keepdims=True) * vh[i]
        outs.append((yn + bonus) * pre[bi]["g"][:, sl])
    for bi in range(nb):
        y_ref[bi] = jnp.concatenate(outs[bi * RWKV_HEADS:(bi + 1) * RWKV_HEADS], axis=1)


def _merge_kernel(x_ref, ya_ref, yb_ref, yc_ref, g_ref, wg_ref, bg_ref, wr_ref, wp_ref, wm_ref, wo_ref,
                  out_ref):
    x = x_ref[...]
    h = _rms_norm(x, g_ref[...]).astype(BF16)
    d = D_MODEL
    merged = None
    for i, (y_ref, w_ref) in enumerate(((ya_ref, wr_ref), (yb_ref, wp_ref), (yc_ref, wm_ref))):
        gate = jax.nn.sigmoid(_dot(h, wg_ref[:, i * d:(i + 1) * d]) + bg_ref[:, i * d:(i + 1) * d])
        term = gate * _dot(y_ref[...].astype(BF16), w_ref[...])
        merged = term if merged is None else merged + term
    out_ref[...] = x + _dot(merged.astype(BF16), wo_ref[...])


def _conv_ffn_kernel(x_ref, g_ref, win_ref, cw_ref, cb_ref, wout_ref, gf_ref, out_ref, tail_ref):
    t = pl.program_id(1)
    tm = x_ref.shape[0]
    x = x_ref[...]
    h = _rms_norm(x, g_ref[...]).astype(BF16)
    row = lax.broadcasted_iota(jnp.int32, (tm, 1), 0)
    first = t == 0
    acc = x
    n_chunks = D_FF // FF_CHUNK

    def up_proj(j):
        return (_dot(h, win_ref[:, j * FF_CHUNK:(j + 1) * FF_CHUNK]),
                _dot(h, win_ref[:, D_FF + j * FF_CHUNK:D_FF + (j + 1) * FF_CHUNK]))

    nxt = up_proj(0)
    for j in range(n_chunks):
        cs = slice(j * FF_CHUNK, (j + 1) * FF_CHUNK)
        u, gv = nxt
        if j + 1 < n_chunks:
            nxt = up_proj(j + 1)
        tail = jnp.where(first, 0.0, tail_ref[:, cs])
        tail_ref[:, cs] = u[tm - 8:, :]
        u1 = jnp.where(row == 0, tail[7:8, :], pltpu.roll(u, 1, axis=0))
        u2 = jnp.where(row == 0, tail[6:7, :], jnp.where(row == 1, tail[7:8, :], pltpu.roll(u, 2, axis=0)))
        uc = cw_ref[0:1, cs] * u2 + cw_ref[1:2, cs] * u1 + cw_ref[2:3, cs] * u + cb_ref[:, cs]
        act = 0.5 * uc * (1.0 + lax.erf(uc * (2.0 ** -0.5))) * gv
        acc = acc + _dot(act.astype(BF16), wout_ref[cs, :])
    out_ref[...] = _rms_norm(acc, gf_ref[...])


def _const_spec(shape):
    nd = len(shape)
    return pl.BlockSpec(shape, lambda *_: (0,) * nd)


def kernel(x, mem, norm_mix_g, w_in_mix, mu_shift, w0, w_lora_b, a0, a_lora_b, g_lora_b, k_k, k_a, r_k,
           ln_x_w, ln_x_b, pool_w, pool_scale, norm_mem_g, w_mem_kv, w_up_rwkv, w_up_pool, w_up_mem,
           w_gate, b_gate, w_o, norm_ffn_g, w_ffn_in, ffn_conv_w, ffn_conv_b, w_ffn_out, norm_final_g):
    bsz, seq, d = x.shape
    n_mem = mem.shape[1]
    assert d == D_MODEL and seq % TM_PROJ == 0 and seq % CHUNK == 0 and bsz % SCAN_BATCH == 0
    assert norm_mix_g.shape[0] == 1, "single-layer block"
    l = 0
    row = lambda a: a.reshape(1, -1).astype(F32)

    wmix = w_in_mix[l].astype(BF16)
    zpad = jnp.zeros((DECAY_LORA, RWKV_DIM), F32)
    wl_pad = jnp.concatenate([w_lora_b[l], zpad], axis=0).astype(BF16)
    al_pad = jnp.concatenate([zpad, a_lora_b[l]], axis=0).astype(BF16)
    gl = g_lora_b[l].astype(BF16)
    head_id = jnp.arange(LANES) // HEAD_DIM
    hsum = (head_id[:, None] == head_id[None, :]).astype(BF16)
    pool_bd = jax.scipy.linalg.block_diag(*[pool_w[l, i] for i in range(len(POOL_WINDOWS))]).astype(BF16)

    kv = pl.pallas_call(
        _mem_kv_kernel,
        out_shape=jax.ShapeDtypeStruct((bsz, n_mem, 2 * MEM_DIM), BF16),
        grid=(bsz,),
        in_specs=[pl.BlockSpec((None, n_mem, d), lambda b: (b, 0, 0)),
                  _const_spec((1, d)), _const_spec((d, 2 * MEM_DIM))],
        out_specs=pl.BlockSpec((None, n_mem, 2 * MEM_DIM), lambda b: (b, 0, 0)),
        compiler_params=pltpu.CompilerParams(dimension_semantics=("arbitrary",)),
        name="mem_kv",
    )(mem, row(norm_mem_g[l]), w_mem_kv[l].astype(BF16))

    tok = lambda w: pl.BlockSpec((None, TM_PROJ, w), lambda b, t: (b, t, 0))
    f32_out = lambda w: jax.ShapeDtypeStruct((bsz, seq, w), F32)
    outs = pl.pallas_call(
        _mix_proj_kernel,
        out_shape=[f32_out(RWKV_DIM)] * 7 + [f32_out(POOL_DIM), f32_out(MEM_DIM)],
        grid=(bsz, seq // TM_PROJ),
        in_specs=[tok(d), _const_spec((1, d)), _const_spec((d, MIX_IN)), _const_spec((1, RWKV_IN)),
                  _const_spec((1, RWKV_DIM)), _const_spec((DECAY_LORA + ICLR_LORA, RWKV_DIM)),
                  _const_spec((1, RWKV_DIM)), _const_spec((DECAY_LORA + ICLR_LORA, RWKV_DIM)),
                  _const_spec((GATE_LORA, RWKV_DIM)), _const_spec((1, RWKV_DIM)), _const_spec((1, RWKV_DIM)),
                  _const_spec((LANES, LANES)), _const_spec((POOL_DIM, POOL_DIM)),
                  _const_spec((1, POOL_DIM)),
                  pl.BlockSpec((None, n_mem, 2 * MEM_DIM), lambda b, t: (b, 0, 0))],
        out_specs=[tok(RWKV_DIM)] * 7 + [tok(POOL_DIM), tok(MEM_DIM)],
        scratch_shapes=[pltpu.VMEM((1, RWKV_IN), F32), pltpu.VMEM((POOL_HALO, POOL_DIM), F32)],
        compiler_params=pltpu.CompilerParams(dimension_semantics=("arbitrary", "arbitrary"),
                                             vmem_limit_bytes=V7X_VMEM_LIMIT_BYTES),
        name="mix_proj",
    )(x, row(norm_mix_g[l]), wmix, row(mu_shift[l]), row(w0[l]), wl_pad, row(a0[l]), al_pad, gl,
      row(k_k[l]), row(k_a[l]), hsum, pool_bd, row(pool_scale[l]), kv)
    r_s, k_s, v_s, kk_s, b_s, lw_s, g_s, y_b, y_c = outs

    sb = SCAN_BATCH
    blk = pl.BlockSpec((sb, CHUNK, RWKV_DIM), lambda b, c: (b, c, 0))
    y_a = pl.pallas_call(
        _rwkv_scan_kernel,
        out_shape=f32_out(RWKV_DIM),
        grid=(bsz // sb, seq // CHUNK),
        in_specs=[blk] * 7 + [_const_spec((1, RWKV_DIM))] * 3,
        out_specs=blk,
        scratch_shapes=[pltpu.VMEM((sb * RWKV_HEADS, HEAD_DIM, HEAD_DIM), F32)],
        compiler_params=pltpu.CompilerParams(dimension_semantics=("arbitrary", "arbitrary")),
        name="rwkv_scan",
    )(r_s, k_s, v_s, kk_s, b_s, lw_s, g_s, row(r_k[l]), row(ln_x_w[l]), row(ln_x_b[l]))

    n_tok = bsz * seq
    flat = lambda a: a.reshape(n_tok, a.shape[-1])
    tokm = lambda w: pl.BlockSpec((TM_MERGE, w), lambda i: (i, 0))
    x1 = pl.pallas_call(
        _merge_kernel,
        out_shape=jax.ShapeDtypeStruct((n_tok, d), F32),
        grid=(n_tok // TM_MERGE,),
        in_specs=[tokm(d), tokm(RWKV_DIM), tokm(POOL_DIM), tokm(MEM_DIM), _const_spec((1, d)),
                  _const_spec((d, 3 * d)), _const_spec((1, 3 * d)), _const_spec((RWKV_DIM, d)),
                  _const_spec((POOL_DIM, d)), _const_spec((MEM_DIM, d)), _const_spec((d, d))],
        out_specs=tokm(d),
        compiler_params=pltpu.CompilerParams(dimension_semantics=("arbitrary",),
                                             vmem_limit_bytes=V7X_VMEM_LIMIT_BYTES),
        name="merge",
    )(flat(x), flat(y_a), flat(y_b), flat(y_c), row(norm_mix_g[l]), w_gate[l].astype(BF16),
      row(b_gate[l]), w_up_rwkv[l].astype(BF16), w_up_pool[l].astype(BF16), w_up_mem[l].astype(BF16),
      w_o[l].astype(BF16))

    tokf = pl.BlockSpec((None, TM_FFN, d), lambda b, t: (b, t, 0))
    single = pl.Buffered(1)
    out = pl.pallas_call(
        _conv_ffn_kernel,
        out_shape=jax.ShapeDtypeStruct((bsz, seq, d), F32),
        grid=(bsz, seq // TM_FFN),
        in_specs=[tokf, _const_spec((1, d)),
                  pl.BlockSpec((d, 2 * D_FF), lambda b, t: (0, 0), pipeline_mode=single),
                  _const_spec((3, D_FF)), _const_spec((1, D_FF)),
                  pl.BlockSpec((D_FF, d), lambda b, t: (0, 0), pipeline_mode=single),
                  _const_spec((1, d))],
        out_specs=tokf,
        scratch_shapes=[pltpu.VMEM((8, D_FF), F32)],
        compiler_params=pltpu.CompilerParams(dimension_semantics=("arbitrary", "arbitrary"),
                                             vmem_limit_bytes=V7X_VMEM_LIMIT_BYTES),
        name="conv_ffn",
    )(x1.reshape(bsz, seq, d), row(norm_ffn_g[l]), w_ffn_in[l].astype(BF16), ffn_conv_w[l].astype(F32),
      row(ffn_conv_b[l]), w_ffn_out[l].astype(BF16), row(norm_final_g))
    return out
```

```python
import functools
import math

import jax
import jax.numpy as jnp
from jax import lax
from jax.experimental import pallas as pl
from jax.experimental.pallas import tpu as pltpu

F32 = jnp.float32
BF16 = jnp.bfloat16

D_MODEL = 1024
HEAD_DIM = 64
RWKV_HEADS = 8
RWKV_DIM = RWKV_HEADS * HEAD_DIM
DECAY_LORA = 64
ICLR_LORA = 64
GATE_LORA = 128
RWKV_IN = 3 * RWKV_DIM + DECAY_LORA + ICLR_LORA + GATE_LORA
POOL_WINDOWS = (2, 4, 8, 16)
POOL_GROUP_DIM = 64
POOL_DIM = len(POOL_WINDOWS) * POOL_GROUP_DIM
POOL_HALO = 16
MEM_HEADS = 4
MEM_DIM = MEM_HEADS * HEAD_DIM
MIX_IN = RWKV_IN + POOL_DIM + MEM_DIM
D_FF = 2816
NORM_EPS = 1e-6
GN_EPS = 64e-5
CHUNK = 64
FF_CHUNK = 256
V7X_VMEM_LIMIT_BYTES = 56 * 1024 * 1024
LANES = 128

TM_PROJ = 512
TM_MERGE = 512
TM_FFN = 512
SCAN_BATCH = 4


def _dot(a, b):
    return jnp.dot(a, b, preferred_element_type=F32)


def _dot_nt(a, b):
    return lax.dot_general(a, b, (((1,), (1,)), ((), ())), preferred_element_type=F32)


def _dot_tn(a, b):
    return lax.dot_general(a, b, (((0,), (0,)), ((), ())), preferred_element_type=F32)


def _rms_norm(x, g):
    return x * lax.rsqrt(jnp.mean(x * x, axis=-1, keepdims=True) + NORM_EPS) * g


def _split_dot(x, w_bf16):
    hi = x.astype(BF16)
    lo = (x - hi.astype(F32)).astype(BF16)
    return _dot(hi, w_bf16) + _dot(lo, w_bf16)


def _split3_dot_left(w_bf16, x):
    hi = x.astype(BF16)
    r1 = x - hi.astype(F32)
    mid = r1.astype(BF16)
    lo = (r1 - mid.astype(F32)).astype(BF16)
    return _dot(w_bf16, hi) + _dot(w_bf16, mid) + _dot(w_bf16, lo)


def _mem_kv_kernel(mem_ref, g_ref, w_ref, kv_ref):
    m = _rms_norm(mem_ref[...], g_ref[...])
    kv_ref[...] = _dot(m.astype(BF16), w_ref[...]).astype(BF16)


def _mix_proj_kernel(x_ref, g_ref, wmix_ref, mu_ref, w0_ref, wl_ref, a0_ref, al_ref, gl_ref,
                     kk_ref, ka_ref, hsum_ref, poolw_ref, pools_ref, kv_ref,
                     r_out, k_out, v_out, kkn_out, b_out, lw_out, g_out, yb_out, yc_out,
                     prev_ref, halo_ref):
    t = pl.program_id(1)
    tm = x_ref.shape[0]
    first = t == 0

    h = _rms_norm(x_ref[...], g_ref[...]).astype(BF16)
    p = _dot(h, wmix_ref[...])

    ps = p[:, :RWKV_IN]
    row = lax.broadcasted_iota(jnp.int32, (tm, 1), 0)
    carry = jnp.where(first, 0.0, prev_ref[...])
    prev = jnp.where(row == 0, carry, pltpu.roll(ps, 1, axis=0))
    prev_ref[...] = ps[tm - 1:tm, :]
    ps = ps + (prev - ps) * mu_ref[...]

    r = ps[:, :RWKV_DIM]
    k = ps[:, RWKV_DIM:2 * RWKV_DIM]
    v = ps[:, 2 * RWKV_DIM:3 * RWKV_DIM]
    wa = ps[:, 3 * RWKV_DIM:3 * RWKV_DIM + DECAY_LORA + ICLR_LORA]
    gd = ps[:, 3 * RWKV_DIM + DECAY_LORA + ICLR_LORA:RWKV_IN]

    z = w0_ref[...] + _dot(jnp.tanh(wa).astype(BF16), wl_ref[...])
    lw_out[...] = (-math.exp(-0.5)) * jax.nn.sigmoid(z)
    a = jax.nn.sigmoid(a0_ref[...] + _dot(wa.astype(BF16), al_ref[...]))
    g_out[...] = _dot(jax.nn.sigmoid(gd).astype(BF16), gl_ref[...])

    kkr = k * kk_ref[...]
    kk_sq = kkr * kkr
    ssq = jnp.concatenate([_split_dot(kk_sq[:, j:j + LANES], hsum_ref[...])
                           for j in range(0, RWKV_DIM, LANES)], axis=1)
    kkn = kkr * lax.rsqrt(ssq + 1e-12)
    r_out[...] = r
    v_out[...] = v
    k_out[...] = k * (1.0 + (a - 1.0) * ka_ref[...])
    kkn_out[...] = kkn
    b_out[...] = kkn * a

    pp = p[:, RWKV_IN:RWKV_IN + POOL_DIM]
    halo = jnp.where(first, 0.0, halo_ref[...])
    halo_ref[...] = pp[tm - POOL_HALO:, :]
    ext = jnp.concatenate([halo, pp], axis=0)
    lane_group = lax.broadcasted_iota(jnp.int32, (1, POOL_DIM), 1) // POOL_GROUP_DIM
    win = jnp.zeros((1, POOL_DIM), jnp.int32)
    sel = jnp.zeros((tm, POOL_DIM), F32)
    acc = ext
    shift = 1
    for gi, w in enumerate(POOL_WINDOWS):
        while shift < w:
            acc = acc + pltpu.roll(acc, shift, axis=0)
            shift *= 2
        sel = jnp.where(lane_group == gi, acc[POOL_HALO:, :], sel)
        win = jnp.where(lane_group == gi, w, win)
    pos = t * tm + row + 1
    cnt = jnp.minimum(pos, win).astype(F32)
    dpool = sel / cnt - pp
    yb_out[...] = _dot(dpool.astype(BF16), poolw_ref[...]) * pools_ref[...]

    q = p[:, RWKV_IN + POOL_DIM:]
    kmem = kv_ref[:, :MEM_DIM]
    vmem = kv_ref[:, MEM_DIM:]
    lane_head = lax.broadcasted_iota(jnp.int32, (1, MEM_DIM), 1) // HEAD_DIM
    cols = []
    for j in range(0, MEM_DIM, LANES):
        col = None
        for hd in range(j // HEAD_DIM, (j + LANES) // HEAD_DIM):
            in_head = lane_head == hd
            qh = jnp.where(in_head, q, 0.0).astype(BF16)
            s = _dot_nt(qh, kmem) * (HEAD_DIM ** -0.5)
            e = jnp.exp(s - jnp.max(s, axis=-1, keepdims=True))
            prob = e / jnp.sum(e, axis=-1, keepdims=True)
            vh = jnp.where(in_head[:, j:j + LANES], vmem[:, j:j + LANES], 0.0)
            o = _dot(prob.astype(BF16), vh)
            col = o if col is None else col + o
        cols.append(col)
    yc_out[...] = jnp.concatenate(cols, axis=1)


def _chunk_masks():
    c = CHUNK
    ri = lax.broadcasted_iota(jnp.int32, (2 * c, 2 * c), 0)
    ci = lax.broadcasted_iota(jnp.int32, (2 * c, 2 * c), 1)
    rt = ri % c
    ct = ci % c
    aa_mask = (rt > ct) | ((ri >= c) & (rt == ct))
    r64 = lax.broadcasted_iota(jnp.int32, (c, 2 * c), 0)
    c64 = lax.broadcasted_iota(jnp.int32, (c, 2 * c), 1) % c
    eye = r64 == c64
    levels = []
    blk = 2
    while blk < c:
        same_outer = (r64 // (2 * blk)) == (c64 // (2 * blk))
        diff_inner = (r64 // blk) != (c64 // blk)
        levels.append(same_outer & diff_inner & (r64 > c64))
        blk *= 2
    diag2 = ((r64 // 2) == (c64 // 2)) & (r64 > c64)
    tri_incl = lax.broadcasted_iota(jnp.int32, (c, c), 0) >= lax.broadcasted_iota(jnp.int32, (c, c), 1)
    block_diag = (ri < c) == (ci < c)
    eye2 = ri == ci
    return aa_mask, eye, diag2, levels, tri_incl, block_diag, eye2


def _rwkv_scan_kernel(r_ref, k_ref, v_ref, kk_ref, b_ref, lw_ref, g_ref, rk_ref, lnw_ref, lnb_ref,
                      y_ref, state_ref):
    c = CHUNK
    n = HEAD_DIM
    chunk_idx = pl.program_id(1)

    @pl.when(chunk_idx == 0)
    def _():
        state_ref[...] = jnp.zeros_like(state_ref)

    aa_mask, eye, diag2, levels, tri_incl, block_diag, eye2 = _chunk_masks()
    tri = jnp.where(tri_incl, 1.0, 0.0).astype(BF16)
    eye_f = jnp.where(eye, 1.0, 0.0)
    zeros_cl = jnp.zeros((c, LANES), F32)
    head_a = lax.broadcasted_iota(jnp.int32, (c, LANES), 1) < n
    head_a2 = lax.broadcasted_iota(jnp.int32, (2 * c, LANES), 1) < n

    def both(x):
        return jnp.concatenate([jnp.where(head_a, x, 0.0), jnp.where(head_a, 0.0, x)], axis=0).astype(BF16)

    def head_sums(x):
        sa = jnp.sum(jnp.where(head_a, x, 0.0), axis=-1, keepdims=True)
        sb = jnp.sum(jnp.where(head_a, 0.0, x), axis=-1, keepdims=True)
        return jnp.where(head_a, sa, sb)

    nb = r_ref.shape[0]
    items = [(bi, j) for bi in range(nb) for j in range(RWKV_DIM // LANES)]
    sls = [slice(j * LANES, (j + 1) * LANES) for _, j in items]
    ni = len(items)

    pre = []
    for bi in range(nb):
        lw = lw_ref[bi]
        cum = _split3_dot_left(tri, lw)
        tot = cum[c - 1:c, :]
        e_inv = jnp.exp(-cum)
        e_end = jnp.exp(tot - cum)
        r_all = r_ref[bi]
        k_all = k_ref[bi]
        b_all = b_ref[bi]
        pre.append(dict(
            rt=r_all * jnp.exp(cum), kkt=kk_ref[bi] * jnp.exp(cum - lw), bt=b_all * e_inv, kt=k_all * e_inv,
            bd=b_all * e_end, kd=k_all * e_end, e_tot=jnp.exp(tot), v=v_ref[bi], g=g_ref[bi],
            bonus=r_all * k_all * rk_ref[...]))

    rt = [pre[bi]["rt"][:, sl] for (bi, _), sl in zip(items, sls)]
    kkt = [pre[bi]["kkt"][:, sl] for (bi, _), sl in zip(items, sls)]
    vh = [pre[bi]["v"][:, sl] for (bi, _), sl in zip(items, sls)]
    aa_a, aa_b = [], []
    for i, ((bi, _), sl) in enumerate(zip(items, sls)):
        lhs = jnp.concatenate([kkt[i], rt[i]], axis=0)
        bt, kt = pre[bi]["bt"][:, sl], pre[bi]["kt"][:, sl]
        rhs_a = jnp.concatenate([bt, kt], axis=0).astype(BF16)
        rhs_b = jnp.concatenate([kt, bt], axis=0).astype(BF16)
        aa_a.append(jnp.where(aa_mask, _dot_nt(jnp.where(head_a2, lhs, 0.0).astype(BF16), rhs_a), 0.0))
        aa_b.append(jnp.where(aa_mask, _dot_nt(jnp.where(head_a2, 0.0, lhs).astype(BF16), rhs_b), 0.0))
    av = []
    for i in range(ni):
        v_rows = jnp.concatenate([zeros_cl, jnp.where(head_a, vh[i], 0.0),
                                  jnp.where(head_a, 0.0, vh[i]), zeros_cl], axis=0).astype(BF16)
        av.append(_dot(jnp.concatenate([aa_a[i], aa_b[i]], axis=1).astype(BF16), v_rows))
    a_ab = [jnp.where(head_a, aa_a[i][:c], aa_b[i][:c]) for i in range(ni)]
    a_rb = [jnp.where(head_a, aa_a[i][c:], aa_b[i][c:]).astype(BF16) for i in range(ni)]

    tinv = [eye_f - jnp.where(diag2, a, 0.0) for a in a_ab]
    for lvl in levels:
        t_bd = [both(t_) for t_ in tinv]
        half = [_dot(tinv[i].astype(BF16), both(jnp.where(lvl, a_ab[i], 0.0))).astype(BF16) for i in range(ni)]
        tinv = [tinv[i] - _dot(half[i], t_bd[i]) for i in range(ni)]

    w12 = [_dot(tinv[i].astype(BF16), jnp.concatenate([both(kkt[i]), both(av[i][:c])], axis=1))
           for i in range(ni)]
    pq = [jnp.concatenate([rt[i], av[i][c:]], axis=1)
          - _dot(a_rb[i], jnp.concatenate([both(w12[i][:, :LANES]), both(w12[i][:, LANES:])], axis=1))
          for i in range(ni)]
    mg = []
    for i, ((bi, _), sl) in enumerate(zip(items, sls)):
        neg_bd_kd = jnp.concatenate([-pre[bi]["bd"][:, sl], pre[bi]["kd"][:, sl]], axis=0).astype(BF16)
        wv = jnp.concatenate([w12[i], jnp.concatenate([zeros_cl, vh[i]], axis=1)], axis=0).astype(BF16)
        mg.append(_dot_tn(neg_bd_kd, wv))

    st_b = [state_ref[i].astype(BF16) for i in range(ni)]
    ys = [_dot(pq[i][:, :LANES].astype(BF16), st_b[i]) + pq[i][:, LANES:] for i in range(ni)]
    for i, ((bi, _), sl) in enumerate(zip(items, sls)):
        m_mat = jnp.where(block_diag, mg[i][:, :LANES], 0.0) + jnp.where(eye2, pre[bi]["e_tot"][:, sl], 0.0)
        state_ref[i] = _dot(m_mat.astype(BF16), st_b[i]) + jnp.where(block_diag, mg[i][:, LANES:], 0.0)

    inv_n = 1.0 / n
    for i, ((bi, _), sl) in enumerate(zip(items, sls)):
        y = ys[i]
        yc = y - head_sums(y) * inv_n
        var = head_sums(yc * yc) * inv_n
        yn = yc * lax.rsqrt(var + GN_EPS) * lnw_ref[:, sl] + lnb_ref[:, sl]
        bonus = head_sums(pre[bi]["bonus"][:, sl]) * vh[i]
        y_ref[bi, :, sl] = (yn + bonus) * pre[bi]["g"][:, sl]


def _merge_kernel(x_ref, ya_ref, yb_ref, yc_ref, g_ref, wg_ref, bg_ref, wr_ref, wp_ref, wm_ref, wo_ref,
                  out_ref):
    x = x_ref[...]
    h = _rms_norm(x, g_ref[...]).astype(BF16)
    d = D_MODEL
    merged = None
    for i, (y_ref, w_ref) in enumerate(((ya_ref, wr_ref), (yb_ref, wp_ref), (yc_ref, wm_ref))):
        gate = jax.nn.sigmoid(_dot(h, wg_ref[:, i * d:(i + 1) * d]) + bg_ref[:, i * d:(i + 1) * d])
        term = gate * _dot(y_ref[...].astype(BF16), w_ref[...])
        merged = term if merged is None else merged + term
    out_ref[...] = x + _dot(merged.astype(BF16), wo_ref[...])


def _conv_ffn_kernel(x_ref, g_ref, win_ref, cw_ref, cb_ref, wout_ref, gf_ref, out_ref, tail_ref):
    t = pl.program_id(1)
    tm = x_ref.shape[0]
    x = x_ref[...]
    h = _rms_norm(x, g_ref[...]).astype(BF16)
    row = lax.broadcasted_iota(jnp.int32, (tm, 1), 0)
    first = t == 0
    acc = x
    n_chunks = D_FF // FF_CHUNK

    def up_proj(j):
        return (_dot(h, win_ref[:, j * FF_CHUNK:(j + 1) * FF_CHUNK]),
                _dot(h, win_ref[:, D_FF + j * FF_CHUNK:D_FF + (j + 1) * FF_CHUNK]))

    nxt = up_proj(0)
    for j in range(n_chunks):
        cs = slice(j * FF_CHUNK, (j + 1) * FF_CHUNK)
        u, gv = nxt
        if j + 1 < n_chunks:
            nxt = up_proj(j + 1)
        tail = jnp.where(first, 0.0, tail_ref[:, cs])
        tail_ref[:, cs] = u[tm - 8:, :]
        u1 = jnp.where(row == 0, tail[7:8, :], pltpu.roll(u, 1, axis=0))
        u2 = jnp.where(row == 0, tail[6:7, :], jnp.where(row == 1, tail[7:8, :], pltpu.roll(u, 2, axis=0)))
        uc = cw_ref[0:1, cs] * u2 + cw_ref[1:2, cs] * u1 + cw_ref[2:3, cs] * u + cb_ref[:, cs]
        act = 0.5 * uc * (1.0 + lax.erf(uc * (2.0 ** -0.5))) * gv
        acc = acc + _dot(act.astype(BF16), wout_ref[cs, :])
    out_ref[...] = _rms_norm(acc, gf_ref[...])


def _const_spec(shape):
    nd = len(shape)
    return pl.BlockSpec(shape, lambda *_: (0,) * nd)


def kernel(x, mem, norm_mix_g, w_in_mix, mu_shift, w0, w_lora_b, a0, a_lora_b, g_lora_b, k_k, k_a, r_k,
           ln_x_w, ln_x_b, pool_w, pool_scale, norm_mem_g, w_mem_kv, w_up_rwkv, w_up_pool, w_up_mem,
           w_gate, b_gate, w_o, norm_ffn_g, w_ffn_in, ffn_conv_w, ffn_conv_b, w_ffn_out, norm_final_g):
    bsz, seq, d = x.shape
    n_mem = mem.shape[1]
    assert d == D_MODEL and seq % TM_PROJ == 0 and seq % CHUNK == 0 and bsz % SCAN_BATCH == 0
    assert norm_mix_g.shape[0] == 1, "single-layer block"
    l = 0
    row = lambda a: a.reshape(1, -1).astype(F32)

    wmix = w_in_mix[l].astype(BF16)
    zpad = jnp.zeros((DECAY_LORA, RWKV_DIM), F32)
    wl_pad = jnp.concatenate([w_lora_b[l], zpad], axis=0).astype(BF16)
    al_pad = jnp.concatenate([zpad, a_lora_b[l]], axis=0).astype(BF16)
    gl = g_lora_b[l].astype(BF16)
    head_id = jnp.arange(LANES) // HEAD_DIM
    hsum = (head_id[:, None] == head_id[None, :]).astype(BF16)
    pool_bd = jax.scipy.linalg.block_diag(*[pool_w[l, i] for i in range(len(POOL_WINDOWS))]).astype(BF16)

    kv = pl.pallas_call(
        _mem_kv_kernel,
        out_shape=jax.ShapeDtypeStruct((bsz, n_mem, 2 * MEM_DIM), BF16),
        grid=(bsz,),
        in_specs=[pl.BlockSpec((None, n_mem, d), lambda b: (b, 0, 0)),
                  _const_spec((1, d)), _const_spec((d, 2 * MEM_DIM))],
        out_specs=pl.BlockSpec((None, n_mem, 2 * MEM_DIM), lambda b: (b, 0, 0)),
        compiler_params=pltpu.CompilerParams(dimension_semantics=("arbitrary",)),
        name="mem_kv",
    )(mem, row(norm_mem_g[l]), w_mem_kv[l].astype(BF16))

    tok = lambda w: pl.BlockSpec((None, TM_PROJ, w), lambda b, t: (b, t, 0))
    f32_out = lambda w: jax.ShapeDtypeStruct((bsz, seq, w), F32)
    outs = pl.pallas_call(
        _mix_proj_kernel,
        out_shape=[f32_out(RWKV_DIM)] * 7 + [f32_out(POOL_DIM), f32_out(MEM_DIM)],
        grid=(bsz, seq // TM_PROJ),
        in_specs=[tok(d), _const_spec((1, d)), _const_spec((d, MIX_IN)), _const_spec((1, RWKV_IN)),
                  _const_spec((1, RWKV_DIM)), _const_spec((DECAY_LORA + ICLR_LORA, RWKV_DIM)),
                  _const_spec((1, RWKV_DIM)), _const_spec((DECAY_LORA + ICLR_LORA, RWKV_DIM)),
                  _const_spec((GATE_LORA, RWKV_DIM)), _const_spec((1, RWKV_DIM)), _const_spec((1, RWKV_DIM)),
                  _const_spec((LANES, LANES)), _const_spec((POOL_DIM, POOL_DIM)),
                  _const_spec((1, POOL_DIM)),
                  pl.BlockSpec((None, n_mem, 2 * MEM_DIM), lambda b, t: (b, 0, 0))],
        out_specs=[tok(RWKV_DIM)] * 7 + [tok(POOL_DIM), tok(MEM_DIM)],
        scratch_shapes=[pltpu.VMEM((1, RWKV_IN), F32), pltpu.VMEM((POOL_HALO, POOL_DIM), F32)],
        compiler_params=pltpu.CompilerParams(dimension_semantics=("arbitrary", "arbitrary"),
                                             vmem_limit_bytes=V7X_VMEM_LIMIT_BYTES),
        name="mix_proj",
    )(x, row(norm_mix_g[l]), wmix, row(mu_shift[l]), row(w0[l]), wl_pad, row(a0[l]), al_pad, gl,
      row(k_k[l]), row(k_a[l]), hsum, pool_bd, row(pool_scale[l]), kv)
    r_s, k_s, v_s, kk_s, b_s, lw_s, g_s, y_b, y_c = outs

    sb = SCAN_BATCH
    blk = pl.BlockSpec((sb, CHUNK, RWKV_DIM), lambda b, c: (b, c, 0))
    y_a = pl.pallas_call(
        _rwkv_scan_kernel,
        out_shape=f32_out(RWKV_DIM),
        grid=(bsz // sb, seq // CHUNK),
        in_specs=[blk] * 7 + [_const_spec((1, RWKV_DIM))] * 3,
        out_specs=blk,
        scratch_shapes=[pltpu.VMEM((sb * RWKV_DIM // LANES, LANES, LANES), F32)],
        compiler_params=pltpu.CompilerParams(dimension_semantics=("arbitrary", "arbitrary")),
        name="rwkv_scan",
    )(r_s, k_s, v_s, kk_s, b_s, lw_s, g_s, row(r_k[l]), row(ln_x_w[l]), row(ln_x_b[l]))

    n_tok = bsz * seq
    flat = lambda a: a.reshape(n_tok, a.shape[-1])
    tokm = lambda w: pl.BlockSpec((TM_MERGE, w), lambda i: (i, 0))
    x1 = pl.pallas_call(
        _merge_kernel,
        out_shape=jax.ShapeDtypeStruct((n_tok, d), F32),
        grid=(n_tok // TM_MERGE,),
        in_specs=[tokm(d), tokm(RWKV_DIM), tokm(POOL_DIM), tokm(MEM_DIM), _const_spec((1, d)),
                  _const_spec((d, 3 * d)), _const_spec((1, 3 * d)), _const_spec((RWKV_DIM, d)),
                  _const_spec((POOL_DIM, d)), _const_spec((MEM_DIM, d)), _const_spec((d, d))],
        out_specs=tokm(d),
        compiler_params=pltpu.CompilerParams(dimension_semantics=("arbitrary",),
                                             vmem_limit_bytes=V7X_VMEM_LIMIT_BYTES),
        name="merge",
    )(flat(x), flat(y_a), flat(y_b), flat(y_c), row(norm_mix_g[l]), w_gate[l].astype(BF16),
      row(b_gate[l]), w_up_rwkv[l].astype(BF16), w_up_pool[l].astype(BF16), w_up_mem[l].astype(BF16),
      w_o[l].astype(BF16))

    tokf = pl.BlockSpec((None, TM_FFN, d), lambda b, t: (b, t, 0))
    single = pl.Buffered(1)
    out = pl.pallas_call(
        _conv_ffn_kernel,
        out_shape=jax.ShapeDtypeStruct((bsz, seq, d), F32),
        grid=(bsz, seq // TM_FFN),
        in_specs=[tokf, _const_spec((1, d)),
                  pl.BlockSpec((d, 2 * D_FF), lambda b, t: (0, 0), pipeline_mode=single),
                  _const_spec((3, D_FF)), _const_spec((1, D_FF)),
                  pl.BlockSpec((D_FF, d), lambda b, t: (0, 0), pipeline_mode=single),
                  _const_spec((1, d))],
        out_specs=tokf,
        scratch_shapes=[pltpu.VMEM((8, D_FF), F32)],
        compiler_params=pltpu.CompilerParams(dimension_semantics=("arbitrary", "arbitrary"),
                                             vmem_limit_bytes=V7X_VMEM_LIMIT_BYTES),
        name="conv_ffn",
    )(x1.reshape(bsz, seq, d), row(norm_ffn_g[l]), w_ffn_in[l].astype(BF16), ffn_conv_w[l].astype(F32),
      row(ffn_conv_b[l]), w_ffn_out[l].astype(BF16), row(norm_final_g))
    return out
```

```python
import functools
import math

import jax
import jax.numpy as jnp
from jax import lax
from jax.experimental import pallas as pl
from jax.experimental.pallas import tpu as pltpu

F32 = jnp.float32
BF16 = jnp.bfloat16

D_MODEL = 1024
HEAD_DIM = 64
RWKV_HEADS = 8
RWKV_DIM = RWKV_HEADS * HEAD_DIM
DECAY_LORA = 64
ICLR_LORA = 64
GATE_LORA = 128
RWKV_IN = 3 * RWKV_DIM + DECAY_LORA + ICLR_LORA + GATE_LORA
POOL_WINDOWS = (2, 4, 8, 16)
POOL_GROUP_DIM = 64
POOL_DIM = len(POOL_WINDOWS) * POOL_GROUP_DIM
POOL_HALO = 16
MEM_HEADS = 4
MEM_DIM = MEM_HEADS * HEAD_DIM
MIX_IN = RWKV_IN + POOL_DIM + MEM_DIM
D_FF = 2816
NORM_EPS = 1e-6
GN_EPS = 64e-5
CHUNK = 64
FF_CHUNK = 256
V7X_VMEM_LIMIT_BYTES = 56 * 1024 * 1024
LANES = 128

TM_PROJ = 512
TM_MERGE = 512
TM_FFN = 512
SCAN_BATCH = 4


def _dot(a, b):
    return jnp.dot(a, b, preferred_element_type=F32)


def _dot_nt(a, b):
    return lax.dot_general(a, b, (((1,), (1,)), ((), ())), preferred_element_type=F32)


def _dot_tn(a, b):
    return lax.dot_general(a, b, (((0,), (0,)), ((), ())), preferred_element_type=F32)


def _rms_norm(x, g):
    return x * lax.rsqrt(jnp.mean(x * x, axis=-1, keepdims=True) + NORM_EPS) * g


def _split_dot(x, w_bf16):
    hi = x.astype(BF16)
    lo = (x - hi.astype(F32)).astype(BF16)
    return _dot(hi, w_bf16) + _dot(lo, w_bf16)


def _split3_dot_left(w_bf16, x):
    hi = x.astype(BF16)
    r1 = x - hi.astype(F32)
    mid = r1.astype(BF16)
    lo = (r1 - mid.astype(F32)).astype(BF16)
    return _dot(w_bf16, hi) + _dot(w_bf16, mid) + _dot(w_bf16, lo)


def _mem_kv_kernel(mem_ref, g_ref, w_ref, kv_ref):
    m = _rms_norm(mem_ref[...], g_ref[...])
    kv_ref[...] = _dot(m.astype(BF16), w_ref[...]).astype(BF16)


def _mix_proj_kernel(x_ref, g_ref, wmix_ref, mu_ref, w0_ref, wl_ref, a0_ref, al_ref, gl_ref,
                     kk_ref, ka_ref, hsum_ref, poolw_ref, pools_ref, kv_ref,
                     r_out, k_out, v_out, kkn_out, b_out, lw_out, g_out, yb_out, yc_out,
                     prev_ref, halo_ref):
    t = pl.program_id(1)
    tm = x_ref.shape[0]
    first = t == 0

    h = _rms_norm(x_ref[...], g_ref[...]).astype(BF16)
    p = _dot(h, wmix_ref[...])

    ps = p[:, :RWKV_IN]
    row = lax.broadcasted_iota(jnp.int32, (tm, 1), 0)
    carry = jnp.where(first, 0.0, prev_ref[...])
    prev = jnp.where(row == 0, carry, pltpu.roll(ps, 1, axis=0))
    prev_ref[...] = ps[tm - 1:tm, :]
    ps = ps + (prev - ps) * mu_ref[...]

    r = ps[:, :RWKV_DIM]
    k = ps[:, RWKV_DIM:2 * RWKV_DIM]
    v = ps[:, 2 * RWKV_DIM:3 * RWKV_DIM]
    wa = ps[:, 3 * RWKV_DIM:3 * RWKV_DIM + DECAY_LORA + ICLR_LORA]
    gd = ps[:, 3 * RWKV_DIM + DECAY_LORA + ICLR_LORA:RWKV_IN]

    z = w0_ref[...] + _dot(jnp.tanh(wa).astype(BF16), wl_ref[...])
    lw_out[...] = (-math.exp(-0.5)) * jax.nn.sigmoid(z)
    a = jax.nn.sigmoid(a0_ref[...] + _dot(wa.astype(BF16), al_ref[...]))
    g_out[...] = _dot(jax.nn.sigmoid(gd).astype(BF16), gl_ref[...])

    kkr = k * kk_ref[...]
    kk_sq = kkr * kkr
    ssq = jnp.concatenate([_split_dot(kk_sq[:, j:j + LANES], hsum_ref[...])
                           for j in range(0, RWKV_DIM, LANES)], axis=1)
    kkn = kkr * lax.rsqrt(ssq + 1e-12)
    r_out[...] = r
    v_out[...] = v
    k_out[...] = k * (1.0 + (a - 1.0) * ka_ref[...])
    kkn_out[...] = kkn
    b_out[...] = kkn * a

    pp = p[:, RWKV_IN:RWKV_IN + POOL_DIM]
    halo = jnp.where(first, 0.0, halo_ref[...])
    halo_ref[...] = pp[tm - POOL_HALO:, :]
    ext = jnp.concatenate([halo, pp], axis=0)
    lane_group = lax.broadcasted_iota(jnp.int32, (1, POOL_DIM), 1) // POOL_GROUP_DIM
    win = jnp.zeros((1, POOL_DIM), jnp.int32)
    sel = jnp.zeros((tm, POOL_DIM), F32)
    acc = ext
    shift = 1
    for gi, w in enumerate(POOL_WINDOWS):
        while shift < w:
            acc = acc + pltpu.roll(acc, shift, axis=0)
            shift *= 2
        sel = jnp.where(lane_group == gi, acc[POOL_HALO:, :], sel)
        win = jnp.where(lane_group == gi, w, win)
    pos = t * tm + row + 1
    cnt = jnp.minimum(pos, win).astype(F32)
    dpool = sel / cnt - pp
    yb_out[...] = _dot(dpool.astype(BF16), poolw_ref[...]) * pools_ref[...]

    q = p[:, RWKV_IN + POOL_DIM:]
    kmem = kv_ref[:, :MEM_DIM]
    vmem = kv_ref[:, MEM_DIM:]
    lane_head = lax.broadcasted_iota(jnp.int32, (1, MEM_DIM), 1) // HEAD_DIM
    cols = []
    for j in range(0, MEM_DIM, LANES):
        col = None
        for hd in range(j // HEAD_DIM, (j + LANES) // HEAD_DIM):
            in_head = lane_head == hd
            qh = jnp.where(in_head, q, 0.0).astype(BF16)
            s = _dot_nt(qh, kmem) * (HEAD_DIM ** -0.5)
            e = jnp.exp(s - jnp.max(s, axis=-1, keepdims=True))
            prob = e / jnp.sum(e, axis=-1, keepdims=True)
            vh = jnp.where(in_head[:, j:j + LANES], vmem[:, j:j + LANES], 0.0)
            o = _dot(prob.astype(BF16), vh)
            col = o if col is None else col + o
        cols.append(col)
    yc_out[...] = jnp.concatenate(cols, axis=1)


def _chunk_masks():
    c = CHUNK
    ri = lax.broadcasted_iota(jnp.int32, (2 * c, 2 * c), 0)
    ci = lax.broadcasted_iota(jnp.int32, (2 * c, 2 * c), 1)
    rt = ri % c
    ct = ci % c
    aa_mask = (rt > ct) | ((ri >= c) & (rt == ct))
    r64 = lax.broadcasted_iota(jnp.int32, (c, 2 * c), 0)
    c64 = lax.broadcasted_iota(jnp.int32, (c, 2 * c), 1) % c
    eye = r64 == c64
    levels = []
    blk = 2
    while blk < c:
        same_outer = (r64 // (2 * blk)) == (c64 // (2 * blk))
        diff_inner = (r64 // blk) != (c64 // blk)
        levels.append(same_outer & diff_inner & (r64 > c64))
        blk *= 2
    diag2 = ((r64 // 2) == (c64 // 2)) & (r64 > c64)
    tri_incl = lax.broadcasted_iota(jnp.int32, (c, c), 0) >= lax.broadcasted_iota(jnp.int32, (c, c), 1)
    block_diag = (ri < c) == (ci < c)
    eye2 = ri == ci
    return aa_mask, eye, diag2, levels, tri_incl, block_diag, eye2


def _rwkv_scan_kernel(r_ref, k_ref, v_ref, kk_ref, b_ref, lw_ref, g_ref, rk_ref, lnw_ref, lnb_ref,
                      y_ref, state_ref):
    c = CHUNK
    n = HEAD_DIM
    chunk_idx = pl.program_id(1)

    @pl.when(chunk_idx == 0)
    def _():
        state_ref[...] = jnp.zeros_like(state_ref)

    aa_mask, eye, diag2, levels, tri_incl, block_diag, eye2 = _chunk_masks()
    tri = jnp.where(tri_incl, 1.0, 0.0).astype(BF16)
    eye_f = jnp.where(eye, 1.0, 0.0)
    zeros_b = jnp.zeros((c, LANES), BF16)
    head_a = lax.broadcasted_iota(jnp.int32, (c, LANES), 1) < n
    only_a = jnp.where(head_a, 1.0, 0.0).astype(BF16)
    only_b = jnp.where(head_a, 0.0, 1.0).astype(BF16)
    only_a2 = jnp.concatenate([only_a, only_a], axis=0)
    only_b2 = jnp.concatenate([only_b, only_b], axis=0)
    lvl_a = [jnp.where(lvl & head_a, 1.0, 0.0).astype(BF16) for lvl in levels]
    lvl_b = [jnp.where(lvl & ~head_a, 1.0, 0.0).astype(BF16) for lvl in levels]

    def both(xb, ma=only_a, mb=only_b):
        return jnp.concatenate([xb * ma, xb * mb], axis=0)

    def head_sums(x):
        sa = jnp.sum(jnp.where(head_a, x, 0.0), axis=-1, keepdims=True)
        sb = jnp.sum(jnp.where(head_a, 0.0, x), axis=-1, keepdims=True)
        return jnp.where(head_a, sa, sb)

    nb = r_ref.shape[0]
    items = [(bi, j) for bi in range(nb) for j in range(RWKV_DIM // LANES)]
    sls = [slice(j * LANES, (j + 1) * LANES) for _, j in items]
    ni = len(items)

    pre = []
    for bi in range(nb):
        lw = lw_ref[bi]
        cum = _split3_dot_left(tri, lw)
        tot = cum[c - 1:c, :]
        e_inv = jnp.exp(-cum)
        e_end = jnp.exp(tot - cum)
        r_all = r_ref[bi]
        k_all = k_ref[bi]
        b_all = b_ref[bi]
        pre.append(dict(
            rt=r_all * jnp.exp(cum), kkt=kk_ref[bi] * jnp.exp(cum - lw), bt=b_all * e_inv, kt=k_all * e_inv,
            bd=b_all * e_end, kd=k_all * e_end, e_tot=jnp.exp(tot), v=v_ref[bi], g=g_ref[bi],
            bonus=r_all * k_all * rk_ref[...]))

    rt = [pre[bi]["rt"][:, sl] for (bi, _), sl in zip(items, sls)]
    kkt = [pre[bi]["kkt"][:, sl] for (bi, _), sl in zip(items, sls)]
    vh = [pre[bi]["v"][:, sl] for (bi, _), sl in zip(items, sls)]
    aa_a, aa_b = [], []
    for i, ((bi, _), sl) in enumerate(zip(items, sls)):
        lhs = jnp.concatenate([kkt[i], rt[i]], axis=0).astype(BF16)
        rhs = jnp.concatenate([pre[bi]["bt"][:, sl], pre[bi]["kt"][:, sl]], axis=0).astype(BF16)
        aa2 = _dot_nt(jnp.concatenate([lhs * only_a2, lhs * only_b2], axis=0), rhs)
        aa_a.append(jnp.where(aa_mask, aa2[:2 * c], 0.0))
        aa_b.append(jnp.where(aa_mask, aa2[2 * c:], 0.0))
    av = []
    for i in range(ni):
        vb = vh[i].astype(BF16)
        v_rows = jnp.concatenate([zeros_b, vb * only_a, zeros_b, vb * only_b], axis=0)
        av.append(_dot(jnp.concatenate([aa_a[i], aa_b[i]], axis=1).astype(BF16), v_rows))
    a_ab = [jnp.where(head_a, aa_a[i][:c], pltpu.roll(aa_b[i][:c], n, axis=1)).astype(BF16) for i in range(ni)]
    a_rb = [jnp.where(head_a, aa_a[i][c:], pltpu.roll(aa_b[i][c:], n, axis=1)).astype(BF16) for i in range(ni)]

    tinv = [eye_f - jnp.where(diag2, a.astype(F32), 0.0) for a in a_ab]
    for li in range(len(levels)):
        tb = [t_.astype(BF16) for t_ in tinv]
        half = [_dot(tb[i], both(a_ab[i], lvl_a[li], lvl_b[li])).astype(BF16) for i in range(ni)]
        tinv = [tinv[i] - _dot(half[i], both(tb[i])) for i in range(ni)]

    w12 = [_dot(tinv[i].astype(BF16),
                jnp.concatenate([both(kkt[i].astype(BF16)), both(av[i][:c].astype(BF16))], axis=1))
           for i in range(ni)]
    w12_b = [w.astype(BF16) for w in w12]
    pq = [jnp.concatenate([rt[i], av[i][c:]], axis=1)
          - _dot(a_rb[i], jnp.concatenate([both(w12_b[i][:, :LANES]), both(w12_b[i][:, LANES:])], axis=1))
          for i in range(ni)]
    mg = []
    for i, ((bi, _), sl) in enumerate(zip(items, sls)):
        neg_bd_kd = jnp.concatenate([-pre[bi]["bd"][:, sl], pre[bi]["kd"][:, sl]], axis=0).astype(BF16)
        wv = jnp.concatenate([w12_b[i], jnp.concatenate([zeros_b, vh[i].astype(BF16)], axis=1)], axis=0)
        mg.append(_dot_tn(neg_bd_kd, wv))

    ys = []
    for i, ((bi, _), sl) in enumerate(zip(items, sls)):
        m_mat = jnp.where(block_diag, mg[i][:, :LANES], 0.0) + jnp.where(eye2, pre[bi]["e_tot"][:, sl], 0.0)
        pm = jnp.concatenate([pq[i][:, :LANES], m_mat], axis=0).astype(BF16)
        out = _dot(pm, state_ref[i].astype(BF16))
        ys.append(out[:c] + pq[i][:, LANES:])
        state_ref[i] = out[c:] + jnp.where(block_diag, mg[i][:, LANES:], 0.0)

    inv_n = 1.0 / n
    for i, ((bi, _), sl) in enumerate(zip(items, sls)):
        y = ys[i]
        yc = y - head_sums(y) * inv_n
        var = head_sums(yc * yc) * inv_n
        yn = yc * lax.rsqrt(var + GN_EPS) * lnw_ref[:, sl] + lnb_ref[:, sl]
        bonus = head_sums(pre[bi]["bonus"][:, sl]) * vh[i]
        y_ref[bi, :, sl] = (yn + bonus) * pre[bi]["g"][:, sl]


def _merge_kernel(x_ref, ya_ref, yb_ref, yc_ref, g_ref, wg_ref, bg_ref, wr_ref, wp_ref, wm_ref, wo_ref,
                  out_ref):
    x = x_ref[...]
    h = _rms_norm(x, g_ref[...]).astype(BF16)
    d = D_MODEL
    merged = None
    for i, (y_ref, w_ref) in enumerate(((ya_ref, wr_ref), (yb_ref, wp_ref), (yc_ref, wm_ref))):
        gate = jax.nn.sigmoid(_dot(h, wg_ref[:, i * d:(i + 1) * d]) + bg_ref[:, i * d:(i + 1) * d])
        term = gate * _dot(y_ref[...].astype(BF16), w_ref[...])
        merged = term if merged is None else merged + term
    out_ref[...] = x + _dot(merged.astype(BF16), wo_ref[...])


def _conv_ffn_kernel(x_ref, g_ref, win_ref, cw_ref, cb_ref, wout_ref, gf_ref, out_ref, tail_ref, act_ref):
    t = pl.program_id(1)
    tm = x_ref.shape[0]
    x = x_ref[...]
    h = _rms_norm(x, g_ref[...]).astype(BF16)
    row = lax.broadcasted_iota(jnp.int32, (tm, 1), 0)
    first = t == 0
    n_chunks = D_FF // FF_CHUNK

    def up_proj(j):
        return (_dot(h, win_ref[:, j * FF_CHUNK:(j + 1) * FF_CHUNK]),
                _dot(h, win_ref[:, D_FF + j * FF_CHUNK:D_FF + (j + 1) * FF_CHUNK]))

    nxt = up_proj(0)
    for j in range(n_chunks):
        cs = slice(j * FF_CHUNK, (j + 1) * FF_CHUNK)
        u, gv = nxt
        if j + 1 < n_chunks:
            nxt = up_proj(j + 1)
        tail = jnp.where(first, 0.0, tail_ref[:, cs])
        tail_ref[:, cs] = u[tm - 8:, :]
        u1 = jnp.where(row == 0, tail[7:8, :], pltpu.roll(u, 1, axis=0))
        u2 = jnp.where(row == 0, tail[6:7, :], jnp.where(row == 1, tail[7:8, :], pltpu.roll(u, 2, axis=0)))
        uc = cw_ref[0:1, cs] * u2 + cw_ref[1:2, cs] * u1 + cw_ref[2:3, cs] * u + cb_ref[:, cs]
        act = 0.5 * uc * (1.0 + lax.erf(uc * (2.0 ** -0.5))) * gv
        act_ref[:, cs] = act.astype(BF16)
    out_ref[...] = _rms_norm(x + _dot(act_ref[...], wout_ref[...]), gf_ref[...])


def _const_spec(shape):
    nd = len(shape)
    return pl.BlockSpec(shape, lambda *_: (0,) * nd)


def kernel(x, mem, norm_mix_g, w_in_mix, mu_shift, w0, w_lora_b, a0, a_lora_b, g_lora_b, k_k, k_a, r_k,
           ln_x_w, ln_x_b, pool_w, pool_scale, norm_mem_g, w_mem_kv, w_up_rwkv, w_up_pool, w_up_mem,
           w_gate, b_gate, w_o, norm_ffn_g, w_ffn_in, ffn_conv_w, ffn_conv_b, w_ffn_out, norm_final_g):
    bsz, seq, d = x.shape
    n_mem = mem.shape[1]
    assert d == D_MODEL and seq % TM_PROJ == 0 and seq % CHUNK == 0 and bsz % SCAN_BATCH == 0
    assert norm_mix_g.shape[0] == 1, "single-layer block"
    l = 0
    row = lambda a: a.reshape(1, -1).astype(F32)

    wmix = w_in_mix[l].astype(BF16)
    zpad = jnp.zeros((DECAY_LORA, RWKV_DIM), F32)
    wl_pad = jnp.concatenate([w_lora_b[l], zpad], axis=0).astype(BF16)
    al_pad = jnp.concatenate([zpad, a_lora_b[l]], axis=0).astype(BF16)
    gl = g_lora_b[l].astype(BF16)
    head_id = jnp.arange(LANES) // HEAD_DIM
    hsum = (head_id[:, None] == head_id[None, :]).astype(BF16)
    pool_bd = jax.scipy.linalg.block_diag(*[pool_w[l, i] for i in range(len(POOL_WINDOWS))]).astype(BF16)

    kv = pl.pallas_call(
        _mem_kv_kernel,
        out_shape=jax.ShapeDtypeStruct((bsz, n_mem, 2 * MEM_DIM), BF16),
        grid=(bsz,),
        in_specs=[pl.BlockSpec((None, n_mem, d), lambda b: (b, 0, 0)),
                  _const_spec((1, d)), _const_spec((d, 2 * MEM_DIM))],
        out_specs=pl.BlockSpec((None, n_mem, 2 * MEM_DIM), lambda b: (b, 0, 0)),
        compiler_params=pltpu.CompilerParams(dimension_semantics=("arbitrary",)),
        name="mem_kv",
    )(mem, row(norm_mem_g[l]), w_mem_kv[l].astype(BF16))

    tok = lambda w: pl.BlockSpec((None, TM_PROJ, w), lambda b, t: (b, t, 0))
    f32_out = lambda w: jax.ShapeDtypeStruct((bsz, seq, w), F32)
    outs = pl.pallas_call(
        _mix_proj_kernel,
        out_shape=[f32_out(RWKV_DIM)] * 7 + [f32_out(POOL_DIM), f32_out(MEM_DIM)],
        grid=(bsz, seq // TM_PROJ),
        in_specs=[tok(d), _const_spec((1, d)), _const_spec((d, MIX_IN)), _const_spec((1, RWKV_IN)),
                  _const_spec((1, RWKV_DIM)), _const_spec((DECAY_LORA + ICLR_LORA, RWKV_DIM)),
                  _const_spec((1, RWKV_DIM)), _const_spec((DECAY_LORA + ICLR_LORA, RWKV_DIM)),
                  _const_spec((GATE_LORA, RWKV_DIM)), _const_spec((1, RWKV_DIM)), _const_spec((1, RWKV_DIM)),
                  _const_spec((LANES, LANES)), _const_spec((POOL_DIM, POOL_DIM)),
                  _const_spec((1, POOL_DIM)),
                  pl.BlockSpec((None, n_mem, 2 * MEM_DIM), lambda b, t: (b, 0, 0))],
        out_specs=[tok(RWKV_DIM)] * 7 + [tok(POOL_DIM), tok(MEM_DIM)],
        scratch_shapes=[pltpu.VMEM((1, RWKV_IN), F32), pltpu.VMEM((POOL_HALO, POOL_DIM), F32)],
        compiler_params=pltpu.CompilerParams(dimension_semantics=("arbitrary", "arbitrary"),
                                             vmem_limit_bytes=V7X_VMEM_LIMIT_BYTES),
        name="mix_proj",
    )(x, row(norm_mix_g[l]), wmix, row(mu_shift[l]), row(w0[l]), wl_pad, row(a0[l]), al_pad, gl,
      row(k_k[l]), row(k_a[l]), hsum, pool_bd, row(pool_scale[l]), kv)
    r_s, k_s, v_s, kk_s, b_s, lw_s, g_s, y_b, y_c = outs

    sb = SCAN_BATCH
    blk = pl.BlockSpec((sb, CHUNK, RWKV_DIM), lambda b, c: (b, c, 0))
    y_a = pl.pallas_call(
        _rwkv_scan_kernel,
        out_shape=f32_out(RWKV_DIM),
        grid=(bsz // sb, seq // CHUNK),
        in_specs=[blk] * 7 + [_const_spec((1, RWKV_DIM))] * 3,
        out_specs=blk,
        scratch_shapes=[pltpu.VMEM((sb * RWKV_DIM // LANES, LANES, LANES), F32)],
        compiler_params=pltpu.CompilerParams(dimension_semantics=("arbitrary", "arbitrary")),
        name="rwkv_scan",
    )(r_s, k_s, v_s, kk_s, b_s, lw_s, g_s, row(r_k[l]), row(ln_x_w[l]), row(ln_x_b[l]))

    n_tok = bsz * seq
    flat = lambda a: a.reshape(n_tok, a.shape[-1])
    tokm = lambda w: pl.BlockSpec((TM_MERGE, w), lambda i: (i, 0))
    x1 = pl.pallas_call(
        _merge_kernel,
        out_shape=jax.ShapeDtypeStruct((n_tok, d), F32),
        grid=(n_tok // TM_MERGE,),
        in_specs=[tokm(d), tokm(RWKV_DIM), tokm(POOL_DIM), tokm(MEM_DIM), _const_spec((1, d)),
                  _const_spec((d, 3 * d)), _const_spec((1, 3 * d)), _const_spec((RWKV_DIM, d)),
                  _const_spec((POOL_DIM, d)), _const_spec((MEM_DIM, d)), _const_spec((d, d))],
        out_specs=tokm(d),
        compiler_params=pltpu.CompilerParams(dimension_semantics=("arbitrary",),
                                             vmem_limit_bytes=V7X_VMEM_LIMIT_BYTES),
        name="merge",
    )(flat(x), flat(y_a), flat(y_b), flat(y_c), row(norm_mix_g[l]), w_gate[l].astype(BF16),
      row(b_gate[l]), w_up_rwkv[l].astype(BF16), w_up_pool[l].astype(BF16), w_up_mem[l].astype(BF16),
      w_o[l].astype(BF16))

    tokf = pl.BlockSpec((None, TM_FFN, d), lambda b, t: (b, t, 0))
    single = pl.Buffered(1)
    out = pl.pallas_call(
        _conv_ffn_kernel,
        out_shape=jax.ShapeDtypeStruct((bsz, seq, d), F32),
        grid=(bsz, seq // TM_FFN),
        in_specs=[tokf, _const_spec((1, d)),
                  pl.BlockSpec((d, 2 * D_FF), lambda b, t: (0, 0), pipeline_mode=single),
                  _const_spec((3, D_FF)), _const_spec((1, D_FF)),
                  pl.BlockSpec((D_FF, d), lambda b, t: (0, 0), pipeline_mode=single),
                  _const_spec((1, d))],
        out_specs=tokf,
        scratch_shapes=[pltpu.VMEM((8, D_FF), F32), pltpu.VMEM((TM_FFN, D_FF), BF16)],
        compiler_params=pltpu.CompilerParams(dimension_semantics=("arbitrary", "arbitrary"),
                                             vmem_limit_bytes=V7X_VMEM_LIMIT_BYTES),
        name="conv_ffn",
    )(x1.reshape(bsz, seq, d), row(norm_ffn_g[l]), w_ffn_in[l].astype(BF16), ffn_conv_w[l].astype(F32),
      row(ffn_conv_b[l]), w_ffn_out[l].astype(BF16), row(norm_final_g))
    return out
```

```python
import functools
import math

import jax
import jax.numpy as jnp
from jax import lax
from jax.experimental import pallas as pl
from jax.experimental.pallas import tpu as pltpu

F32 = jnp.float32
BF16 = jnp.bfloat16

D_MODEL = 1024
HEAD_DIM = 64
RWKV_HEADS = 8
RWKV_DIM = RWKV_HEADS * HEAD_DIM
DECAY_LORA = 64
ICLR_LORA = 64
GATE_LORA = 128
RWKV_IN = 3 * RWKV_DIM + DECAY_LORA + ICLR_LORA + GATE_LORA
POOL_WINDOWS = (2, 4, 8, 16)
POOL_GROUP_DIM = 64
POOL_DIM = len(POOL_WINDOWS) * POOL_GROUP_DIM
POOL_HALO = 16
MEM_HEADS = 4
MEM_DIM = MEM_HEADS * HEAD_DIM
MIX_IN = RWKV_IN + POOL_DIM + MEM_DIM
D_FF = 2816
NORM_EPS = 1e-6
GN_EPS = 64e-5
CHUNK = 64
FF_CHUNK = 256
V7X_VMEM_LIMIT_BYTES = 56 * 1024 * 1024
LANES = 128

TM_PROJ = 512
TM_MERGE = 512
TM_FFN = 512
SCAN_BATCH = 4


def _dot(a, b):
    return jnp.dot(a, b, preferred_element_type=F32)


def _dot_nt(a, b):
    return lax.dot_general(a, b, (((1,), (1,)), ((), ())), preferred_element_type=F32)


def _dot_tn(a, b):
    return lax.dot_general(a, b, (((0,), (0,)), ((), ())), preferred_element_type=F32)


def _rms_norm(x, g):
    return x * lax.rsqrt(jnp.mean(x * x, axis=-1, keepdims=True) + NORM_EPS) * g


def _split_dot(x, w_bf16):
    hi = x.astype(BF16)
    lo = (x - hi.astype(F32)).astype(BF16)
    return _dot(hi, w_bf16) + _dot(lo, w_bf16)


def _split3_dot_left(w_bf16, x):
    hi = x.astype(BF16)
    r1 = x - hi.astype(F32)
    mid = r1.astype(BF16)
    lo = (r1 - mid.astype(F32)).astype(BF16)
    return _dot(w_bf16, hi) + _dot(w_bf16, mid) + _dot(w_bf16, lo)


def _mem_kv_kernel(mem_ref, g_ref, w_ref, kv_ref):
    m = _rms_norm(mem_ref[...], g_ref[...])
    kv_ref[...] = _dot(m.astype(BF16), w_ref[...]).astype(BF16)


def _mix_proj_kernel(x_ref, g_ref, wmix_ref, mu_ref, w0_ref, wl_ref, a0_ref, al_ref, gl_ref,
                     kk_ref, ka_ref, hsum_ref, poolw_ref, pools_ref, kv_ref,
                     r_out, k_out, v_out, kkn_out, b_out, lw_out, g_out, yb_out, yc_out,
                     prev_ref, halo_ref):
    t = pl.program_id(1)
    tm = x_ref.shape[0]
    first = t == 0

    h = _rms_norm(x_ref[...], g_ref[...]).astype(BF16)
    row = lax.broadcasted_iota(jnp.int32, (tm, 1), 0)
    lo0 = 3 * RWKV_DIM

    def proj(c0, c1):
        return _dot(h, wmix_ref[:, c0:c1])

    def token_shift(ps, c0, c1):
        carry = jnp.where(first, 0.0, prev_ref[:, c0:c1])
        prev = jnp.where(row == 0, carry, pltpu.roll(ps, 1, axis=0))
        prev_ref[:, c0:c1] = ps[tm - 1:tm, :]
        return ps + (prev - ps) * mu_ref[:, c0:c1]

    p_pq = proj(RWKV_IN, MIX_IN)
    p_lo = proj(lo0, RWKV_IN)
    p_r = proj(0, RWKV_DIM)
    pp = p_pq[:, :POOL_DIM]
    q = p_pq[:, POOL_DIM:]

    kmem = kv_ref[:, :MEM_DIM]
    vmem = kv_ref[:, MEM_DIM:]
    lane_head = lax.broadcasted_iota(jnp.int32, (1, MEM_DIM), 1) // HEAD_DIM
    scores = [_dot_nt(jnp.where(lane_head == hd, q, 0.0).astype(BF16), kmem) * (HEAD_DIM ** -0.5)
              for hd in range(MEM_HEADS)]

    p_k = proj(RWKV_DIM, 2 * RWKV_DIM)
    p_v = proj(2 * RWKV_DIM, lo0)

    lo = token_shift(p_lo, lo0, RWKV_IN)
    wa = lo[:, :DECAY_LORA + ICLR_LORA]
    gd = lo[:, DECAY_LORA + ICLR_LORA:]
    z = w0_ref[...] + _dot(jnp.tanh(wa).astype(BF16), wl_ref[...])
    lw_out[...] = (-math.exp(-0.5)) * jax.nn.sigmoid(z)
    a = jax.nn.sigmoid(a0_ref[...] + _dot(wa.astype(BF16), al_ref[...]))
    g_out[...] = _dot(jax.nn.sigmoid(gd).astype(BF16), gl_ref[...])

    halo = jnp.where(first, 0.0, halo_ref[...])
    halo_ref[...] = pp[tm - POOL_HALO:, :]
    ext = jnp.concatenate([halo, pp], axis=0)
    lane_group = lax.broadcasted_iota(jnp.int32, (1, POOL_DIM), 1) // POOL_GROUP_DIM
    win = jnp.zeros((1, POOL_DIM), jnp.int32)
    sel = jnp.zeros((tm, POOL_DIM), F32)
    acc = ext
    shift = 1
    for gi, w in enumerate(POOL_WINDOWS):
        while shift < w:
            acc = acc + pltpu.roll(acc, shift, axis=0)
            shift *= 2
        sel = jnp.where(lane_group == gi, acc[POOL_HALO:, :], sel)
        win = jnp.where(lane_group == gi, w, win)
    pos = t * tm + row + 1
    cnt = jnp.minimum(pos, win).astype(F32)
    dpool = sel / cnt - pp
    yb_out[...] = _dot(dpool.astype(BF16), poolw_ref[...]) * pools_ref[...]

    cols = []
    for j in range(0, MEM_DIM, LANES):
        col = None
        for hd in range(j // HEAD_DIM, (j + LANES) // HEAD_DIM):
            s = scores[hd]
            e = jnp.exp(s - jnp.max(s, axis=-1, keepdims=True))
            prob = e / jnp.sum(e, axis=-1, keepdims=True)
            vh = jnp.where((lane_head == hd)[:, j:j + LANES], vmem[:, j:j + LANES], 0.0)
            o = _dot(prob.astype(BF16), vh)
            col = o if col is None else col + o
        cols.append(col)
    yc_out[...] = jnp.concatenate(cols, axis=1)

    r = token_shift(p_r, 0, RWKV_DIM)
    k = token_shift(p_k, RWKV_DIM, 2 * RWKV_DIM)
    v = token_shift(p_v, 2 * RWKV_DIM, lo0)
    kkr = k * kk_ref[...]
    kk_sq = kkr * kkr
    ssq = jnp.concatenate([_split_dot(kk_sq[:, j:j + LANES], hsum_ref[...])
                           for j in range(0, RWKV_DIM, LANES)], axis=1)
    kkn = kkr * lax.rsqrt(ssq + 1e-12)
    r_out[...] = r
    v_out[...] = v
    k_out[...] = k * (1.0 + (a - 1.0) * ka_ref[...])
    kkn_out[...] = kkn
    b_out[...] = kkn * a


def _chunk_masks():
    c = CHUNK
    ri = lax.broadcasted_iota(jnp.int32, (2 * c, 2 * c), 0)
    ci = lax.broadcasted_iota(jnp.int32, (2 * c, 2 * c), 1)
    rt = ri % c
    ct = ci % c
    aa_mask = (rt > ct) | ((ri >= c) & (rt == ct))
    r64 = lax.broadcasted_iota(jnp.int32, (c, 2 * c), 0)
    c64 = lax.broadcasted_iota(jnp.int32, (c, 2 * c), 1) % c
    eye = r64 == c64
    levels = []
    blk = 2
    while blk < c:
        same_outer = (r64 // (2 * blk)) == (c64 // (2 * blk))
        diff_inner = (r64 // blk) != (c64 // blk)
        levels.append(same_outer & diff_inner & (r64 > c64))
        blk *= 2
    diag2 = ((r64 // 2) == (c64 // 2)) & (r64 > c64)
    tri_incl = lax.broadcasted_iota(jnp.int32, (c, c), 0) >= lax.broadcasted_iota(jnp.int32, (c, c), 1)
    block_diag = (ri < c) == (ci < c)
    eye2 = ri == ci
    return aa_mask, eye, diag2, levels, tri_incl, block_diag, eye2


def _rwkv_scan_kernel(r_ref, k_ref, v_ref, kk_ref, b_ref, lw_ref, g_ref, rk_ref, lnw_ref, lnb_ref,
                      y_ref, state_ref):
    c = CHUNK
    n = HEAD_DIM
    chunk_idx = pl.program_id(1)

    @pl.when(chunk_idx == 0)
    def _():
        state_ref[...] = jnp.zeros_like(state_ref)

    aa_mask, eye, diag2, levels, tri_incl, block_diag, eye2 = _chunk_masks()
    tri = jnp.where(tri_incl, 1.0, 0.0).astype(BF16)
    eye_f = jnp.where(eye, 1.0, 0.0)
    zeros_b = jnp.zeros((c, LANES), BF16)
    head_a = lax.broadcasted_iota(jnp.int32, (c, LANES), 1) < n
    only_a = jnp.where(head_a, 1.0, 0.0).astype(BF16)
    only_b = jnp.where(head_a, 0.0, 1.0).astype(BF16)
    only_a2 = jnp.concatenate([only_a, only_a], axis=0)
    only_b2 = jnp.concatenate([only_b, only_b], axis=0)
    lvl_a = [jnp.where(lvl & head_a, 1.0, 0.0).astype(BF16) for lvl in levels]
    lvl_b = [jnp.where(lvl & ~head_a, 1.0, 0.0).astype(BF16) for lvl in levels]

    def both(xb, ma=only_a, mb=only_b):
        return jnp.concatenate([xb * ma, xb * mb], axis=0)

    def head_sums(x):
        sa = jnp.sum(jnp.where(head_a, x, 0.0), axis=-1, keepdims=True)
        sb = jnp.sum(jnp.where(head_a, 0.0, x), axis=-1, keepdims=True)
        return jnp.where(head_a, sa, sb)

    nb = r_ref.shape[0]
    items = [(bi, j) for bi in range(nb) for j in range(RWKV_DIM // LANES)]
    sls = [slice(j * LANES, (j + 1) * LANES) for _, j in items]
    ni = len(items)

    pre = []
    for bi in range(nb):
        lw = lw_ref[bi]
        cum = _split3_dot_left(tri, lw)
        tot = cum[c - 1:c, :]
        e_inv = jnp.exp(-cum)
        e_end = jnp.exp(tot - cum)
        r_all = r_ref[bi]
        k_all = k_ref[bi]
        b_all = b_ref[bi]
        pre.append(dict(
            rt=r_all * jnp.exp(cum), kkt=kk_ref[bi] * jnp.exp(cum - lw), bt=b_all * e_inv, kt=k_all * e_inv,
            bd=b_all * e_end, kd=k_all * e_end, e_tot=jnp.exp(tot), v=v_ref[bi], g=g_ref[bi],
            bonus=r_all * k_all * rk_ref[...]))

    rt = [pre[bi]["rt"][:, sl] for (bi, _), sl in zip(items, sls)]
    kkt = [pre[bi]["kkt"][:, sl] for (bi, _), sl in zip(items, sls)]
    vh = [pre[bi]["v"][:, sl] for (bi, _), sl in zip(items, sls)]
    aa_a, aa_b = [], []
    for i, ((bi, _), sl) in enumerate(zip(items, sls)):
        lhs = jnp.concatenate([kkt[i], rt[i]], axis=0).astype(BF16)
        rhs = jnp.concatenate([pre[bi]["bt"][:, sl], pre[bi]["kt"][:, sl]], axis=0).astype(BF16)
        aa2 = _dot_nt(jnp.concatenate([lhs * only_a2, lhs * only_b2], axis=0), rhs)
        aa_a.append(jnp.where(aa_mask, aa2[:2 * c], 0.0))
        aa_b.append(jnp.where(aa_mask, aa2[2 * c:], 0.0))
    av = []
    for i in range(ni):
        vb = vh[i].astype(BF16)
        v_rows = jnp.concatenate([zeros_b, vb * only_a, zeros_b, vb * only_b], axis=0)
        av.append(_dot(jnp.concatenate([aa_a[i], aa_b[i]], axis=1).astype(BF16), v_rows))
    a_ab = [jnp.where(head_a, aa_a[i][:c], pltpu.roll(aa_b[i][:c], n, axis=1)).astype(BF16) for i in range(ni)]
    a_rb = [jnp.where(head_a, aa_a[i][c:], pltpu.roll(aa_b[i][c:], n, axis=1)).astype(BF16) for i in range(ni)]

    tinv = [eye_f - jnp.where(diag2, a.astype(F32), 0.0) for a in a_ab]
    for li in range(len(levels)):
        tb = [t_.astype(BF16) for t_ in tinv]
        half = [_dot(tb[i], both(a_ab[i], lvl_a[li], lvl_b[li])).astype(BF16) for i in range(ni)]
        tinv = [tinv[i] - _dot(half[i], both(tb[i])) for i in range(ni)]

    w12 = [_dot(tinv[i].astype(BF16),
                jnp.concatenate([both(kkt[i].astype(BF16)), both(av[i][:c].astype(BF16))], axis=1))
           for i in range(ni)]
    w12_b = [w.astype(BF16) for w in w12]
    pq = [jnp.concatenate([rt[i], av[i][c:]], axis=1)
          - _dot(a_rb[i], jnp.concatenate([both(w12_b[i][:, :LANES]), both(w12_b[i][:, LANES:])], axis=1))
          for i in range(ni)]
    mg = []
    for i, ((bi, _), sl) in enumerate(zip(items, sls)):
        neg_bd_kd = jnp.concatenate([-pre[bi]["bd"][:, sl], pre[bi]["kd"][:, sl]], axis=0).astype(BF16)
        wv = jnp.concatenate([w12_b[i], jnp.concatenate([zeros_b, vh[i].astype(BF16)], axis=1)], axis=0)
        mg.append(_dot_tn(neg_bd_kd, wv))

    ys = []
    for i, ((bi, _), sl) in enumerate(zip(items, sls)):
        m_mat = jnp.where(block_diag, mg[i][:, :LANES], 0.0) + jnp.where(eye2, pre[bi]["e_tot"][:, sl], 0.0)
        pm = jnp.concatenate([pq[i][:, :LANES], m_mat], axis=0).astype(BF16)
        out = _dot(pm, state_ref[i].astype(BF16))
        ys.append(out[:c] + pq[i][:, LANES:])
        state_ref[i] = out[c:] + jnp.where(block_diag, mg[i][:, LANES:], 0.0)

    inv_n = 1.0 / n
    for i, ((bi, _), sl) in enumerate(zip(items, sls)):
        y = ys[i]
        yc = y - head_sums(y) * inv_n
        var = head_sums(yc * yc) * inv_n
        yn = yc * lax.rsqrt(var + GN_EPS) * lnw_ref[:, sl] + lnb_ref[:, sl]
        bonus = head_sums(pre[bi]["bonus"][:, sl]) * vh[i]
        y_ref[bi, :, sl] = (yn + bonus) * pre[bi]["g"][:, sl]


def _merge_kernel(x_ref, ya_ref, yb_ref, yc_ref, g_ref, wg_ref, bg_ref, wr_ref, wp_ref, wm_ref, wo_ref,
                  out_ref):
    x = x_ref[...]
    h = _rms_norm(x, g_ref[...]).astype(BF16)
    d = D_MODEL
    merged = None
    for i, (y_ref, w_ref) in enumerate(((ya_ref, wr_ref), (yb_ref, wp_ref), (yc_ref, wm_ref))):
        gate = jax.nn.sigmoid(_dot(h, wg_ref[:, i * d:(i + 1) * d]) + bg_ref[:, i * d:(i + 1) * d])
        term = gate * _dot(y_ref[...].astype(BF16), w_ref[...])
        merged = term if merged is None else merged + term
    out_ref[...] = x + _dot(merged.astype(BF16), wo_ref[...])


def _conv_ffn_kernel(x_ref, g_ref, win_ref, cw_ref, cb_ref, wout_ref, gf_ref, out_ref, tail_ref, act_ref):
    t = pl.program_id(1)
    tm = x_ref.shape[0]
    x = x_ref[...]
    h = _rms_norm(x, g_ref[...]).astype(BF16)
    row = lax.broadcasted_iota(jnp.int32, (tm, 1), 0)
    first = t == 0
    n_chunks = D_FF // FF_CHUNK

    def up_proj(j):
        return (_dot(h, win_ref[:, j * FF_CHUNK:(j + 1) * FF_CHUNK]),
                _dot(h, win_ref[:, D_FF + j * FF_CHUNK:D_FF + (j + 1) * FF_CHUNK]))

    nxt = up_proj(0)
    for j in range(n_chunks):
        cs = slice(j * FF_CHUNK, (j + 1) * FF_CHUNK)
        u, gv = nxt
        if j + 1 < n_chunks:
            nxt = up_proj(j + 1)
        tail = jnp.where(first, 0.0, tail_ref[:, cs])
        tail_ref[:, cs] = u[tm - 8:, :]
        u1 = jnp.where(row == 0, tail[7:8, :], pltpu.roll(u, 1, axis=0))
        u2 = jnp.where(row == 0, tail[6:7, :], jnp.where(row == 1, tail[7:8, :], pltpu.roll(u, 2, axis=0)))
        uc = cw_ref[0:1, cs] * u2 + cw_ref[1:2, cs] * u1 + cw_ref[2:3, cs] * u + cb_ref[:, cs]
        act = 0.5 * uc * (1.0 + lax.erf(uc * (2.0 ** -0.5))) * gv
        act_ref[:, cs] = act.astype(BF16)
    out_ref[...] = _rms_norm(x + _dot(act_ref[...], wout_ref[...]), gf_ref[...])


def _const_spec(shape):
    nd = len(shape)
    return pl.BlockSpec(shape, lambda *_: (0,) * nd)


def kernel(x, mem, norm_mix_g, w_in_mix, mu_shift, w0, w_lora_b, a0, a_lora_b, g_lora_b, k_k, k_a, r_k,
           ln_x_w, ln_x_b, pool_w, pool_scale, norm_mem_g, w_mem_kv, w_up_rwkv, w_up_pool, w_up_mem,
           w_gate, b_gate, w_o, norm_ffn_g, w_ffn_in, ffn_conv_w, ffn_conv_b, w_ffn_out, norm_final_g):
    bsz, seq, d = x.shape
    n_mem = mem.shape[1]
    assert d == D_MODEL and seq % TM_PROJ == 0 and seq % CHUNK == 0 and bsz % SCAN_BATCH == 0
    assert norm_mix_g.shape[0] == 1, "single-layer block"
    l = 0
    row = lambda a: a.reshape(1, -1).astype(F32)

    wmix = w_in_mix[l].astype(BF16)
    zpad = jnp.zeros((DECAY_LORA, RWKV_DIM), F32)
    wl_pad = jnp.concatenate([w_lora_b[l], zpad], axis=0).astype(BF16)
    al_pad = jnp.concatenate([zpad, a_lora_b[l]], axis=0).astype(BF16)
    gl = g_lora_b[l].astype(BF16)
    head_id = jnp.arange(LANES) // HEAD_DIM
    hsum = (head_id[:, None] == head_id[None, :]).astype(BF16)
    pool_bd = jax.scipy.linalg.block_diag(*[pool_w[l, i] for i in range(len(POOL_WINDOWS))]).astype(BF16)

    kv = pl.pallas_call(
        _mem_kv_kernel,
        out_shape=jax.ShapeDtypeStruct((bsz, n_mem, 2 * MEM_DIM), BF16),
        grid=(bsz,),
        in_specs=[pl.BlockSpec((None, n_mem, d), lambda b: (b, 0, 0)),
                  _const_spec((1, d)), _const_spec((d, 2 * MEM_DIM))],
        out_specs=pl.BlockSpec((None, n_mem, 2 * MEM_DIM), lambda b: (b, 0, 0)),
        compiler_params=pltpu.CompilerParams(dimension_semantics=("arbitrary",)),
        name="mem_kv",
    )(mem, row(norm_mem_g[l]), w_mem_kv[l].astype(BF16))

    tok = lambda w: pl.BlockSpec((None, TM_PROJ, w), lambda b, t: (b, t, 0))
    f32_out = lambda w: jax.ShapeDtypeStruct((bsz, seq, w), F32)
    outs = pl.pallas_call(
        _mix_proj_kernel,
        out_shape=[f32_out(RWKV_DIM)] * 7 + [f32_out(POOL_DIM), f32_out(MEM_DIM)],
        grid=(bsz, seq // TM_PROJ),
        in_specs=[tok(d), _const_spec((1, d)), _const_spec((d, MIX_IN)), _const_spec((1, RWKV_IN)),
                  _const_spec((1, RWKV_DIM)), _const_spec((DECAY_LORA + ICLR_LORA, RWKV_DIM)),
                  _const_spec((1, RWKV_DIM)), _const_spec((DECAY_LORA + ICLR_LORA, RWKV_DIM)),
                  _const_spec((GATE_LORA, RWKV_DIM)), _const_spec((1, RWKV_DIM)), _const_spec((1, RWKV_DIM)),
                  _const_spec((LANES, LANES)), _const_spec((POOL_DIM, POOL_DIM)),
                  _const_spec((1, POOL_DIM)),
                  pl.BlockSpec((None, n_mem, 2 * MEM_DIM), lambda b, t: (b, 0, 0))],
        out_specs=[tok(RWKV_DIM)] * 7 + [tok(POOL_DIM), tok(MEM_DIM)],
        scratch_shapes=[pltpu.VMEM((1, RWKV_IN), F32), pltpu.VMEM((POOL_HALO, POOL_DIM), F32)],
        compiler_params=pltpu.CompilerParams(dimension_semantics=("arbitrary", "arbitrary"),
                                             vmem_limit_bytes=V7X_VMEM_LIMIT_BYTES),
        name="mix_proj",
    )(x, row(norm_mix_g[l]), wmix, row(mu_shift[l]), row(w0[l]), wl_pad, row(a0[l]), al_pad, gl,
      row(k_k[l]), row(k_a[l]), hsum, pool_bd, row(pool_scale[l]), kv)
    r_s, k_s, v_s, kk_s, b_s, lw_s, g_s, y_b, y_c = outs

    sb = SCAN_BATCH
    blk = pl.BlockSpec((sb, CHUNK, RWKV_DIM), lambda b, c: (b, c, 0))
    y_a = pl.pallas_call(
        _rwkv_scan_kernel,
        out_shape=f32_out(RWKV_DIM),
        grid=(bsz // sb, seq // CHUNK),
        in_specs=[blk] * 7 + [_const_spec((1, RWKV_DIM))] * 3,
        out_specs=blk,
        scratch_shapes=[pltpu.VMEM((sb * RWKV_DIM // LANES, LANES, LANES), F32)],
        compiler_params=pltpu.CompilerParams(dimension_semantics=("arbitrary", "arbitrary")),
        name="rwkv_scan",
    )(r_s, k_s, v_s, kk_s, b_s, lw_s, g_s, row(r_k[l]), row(ln_x_w[l]), row(ln_x_b[l]))

    n_tok = bsz * seq
    flat = lambda a: a.reshape(n_tok, a.shape[-1])
    tokm = lambda w: pl.BlockSpec((TM_MERGE, w), lambda i: (i, 0))
    x1 = pl.pallas_call(
        _merge_kernel,
        out_shape=jax.ShapeDtypeStruct((n_tok, d), F32),
        grid=(n_tok // TM_MERGE,),
        in_specs=[tokm(d), tokm(RWKV_DIM), tokm(POOL_DIM), tokm(MEM_DIM), _const_spec((1, d)),
                  _const_spec((d, 3 * d)), _const_spec((1, 3 * d)), _const_spec((RWKV_DIM, d)),
                  _const_spec((POOL_DIM, d)), _const_spec((MEM_DIM, d)), _const_spec((d, d))],
        out_specs=tokm(d),
        compiler_params=pltpu.CompilerParams(dimension_semantics=("arbitrary",),
                                             vmem_limit_bytes=V7X_VMEM_LIMIT_BYTES),
        name="merge",
    )(flat(x), flat(y_a), flat(y_b), flat(y_c), row(norm_mix_g[l]), w_gate[l].astype(BF16),
      row(b_gate[l]), w_up_rwkv[l].astype(BF16), w_up_pool[l].astype(BF16), w_up_mem[l].astype(BF16),
      w_o[l].astype(BF16))

    tokf = pl.BlockSpec((None, TM_FFN, d), lambda b, t: (b, t, 0))
    single = pl.Buffered(1)
    out = pl.pallas_call(
        _conv_ffn_kernel,
        out_shape=jax.ShapeDtypeStruct((bsz, seq, d), F32),
        grid=(bsz, seq // TM_FFN),
        in_specs=[tokf, _const_spec((1, d)),
                  pl.BlockSpec((d, 2 * D_FF), lambda b, t: (0, 0), pipeline_mode=single),
                  _const_spec((3, D_FF)), _const_spec((1, D_FF)),
                  pl.BlockSpec((D_FF, d), lambda b, t: (0, 0), pipeline_mode=single),
                  _const_spec((1, d))],
        out_specs=tokf,
        scratch_shapes=[pltpu.VMEM((8, D_FF), F32), pltpu.VMEM((TM_FFN, D_FF), BF16)],
        compiler_params=pltpu.CompilerParams(dimension_semantics=("arbitrary", "arbitrary"),
                                             vmem_limit_bytes=V7X_VMEM_LIMIT_BYTES),
        name="conv_ffn",
    )(x1.reshape(bsz, seq, d), row(norm_ffn_g[l]), w_ffn_in[l].astype(BF16), ffn_conv_w[l].astype(F32),
      row(ffn_conv_b[l]), w_ffn_out[l].astype(BF16), row(norm_final_g))
    return out
```

```python
import functools
import math

import jax
import jax.numpy as jnp
from jax import lax
from jax.experimental import pallas as pl
from jax.experimental.pallas import tpu as pltpu

F32 = jnp.float32
BF16 = jnp.bfloat16

D_MODEL = 1024
HEAD_DIM = 64
RWKV_HEADS = 8
RWKV_DIM = RWKV_HEADS * HEAD_DIM
DECAY_LORA = 64
ICLR_LORA = 64
GATE_LORA = 128
RWKV_IN = 3 * RWKV_DIM + DECAY_LORA + ICLR_LORA + GATE_LORA
POOL_WINDOWS = (2, 4, 8, 16)
POOL_GROUP_DIM = 64
POOL_DIM = len(POOL_WINDOWS) * POOL_GROUP_DIM
POOL_HALO = 16
MEM_HEADS = 4
MEM_DIM = MEM_HEADS * HEAD_DIM
MIX_IN = RWKV_IN + POOL_DIM + MEM_DIM
D_FF = 2816
NORM_EPS = 1e-6
GN_EPS = 64e-5
CHUNK = 64
FF_CHUNK = 256
V7X_VMEM_LIMIT_BYTES = 56 * 1024 * 1024
LANES = 128

TM_PROJ = 512
TM_MERGE = 1024
TM_FFN = 1024
SCAN_BATCH = 8


def _dot(a, b):
    return jnp.dot(a, b, preferred_element_type=F32)


def _dot_nt(a, b):
    return lax.dot_general(a, b, (((1,), (1,)), ((), ())), preferred_element_type=F32)


def _dot_tn(a, b):
    return lax.dot_general(a, b, (((0,), (0,)), ((), ())), preferred_element_type=F32)


def _rms_norm(x, g):
    return x * lax.rsqrt(jnp.mean(x * x, axis=-1, keepdims=True) + NORM_EPS) * g


def _split_dot(x, w_bf16):
    hi = x.astype(BF16)
    lo = (x - hi.astype(F32)).astype(BF16)
    return _dot(hi, w_bf16) + _dot(lo, w_bf16)


def _split3_dot_left(w_bf16, x):
    hi = x.astype(BF16)
    r1 = x - hi.astype(F32)
    mid = r1.astype(BF16)
    lo = (r1 - mid.astype(F32)).astype(BF16)
    return _dot(w_bf16, hi) + _dot(w_bf16, mid) + _dot(w_bf16, lo)


def _mem_kv_kernel(mem_ref, g_ref, w_ref, kv_ref):
    m = _rms_norm(mem_ref[...], g_ref[...])
    kv_ref[...] = _dot(m.astype(BF16), w_ref[...]).astype(BF16)


def _mix_proj_kernel(x_ref, g_ref, wmix_ref, mu_ref, w0_ref, wl_ref, a0_ref, al_ref, gl_ref,
                     kk_ref, ka_ref, hsum_ref, poolw_ref, pools_ref, kv_ref,
                     r_out, k_out, v_out, kkn_out, b_out, lw_out, g_out, yb_out, yc_out,
                     prev_ref, halo_ref):
    t = pl.program_id(1)
    tm = x_ref.shape[0]
    first = t == 0

    h = _rms_norm(x_ref[...], g_ref[...]).astype(BF16)
    row = lax.broadcasted_iota(jnp.int32, (tm, 1), 0)
    lo0 = 3 * RWKV_DIM

    def proj(c0, c1):
        return _dot(h, wmix_ref[:, c0:c1])

    def token_shift(ps, c0, c1):
        carry = jnp.where(first, 0.0, prev_ref[:, c0:c1])
        prev = jnp.where(row == 0, carry, pltpu.roll(ps, 1, axis=0))
        prev_ref[:, c0:c1] = ps[tm - 1:tm, :]
        return ps + (prev - ps) * mu_ref[:, c0:c1]

    p_pq = proj(RWKV_IN, MIX_IN)
    p_lo = proj(lo0, RWKV_IN)
    p_r = proj(0, RWKV_DIM)
    pp = p_pq[:, :POOL_DIM]
    q = p_pq[:, POOL_DIM:]

    kmem = kv_ref[:, :MEM_DIM]
    vmem = kv_ref[:, MEM_DIM:]
    lane_head = lax.broadcasted_iota(jnp.int32, (1, MEM_DIM), 1) // HEAD_DIM
    scores = [_dot_nt(jnp.where(lane_head == hd, q, 0.0).astype(BF16), kmem) * (HEAD_DIM ** -0.5)
              for hd in range(MEM_HEADS)]

    p_k = proj(RWKV_DIM, 2 * RWKV_DIM)
    p_v = proj(2 * RWKV_DIM, lo0)

    lo = token_shift(p_lo, lo0, RWKV_IN)
    wa = lo[:, :DECAY_LORA + ICLR_LORA]
    gd = lo[:, DECAY_LORA + ICLR_LORA:]
    z = w0_ref[...] + _dot(jnp.tanh(wa).astype(BF16), wl_ref[...])
    lw_out[...] = (-math.exp(-0.5)) * jax.nn.sigmoid(z)
    a = jax.nn.sigmoid(a0_ref[...] + _dot(wa.astype(BF16), al_ref[...]))
    g_out[...] = _dot(jax.nn.sigmoid(gd).astype(BF16), gl_ref[...])

    halo = jnp.where(first, 0.0, halo_ref[...])
    halo_ref[...] = pp[tm - POOL_HALO:, :]
    ext = jnp.concatenate([halo, pp], axis=0)
    lane_group = lax.broadcasted_iota(jnp.int32, (1, POOL_DIM), 1) // POOL_GROUP_DIM
    win = jnp.zeros((1, POOL_DIM), jnp.int32)
    sel = jnp.zeros((tm, POOL_DIM), F32)
    acc = ext
    shift = 1
    for gi, w in enumerate(POOL_WINDOWS):
        while shift < w:
            acc = acc + pltpu.roll(acc, shift, axis=0)
            shift *= 2
        sel = jnp.where(lane_group == gi, acc[POOL_HALO:, :], sel)
        win = jnp.where(lane_group == gi, w, win)
    pos = t * tm + row + 1
    cnt = jnp.minimum(pos, win).astype(F32)
    dpool = sel / cnt - pp
    yb_out[...] = _dot(dpool.astype(BF16), poolw_ref[...]) * pools_ref[...]

    cols = []
    for j in range(0, MEM_DIM, LANES):
        col = None
        for hd in range(j // HEAD_DIM, (j + LANES) // HEAD_DIM):
            s = scores[hd]
            e = jnp.exp(s - jnp.max(s, axis=-1, keepdims=True))
            prob = e / jnp.sum(e, axis=-1, keepdims=True)
            vh = jnp.where((lane_head == hd)[:, j:j + LANES], vmem[:, j:j + LANES], 0.0)
            o = _dot(prob.astype(BF16), vh)
            col = o if col is None else col + o
        cols.append(col)
    yc_out[...] = jnp.concatenate(cols, axis=1)

    r = token_shift(p_r, 0, RWKV_DIM)
    k = token_shift(p_k, RWKV_DIM, 2 * RWKV_DIM)
    v = token_shift(p_v, 2 * RWKV_DIM, lo0)
    kkr = k * kk_ref[...]
    kk_sq = kkr * kkr
    ssq = jnp.concatenate([_split_dot(kk_sq[:, j:j + LANES], hsum_ref[...])
                           for j in range(0, RWKV_DIM, LANES)], axis=1)
    kkn = kkr * lax.rsqrt(ssq + 1e-12)
    r_out[...] = r
    v_out[...] = v
    k_out[...] = k * (1.0 + (a - 1.0) * ka_ref[...])
    kkn_out[...] = kkn
    b_out[...] = kkn * a


def _chunk_masks():
    c = CHUNK
    ri = lax.broadcasted_iota(jnp.int32, (2 * c, 2 * c), 0)
    ci = lax.broadcasted_iota(jnp.int32, (2 * c, 2 * c), 1)
    rt = ri % c
    ct = ci % c
    aa_mask = (rt > ct) | ((ri >= c) & (rt == ct))
    r64 = lax.broadcasted_iota(jnp.int32, (c, 2 * c), 0)
    c64 = lax.broadcasted_iota(jnp.int32, (c, 2 * c), 1) % c
    eye = r64 == c64
    levels = []
    blk = 2
    while blk < c:
        same_outer = (r64 // (2 * blk)) == (c64 // (2 * blk))
        diff_inner = (r64 // blk) != (c64 // blk)
        levels.append(same_outer & diff_inner & (r64 > c64))
        blk *= 2
    diag2 = ((r64 // 2) == (c64 // 2)) & (r64 > c64)
    tri_incl = lax.broadcasted_iota(jnp.int32, (c, c), 0) >= lax.broadcasted_iota(jnp.int32, (c, c), 1)
    block_diag = (ri < c) == (ci < c)
    eye2 = ri == ci
    return aa_mask, eye, diag2, levels, tri_incl, block_diag, eye2


def _rwkv_scan_kernel(r_ref, k_ref, v_ref, kk_ref, b_ref, lw_ref, g_ref, rk_ref, lnw_ref, lnb_ref,
                      y_ref, state_ref):
    c = CHUNK
    n = HEAD_DIM
    chunk_idx = pl.program_id(1)

    @pl.when(chunk_idx == 0)
    def _():
        state_ref[...] = jnp.zeros_like(state_ref)

    aa_mask, eye, diag2, levels, tri_incl, block_diag, eye2 = _chunk_masks()
    tri = jnp.where(tri_incl, 1.0, 0.0).astype(BF16)
    eye_f = jnp.where(eye, 1.0, 0.0)
    zeros_b = jnp.zeros((c, LANES), BF16)
    head_a = lax.broadcasted_iota(jnp.int32, (c, LANES), 1) < n
    only_a = jnp.where(head_a, 1.0, 0.0).astype(BF16)
    only_b = jnp.where(head_a, 0.0, 1.0).astype(BF16)
    only_a2 = jnp.concatenate([only_a, only_a], axis=0)
    only_b2 = jnp.concatenate([only_b, only_b], axis=0)
    lvl_a = [jnp.where(lvl & head_a, 1.0, 0.0).astype(BF16) for lvl in levels]
    lvl_b = [jnp.where(lvl & ~head_a, 1.0, 0.0).astype(BF16) for lvl in levels]

    def both(xb, ma=only_a, mb=only_b):
        return jnp.concatenate([xb * ma, xb * mb], axis=0)

    ones_bd = jnp.where(block_diag, 1.0, 0.0).astype(BF16)

    def head_sums_mxu(x):
        hi = x.astype(BF16)
        lo = (x - hi.astype(F32)).astype(BF16)
        s2 = _dot(jnp.concatenate([hi, lo], axis=0), ones_bd)
        return s2[:c] + s2[c:]

    def head_sums(x):
        sa = jnp.sum(jnp.where(head_a, x, 0.0), axis=-1, keepdims=True)
        sb = jnp.sum(jnp.where(head_a, 0.0, x), axis=-1, keepdims=True)
        return jnp.where(head_a, sa, sb)

    nb = r_ref.shape[0]
    items = [(bi, j) for bi in range(nb) for j in range(RWKV_DIM // LANES)]
    sls = [slice(j * LANES, (j + 1) * LANES) for _, j in items]
    ni = len(items)

    pre = []
    for bi in range(nb):
        lw = lw_ref[bi]
        cum = _split3_dot_left(tri, lw)
        tot = cum[c - 1:c, :]
        e_inv = jnp.exp(-cum)
        e_end = jnp.exp(tot - cum)
        r_all = r_ref[bi]
        k_all = k_ref[bi]
        b_all = b_ref[bi]
        pre.append(dict(
            rt=r_all * jnp.exp(cum), kkt=kk_ref[bi] * jnp.exp(cum - lw), bt=b_all * e_inv, kt=k_all * e_inv,
            bd=b_all * e_end, kd=k_all * e_end, e_tot=jnp.exp(tot), v=v_ref[bi], g=g_ref[bi],
            bonus=r_all * k_all * rk_ref[...]))

    rt = [pre[bi]["rt"][:, sl] for (bi, _), sl in zip(items, sls)]
    kkt = [pre[bi]["kkt"][:, sl] for (bi, _), sl in zip(items, sls)]
    vh = [pre[bi]["v"][:, sl] for (bi, _), sl in zip(items, sls)]
    aa_a, aa_b = [], []
    for i, ((bi, _), sl) in enumerate(zip(items, sls)):
        lhs = jnp.concatenate([kkt[i], rt[i]], axis=0).astype(BF16)
        rhs = jnp.concatenate([pre[bi]["bt"][:, sl], pre[bi]["kt"][:, sl]], axis=0).astype(BF16)
        aa2 = _dot_nt(jnp.concatenate([lhs * only_a2, lhs * only_b2], axis=0), rhs)
        aa_a.append(jnp.where(aa_mask, aa2[:2 * c], 0.0))
        aa_b.append(jnp.where(aa_mask, aa2[2 * c:], 0.0))
    av = []
    for i in range(ni):
        vb = vh[i].astype(BF16)
        v_rows = jnp.concatenate([zeros_b, vb * only_a, zeros_b, vb * only_b], axis=0)
        av.append(_dot(jnp.concatenate([aa_a[i], aa_b[i]], axis=1).astype(BF16), v_rows))
    a_ab = [jnp.where(head_a, aa_a[i][:c], pltpu.roll(aa_b[i][:c], n, axis=1)).astype(BF16) for i in range(ni)]
    a_rb = [jnp.where(head_a, aa_a[i][c:], pltpu.roll(aa_b[i][c:], n, axis=1)).astype(BF16) for i in range(ni)]

    tinv = [eye_f - jnp.where(diag2, a.astype(F32), 0.0) for a in a_ab]
    for li in range(len(levels)):
        tb = [t_.astype(BF16) for t_ in tinv]
        half = [_dot(tb[i], both(a_ab[i], lvl_a[li], lvl_b[li])).astype(BF16) for i in range(ni)]
        tinv = [tinv[i] - _dot(half[i], both(tb[i])) for i in range(ni)]

    w12 = [_dot(tinv[i].astype(BF16),
                jnp.concatenate([both(kkt[i].astype(BF16)), both(av[i][:c].astype(BF16))], axis=1))
           for i in range(ni)]
    w12_b = [w.astype(BF16) for w in w12]
    pq = [jnp.concatenate([rt[i], av[i][c:]], axis=1)
          - _dot(a_rb[i], jnp.concatenate([both(w12_b[i][:, :LANES]), both(w12_b[i][:, LANES:])], axis=1))
          for i in range(ni)]
    mg = []
    for i, ((bi, _), sl) in enumerate(zip(items, sls)):
        neg_bd_kd = jnp.concatenate([-pre[bi]["bd"][:, sl], pre[bi]["kd"][:, sl]], axis=0).astype(BF16)
        wv = jnp.concatenate([w12_b[i], jnp.concatenate([zeros_b, vh[i].astype(BF16)], axis=1)], axis=0)
        mg.append(_dot_tn(neg_bd_kd, wv))

    ys = []
    for i, ((bi, _), sl) in enumerate(zip(items, sls)):
        m_mat = jnp.where(block_diag, mg[i][:, :LANES], 0.0) + jnp.where(eye2, pre[bi]["e_tot"][:, sl], 0.0)
        pm = jnp.concatenate([pq[i][:, :LANES], m_mat], axis=0).astype(BF16)
        out = _dot(pm, state_ref[i].astype(BF16))
        ys.append(out[:c] + pq[i][:, LANES:])
        state_ref[i] = out[c:] + jnp.where(block_diag, mg[i][:, LANES:], 0.0)

    inv_n = 1.0 / n
    for i, ((bi, _), sl) in enumerate(zip(items, sls)):
        y = ys[i]
        yc = y - head_sums(y) * inv_n
        var = head_sums_mxu(yc * yc) * inv_n
        yn = yc * lax.rsqrt(var + GN_EPS) * lnw_ref[:, sl] + lnb_ref[:, sl]
        bonus = head_sums_mxu(pre[bi]["bonus"][:, sl]) * vh[i]
        y_ref[bi, :, sl] = (yn + bonus) * pre[bi]["g"][:, sl]


def _merge_kernel(x_ref, ya_ref, yb_ref, yc_ref, g_ref, wg_ref, bg_ref, wr_ref, wp_ref, wm_ref, wo_ref,
                  out_ref):
    x = x_ref[...]
    h = _rms_norm(x, g_ref[...]).astype(BF16)
    d = D_MODEL
    merged = None
    for i, (y_ref, w_ref) in enumerate(((ya_ref, wr_ref), (yb_ref, wp_ref), (yc_ref, wm_ref))):
        gate = jax.nn.sigmoid(_dot(h, wg_ref[:, i * d:(i + 1) * d]) + bg_ref[:, i * d:(i + 1) * d])
        term = gate * _dot(y_ref[...].astype(BF16), w_ref[...])
        merged = term if merged is None else merged + term
    out_ref[...] = x + _dot(merged.astype(BF16), wo_ref[...])


def _conv_ffn_kernel(x_ref, g_ref, win_ref, cw_ref, cb_ref, wout_ref, gf_ref, out_ref, tail_ref, act_ref):
    t = pl.program_id(1)
    tm = x_ref.shape[0]
    x = x_ref[...]
    h = _rms_norm(x, g_ref[...]).astype(BF16)
    row = lax.broadcasted_iota(jnp.int32, (tm, 1), 0)
    first = t == 0
    n_chunks = D_FF // FF_CHUNK

    def up_proj(j):
        return (_dot(h, win_ref[:, j * FF_CHUNK:(j + 1) * FF_CHUNK]),
                _dot(h, win_ref[:, D_FF + j * FF_CHUNK:D_FF + (j + 1) * FF_CHUNK]))

    nxt = up_proj(0)
    for j in range(n_chunks):
        cs = slice(j * FF_CHUNK, (j + 1) * FF_CHUNK)
        u, gv = nxt
        if j + 1 < n_chunks:
            nxt = up_proj(j + 1)
        tail = jnp.where(first, 0.0, tail_ref[:, cs])
        tail_ref[:, cs] = u[tm - 8:, :]
        u1 = jnp.where(row == 0, tail[7:8, :], pltpu.roll(u, 1, axis=0))
        u2 = jnp.where(row == 0, tail[6:7, :], jnp.where(row == 1, tail[7:8, :], pltpu.roll(u, 2, axis=0)))
        uc = cw_ref[0:1, cs] * u2 + cw_ref[1:2, cs] * u1 + cw_ref[2:3, cs] * u + cb_ref[:, cs]
        act = 0.5 * uc * (1.0 + lax.erf(uc * (2.0 ** -0.5))) * gv
        act_ref[:, cs] = act.astype(BF16)
    out_ref[...] = _rms_norm(x + _dot(act_ref[...], wout_ref[...]), gf_ref[...])


def _const_spec(shape):
    nd = len(shape)
    return pl.BlockSpec(shape, lambda *_: (0,) * nd)


def kernel(x, mem, norm_mix_g, w_in_mix, mu_shift, w0, w_lora_b, a0, a_lora_b, g_lora_b, k_k, k_a, r_k,
           ln_x_w, ln_x_b, pool_w, pool_scale, norm_mem_g, w_mem_kv, w_up_rwkv, w_up_pool, w_up_mem,
           w_gate, b_gate, w_o, norm_ffn_g, w_ffn_in, ffn_conv_w, ffn_conv_b, w_ffn_out, norm_final_g):
    bsz, seq, d = x.shape
    n_mem = mem.shape[1]
    assert d == D_MODEL and seq % TM_PROJ == 0 and seq % CHUNK == 0 and bsz % SCAN_BATCH == 0
    assert seq % TM_FFN == 0 and (bsz * seq) % TM_MERGE == 0
    assert norm_mix_g.shape[0] == 1, "single-layer block"
    l = 0
    row = lambda a: a.reshape(1, -1).astype(F32)

    wmix = w_in_mix[l].astype(BF16)
    zpad = jnp.zeros((DECAY_LORA, RWKV_DIM), F32)
    wl_pad = jnp.concatenate([w_lora_b[l], zpad], axis=0).astype(BF16)
    al_pad = jnp.concatenate([zpad, a_lora_b[l]], axis=0).astype(BF16)
    gl = g_lora_b[l].astype(BF16)
    head_id = jnp.arange(LANES) // HEAD_DIM
    hsum = (head_id[:, None] == head_id[None, :]).astype(BF16)
    pool_bd = jax.scipy.linalg.block_diag(*[pool_w[l, i] for i in range(len(POOL_WINDOWS))]).astype(BF16)

    kv = pl.pallas_call(
        _mem_kv_kernel,
        out_shape=jax.ShapeDtypeStruct((bsz, n_mem, 2 * MEM_DIM), BF16),
        grid=(bsz,),
        in_specs=[pl.BlockSpec((None, n_mem, d), lambda b: (b, 0, 0)),
                  _const_spec((1, d)), _const_spec((d, 2 * MEM_DIM))],
        out_specs=pl.BlockSpec((None, n_mem, 2 * MEM_DIM), lambda b: (b, 0, 0)),
        compiler_params=pltpu.CompilerParams(dimension_semantics=("arbitrary",)),
        name="mem_kv",
    )(mem, row(norm_mem_g[l]), w_mem_kv[l].astype(BF16))

    tok = lambda w: pl.BlockSpec((None, TM_PROJ, w), lambda b, t: (b, t, 0))
    f32_out = lambda w: jax.ShapeDtypeStruct((bsz, seq, w), F32)
    outs = pl.pallas_call(
        _mix_proj_kernel,
        out_shape=[f32_out(RWKV_DIM)] * 7 + [f32_out(POOL_DIM), f32_out(MEM_DIM)],
        grid=(bsz, seq // TM_PROJ),
        in_specs=[tok(d), _const_spec((1, d)), _const_spec((d, MIX_IN)), _const_spec((1, RWKV_IN)),
                  _const_spec((1, RWKV_DIM)), _const_spec((DECAY_LORA + ICLR_LORA, RWKV_DIM)),
                  _const_spec((1, RWKV_DIM)), _const_spec((DECAY_LORA + ICLR_LORA, RWKV_DIM)),
                  _const_spec((GATE_LORA, RWKV_DIM)), _const_spec((1, RWKV_DIM)), _const_spec((1, RWKV_DIM)),
                  _const_spec((LANES, LANES)), _const_spec((POOL_DIM, POOL_DIM)),
                  _const_spec((1, POOL_DIM)),
                  pl.BlockSpec((None, n_mem, 2 * MEM_DIM), lambda b, t: (b, 0, 0))],
        out_specs=[tok(RWKV_DIM)] * 7 + [tok(POOL_DIM), tok(MEM_DIM)],
        scratch_shapes=[pltpu.VMEM((1, RWKV_IN), F32), pltpu.VMEM((POOL_HALO, POOL_DIM), F32)],
        compiler_params=pltpu.CompilerParams(dimension_semantics=("arbitrary", "arbitrary"),
                                             vmem_limit_bytes=V7X_VMEM_LIMIT_BYTES),
        name="mix_proj",
    )(x, row(norm_mix_g[l]), wmix, row(mu_shift[l]), row(w0[l]), wl_pad, row(a0[l]), al_pad, gl,
      row(k_k[l]), row(k_a[l]), hsum, pool_bd, row(pool_scale[l]), kv)
    r_s, k_s, v_s, kk_s, b_s, lw_s, g_s, y_b, y_c = outs

    sb = SCAN_BATCH
    blk = pl.BlockSpec((sb, CHUNK, RWKV_DIM), lambda b, c: (b, c, 0))
    y_a = pl.pallas_call(
        _rwkv_scan_kernel,
        out_shape=f32_out(RWKV_DIM),
        grid=(bsz // sb, seq // CHUNK),
        in_specs=[blk] * 7 + [_const_spec((1, RWKV_DIM))] * 3,
        out_specs=blk,
        scratch_shapes=[pltpu.VMEM((sb * RWKV_DIM // LANES, LANES, LANES), F32)],
        compiler_params=pltpu.CompilerParams(dimension_semantics=("arbitrary", "arbitrary")),
        name="rwkv_scan",
    )(r_s, k_s, v_s, kk_s, b_s, lw_s, g_s, row(r_k[l]), row(ln_x_w[l]), row(ln_x_b[l]))

    n_tok = bsz * seq
    flat = lambda a: a.reshape(n_tok, a.shape[-1])
    tokm = lambda w: pl.BlockSpec((TM_MERGE, w), lambda i: (i, 0))
    x1 = pl.pallas_call(
        _merge_kernel,
        out_shape=jax.ShapeDtypeStruct((n_tok, d), F32),
        grid=(n_tok // TM_MERGE,),
        in_specs=[tokm(d), tokm(RWKV_DIM), tokm(POOL_DIM), tokm(MEM_DIM), _const_spec((1, d)),
                  _const_spec((d, 3 * d)), _const_spec((1, 3 * d)), _const_spec((RWKV_DIM, d)),
                  _const_spec((POOL_DIM, d)), _const_spec((MEM_DIM, d)), _const_spec((d, d))],
        out_specs=tokm(d),
        compiler_params=pltpu.CompilerParams(dimension_semantics=("arbitrary",),
                                             vmem_limit_bytes=V7X_VMEM_LIMIT_BYTES),
        name="merge",
    )(flat(x), flat(y_a), flat(y_b), flat(y_c), row(norm_mix_g[l]), w_gate[l].astype(BF16),
      row(b_gate[l]), w_up_rwkv[l].astype(BF16), w_up_pool[l].astype(BF16), w_up_mem[l].astype(BF16),
      w_o[l].astype(BF16))

    tokf = pl.BlockSpec((None, TM_FFN, d), lambda b, t: (b, t, 0))
    single = pl.Buffered(1)
    out = pl.pallas_call(
        _conv_ffn_kernel,
        out_shape=jax.ShapeDtypeStruct((bsz, seq, d), F32),
        grid=(bsz, seq // TM_FFN),
        in_specs=[tokf, _const_spec((1, d)),
                  pl.BlockSpec((d, 2 * D_FF), lambda b, t: (0, 0), pipeline_mode=single),
                  _const_spec((3, D_FF)), _const_spec((1, D_FF)),
                  pl.BlockSpec((D_FF, d), lambda b, t: (0, 0), pipeline_mode=single),
                  _const_spec((1, d))],
        out_specs=tokf,
        scratch_shapes=[pltpu.VMEM((8, D_FF), F32), pltpu.VMEM((TM_FFN, D_FF), BF16)],
        compiler_params=pltpu.CompilerParams(dimension_semantics=("arbitrary", "arbitrary"),
                                             vmem_limit_bytes=V7X_VMEM_LIMIT_BYTES),
        name="conv_ffn",
    )(x1.reshape(bsz, seq, d), row(norm_ffn_g[l]), w_ffn_in[l].astype(BF16), ffn_conv_w[l].astype(F32),
      row(ffn_conv_b[l]), w_ffn_out[l].astype(BF16), row(norm_final_g))
    return out
```

```python
import functools
import math

import jax
import jax.numpy as jnp
from jax import lax
from jax.experimental import pallas as pl
from jax.experimental.pallas import tpu as pltpu

F32 = jnp.float32
BF16 = jnp.bfloat16

D_MODEL = 1024
HEAD_DIM = 64
RWKV_HEADS = 8
RWKV_DIM = RWKV_HEADS * HEAD_DIM
DECAY_LORA = 64
ICLR_LORA = 64
GATE_LORA = 128
RWKV_IN = 3 * RWKV_DIM + DECAY_LORA + ICLR_LORA + GATE_LORA
POOL_WINDOWS = (2, 4, 8, 16)
POOL_GROUP_DIM = 64
POOL_DIM = len(POOL_WINDOWS) * POOL_GROUP_DIM
POOL_HALO = 16
MEM_HEADS = 4
MEM_DIM = MEM_HEADS * HEAD_DIM
MIX_IN = RWKV_IN + POOL_DIM + MEM_DIM
D_FF = 2816
NORM_EPS = 1e-6
GN_EPS = 64e-5
CHUNK = 64
FF_CHUNK = 256
V7X_VMEM_LIMIT_BYTES = 56 * 1024 * 1024
LANES = 128

TM_PROJ = 512
TM_MERGE = 512
TM_FFN = 1024
SCAN_BATCH = 8


def _dot(a, b):
    return jnp.dot(a, b, preferred_element_type=F32)


def _dot_nt(a, b):
    return lax.dot_general(a, b, (((1,), (1,)), ((), ())), preferred_element_type=F32)


def _dot_tn(a, b):
    return lax.dot_general(a, b, (((0,), (0,)), ((), ())), preferred_element_type=F32)


def _rms_norm(x, g):
    return x * lax.rsqrt(jnp.mean(x * x, axis=-1, keepdims=True) + NORM_EPS) * g


def _split3_dot_left(w_bf16, x):
    hi = x.astype(BF16)
    r1 = x - hi.astype(F32)
    mid = r1.astype(BF16)
    lo = (r1 - mid.astype(F32)).astype(BF16)
    return _dot(w_bf16, hi) + _dot(w_bf16, mid) + _dot(w_bf16, lo)


def _head_lane_sums(x):
    first_head = lax.broadcasted_iota(jnp.int32, (1, LANES), 1) < HEAD_DIM
    cols = []
    for j in range(0, x.shape[1], LANES):
        col = x[:, j:j + LANES]
        sa = jnp.sum(jnp.where(first_head, col, 0.0), axis=-1, keepdims=True)
        sb = jnp.sum(jnp.where(first_head, 0.0, col), axis=-1, keepdims=True)
        cols.append(jnp.where(first_head, sa, sb))
    return cols[0] if len(cols) == 1 else jnp.concatenate(cols, axis=1)


def _mem_kv_kernel(mem_ref, g_ref, w_ref, kv_ref):
    m = _rms_norm(mem_ref[...], g_ref[...])
    kv_ref[...] = _dot(m.astype(BF16), w_ref[...]).astype(BF16)


def _mix_proj_kernel(x_ref, g_ref, wmix_ref, mu_ref, w0_ref, wl_ref, a0_ref, al_ref, gl_ref,
                     kk_ref, ka_ref, rk_ref, poolw_ref, pools_ref, kv_ref,
                     r_out, k_out, v_out, kkn_out, b_out, lw_out, g_out, bv_out, yb_out, yc_out,
                     prev_ref, halo_ref):
    t = pl.program_id(1)
    tm = x_ref.shape[0]
    first = t == 0

    h = _rms_norm(x_ref[...], g_ref[...]).astype(BF16)
    row = lax.broadcasted_iota(jnp.int32, (tm, 1), 0)
    lo0 = 3 * RWKV_DIM

    def proj(c0, c1):
        return _dot(h, wmix_ref[:, c0:c1])

    def token_shift(ps, c0, c1):
        carry = jnp.where(first, 0.0, prev_ref[:, c0:c1])
        prev = jnp.where(row == 0, carry, pltpu.roll(ps, 1, axis=0))
        prev_ref[:, c0:c1] = ps[tm - 1:tm, :]
        return ps + (prev - ps) * mu_ref[:, c0:c1]

    p_pq = proj(RWKV_IN, MIX_IN)
    p_lo = proj(lo0, RWKV_IN)
    p_r = proj(0, RWKV_DIM)
    pp = p_pq[:, :POOL_DIM]
    q = p_pq[:, POOL_DIM:]

    kmem = kv_ref[:, :MEM_DIM]
    vmem = kv_ref[:, MEM_DIM:]
    lane_head = lax.broadcasted_iota(jnp.int32, (1, MEM_DIM), 1) // HEAD_DIM
    scores = [_dot_nt(jnp.where(lane_head == hd, q, 0.0).astype(BF16), kmem) * (HEAD_DIM ** -0.5)
              for hd in range(MEM_HEADS)]

    p_k = proj(RWKV_DIM, 2 * RWKV_DIM)
    p_v = proj(2 * RWKV_DIM, lo0)

    lo = token_shift(p_lo, lo0, RWKV_IN)
    wa = lo[:, :DECAY_LORA + ICLR_LORA]
    gd = lo[:, DECAY_LORA + ICLR_LORA:]
    z = w0_ref[...] + _dot(jnp.tanh(wa).astype(BF16), wl_ref[...])
    lw_out[...] = (-math.exp(-0.5)) * jax.nn.sigmoid(z)
    a = jax.nn.sigmoid(a0_ref[...] + _dot(wa.astype(BF16), al_ref[...]))
    g_out[...] = _dot(jax.nn.sigmoid(gd).astype(BF16), gl_ref[...])

    halo = jnp.where(first, 0.0, halo_ref[...])
    halo_ref[...] = pp[tm - POOL_HALO:, :]
    ext = jnp.concatenate([halo, pp], axis=0)
    lane_group = lax.broadcasted_iota(jnp.int32, (1, POOL_DIM), 1) // POOL_GROUP_DIM
    win = jnp.zeros((1, POOL_DIM), jnp.int32)
    sel = jnp.zeros((tm, POOL_DIM), F32)
    acc = ext
    shift = 1
    for gi, w in enumerate(POOL_WINDOWS):
        while shift < w:
            acc = acc + pltpu.roll(acc, shift, axis=0)
            shift *= 2
        sel = jnp.where(lane_group == gi, acc[POOL_HALO:, :], sel)
        win = jnp.where(lane_group == gi, w, win)
    pos = t * tm + row + 1
    cnt = jnp.minimum(pos, win).astype(F32)
    dpool = sel / cnt - pp
    yb_out[...] = _dot(dpool.astype(BF16), poolw_ref[...]) * pools_ref[...]

    cols = []
    for j in range(0, MEM_DIM, LANES):
        col = None
        for hd in range(j // HEAD_DIM, (j + LANES) // HEAD_DIM):
            s = scores[hd]
            e = jnp.exp(s - jnp.max(s, axis=-1, keepdims=True))
            prob = e / jnp.sum(e, axis=-1, keepdims=True)
            vh = jnp.where((lane_head == hd)[:, j:j + LANES], vmem[:, j:j + LANES], 0.0)
            o = _dot(prob.astype(BF16), vh)
            col = o if col is None else col + o
        cols.append(col)
    yc_out[...] = jnp.concatenate(cols, axis=1)

    r = token_shift(p_r, 0, RWKV_DIM)
    k = token_shift(p_k, RWKV_DIM, 2 * RWKV_DIM)
    v = token_shift(p_v, 2 * RWKV_DIM, lo0)
    kkr = k * kk_ref[...]
    kkn = kkr * lax.rsqrt(_head_lane_sums(kkr * kkr) + 1e-12)
    kf = k * (1.0 + (a - 1.0) * ka_ref[...])
    r_out[...] = r
    v_out[...] = v
    k_out[...] = kf
    kkn_out[...] = kkn
    b_out[...] = kkn * a
    bv_out[...] = _head_lane_sums(r * kf * rk_ref[...]) * v


def _chunk_masks():
    c = CHUNK
    ri = lax.broadcasted_iota(jnp.int32, (2 * c, 2 * c), 0)
    ci = lax.broadcasted_iota(jnp.int32, (2 * c, 2 * c), 1)
    rt = ri % c
    ct = ci % c
    aa_mask = (rt > ct) | ((ri >= c) & (rt == ct))
    r64 = lax.broadcasted_iota(jnp.int32, (c, 2 * c), 0)
    c64 = lax.broadcasted_iota(jnp.int32, (c, 2 * c), 1) % c
    eye = r64 == c64
    levels = []
    blk = 2
    while blk < c:
        same_outer = (r64 // (2 * blk)) == (c64 // (2 * blk))
        diff_inner = (r64 // blk) != (c64 // blk)
        levels.append(same_outer & diff_inner & (r64 > c64))
        blk *= 2
    diag2 = ((r64 // 2) == (c64 // 2)) & (r64 > c64)
    tri_incl = lax.broadcasted_iota(jnp.int32, (c, c), 0) >= lax.broadcasted_iota(jnp.int32, (c, c), 1)
    block_diag = (ri < c) == (ci < c)
    eye2 = ri == ci
    return aa_mask, eye, diag2, levels, tri_incl, block_diag, eye2


def _rwkv_scan_kernel(r_ref, k_ref, v_ref, kk_ref, b_ref, lw_ref, y_ref, state_ref):
    c = CHUNK
    n = HEAD_DIM
    chunk_idx = pl.program_id(1)

    @pl.when(chunk_idx == 0)
    def _():
        state_ref[...] = jnp.zeros_like(state_ref)

    aa_mask, eye, diag2, levels, tri_incl, block_diag, eye2 = _chunk_masks()
    tri = jnp.where(tri_incl, 1.0, 0.0).astype(BF16)
    eye_f = jnp.where(eye, 1.0, 0.0)
    zeros_b = jnp.zeros((c, LANES), BF16)
    head_a = lax.broadcasted_iota(jnp.int32, (c, LANES), 1) < n
    only_a = jnp.where(head_a, 1.0, 0.0).astype(BF16)
    only_b = jnp.where(head_a, 0.0, 1.0).astype(BF16)
    only_a2 = jnp.concatenate([only_a, only_a], axis=0)
    only_b2 = jnp.concatenate([only_b, only_b], axis=0)
    lvl_a = [jnp.where(lvl & head_a, 1.0, 0.0).astype(BF16) for lvl in levels]
    lvl_b = [jnp.where(lvl & ~head_a, 1.0, 0.0).astype(BF16) for lvl in levels]

    def both(xb, ma=only_a, mb=only_b):
        return jnp.concatenate([xb * ma, xb * mb], axis=0)

    nb = r_ref.shape[0]
    items = [(bi, j) for bi in range(nb) for j in range(RWKV_DIM // LANES)]
    sls = [slice(j * LANES, (j + 1) * LANES) for _, j in items]
    ni = len(items)

    pre = []
    for bi in range(nb):
        lw = lw_ref[bi]
        cum = _split3_dot_left(tri, lw)
        tot = cum[c - 1:c, :]
        e_inv = jnp.exp(-cum)
        e_end = jnp.exp(tot - cum)
        r_all = r_ref[bi]
        k_all = k_ref[bi]
        b_all = b_ref[bi]
        pre.append(dict(
            rt=r_all * jnp.exp(cum), kkt=kk_ref[bi] * jnp.exp(cum - lw), bt=b_all * e_inv, kt=k_all * e_inv,
            bd=b_all * e_end, kd=k_all * e_end, e_tot=jnp.exp(tot), v=v_ref[bi]))

    rt = [pre[bi]["rt"][:, sl] for (bi, _), sl in zip(items, sls)]
    kkt = [pre[bi]["kkt"][:, sl] for (bi, _), sl in zip(items, sls)]
    vh = [pre[bi]["v"][:, sl] for (bi, _), sl in zip(items, sls)]
    aa_a, aa_b = [], []
    for i, ((bi, _), sl) in enumerate(zip(items, sls)):
        lhs = jnp.concatenate([kkt[i], rt[i]], axis=0).astype(BF16)
        rhs = jnp.concatenate([pre[bi]["bt"][:, sl], pre[bi]["kt"][:, sl]], axis=0).astype(BF16)
        aa2 = _dot_nt(jnp.concatenate([lhs * only_a2, lhs * only_b2], axis=0), rhs)
        aa_a.append(jnp.where(aa_mask, aa2[:2 * c], 0.0))
        aa_b.append(jnp.where(aa_mask, aa2[2 * c:], 0.0))
    av = []
    for i in range(ni):
        vb = vh[i].astype(BF16)
        v_rows = jnp.concatenate([zeros_b, vb * only_a, zeros_b, vb * only_b], axis=0)
        av.append(_dot(jnp.concatenate([aa_a[i], aa_b[i]], axis=1).astype(BF16), v_rows))
    a_ab = [jnp.where(head_a, aa_a[i][:c], pltpu.roll(aa_b[i][:c], n, axis=1)).astype(BF16) for i in range(ni)]
    a_rb = [jnp.where(head_a, aa_a[i][c:], pltpu.roll(aa_b[i][c:], n, axis=1)).astype(BF16) for i in range(ni)]

    tinv = [eye_f - jnp.where(diag2, a.astype(F32), 0.0) for a in a_ab]
    for li in range(len(levels)):
        tb = [t_.astype(BF16) for t_ in tinv]
        half = [_dot(tb[i], both(a_ab[i], lvl_a[li], lvl_b[li])).astype(BF16) for i in range(ni)]
        tinv = [tinv[i] - _dot(half[i], both(tb[i])) for i in range(ni)]

    w12 = [_dot(tinv[i].astype(BF16),
                jnp.concatenate([both(kkt[i].astype(BF16)), both(av[i][:c].astype(BF16))], axis=1))
           for i in range(ni)]
    w12_b = [w.astype(BF16) for w in w12]
    pq = [jnp.concatenate([rt[i], av[i][c:]], axis=1)
          - _dot(a_rb[i], jnp.concatenate([both(w12_b[i][:, :LANES]), both(w12_b[i][:, LANES:])], axis=1))
          for i in range(ni)]
    mg = []
    for i, ((bi, _), sl) in enumerate(zip(items, sls)):
        neg_bd_kd = jnp.concatenate([-pre[bi]["bd"][:, sl], pre[bi]["kd"][:, sl]], axis=0).astype(BF16)
        wv = jnp.concatenate([w12_b[i], jnp.concatenate([zeros_b, vh[i].astype(BF16)], axis=1)], axis=0)
        mg.append(_dot_tn(neg_bd_kd, wv))

    for i, ((bi, _), sl) in enumerate(zip(items, sls)):
        m_mat = jnp.where(block_diag, mg[i][:, :LANES], 0.0) + jnp.where(eye2, pre[bi]["e_tot"][:, sl], 0.0)
        pm = jnp.concatenate([pq[i][:, :LANES], m_mat], axis=0).astype(BF16)
        out = _dot(pm, state_ref[i].astype(BF16))
        y_ref[bi, :, sl] = out[:c] + pq[i][:, LANES:]
        state_ref[i] = out[c:] + jnp.where(block_diag, mg[i][:, LANES:], 0.0)


def _merge_kernel(x_ref, y_ref, bv_ref, og_ref, yb_ref, yc_ref, g_ref, lnw_ref, lnb_ref, wg_ref, bg_ref,
                  wr_ref, wp_ref, wm_ref, wo_ref, out_ref):
    x = x_ref[...]
    h = _rms_norm(x, g_ref[...]).astype(BF16)
    d = D_MODEL
    z = [_dot(h, wg_ref[:, i * d:(i + 1) * d]) for i in range(3)]
    up_b = _dot(yb_ref[...].astype(BF16), wp_ref[...])
    up_c = _dot(yc_ref[...].astype(BF16), wm_ref[...])
    inv_n = 1.0 / HEAD_DIM
    up_a = None
    half = RWKV_DIM // 2
    for c0 in (0, half):
        cs = slice(c0, c0 + half)
        y = y_ref[:, cs]
        yc = y - _head_lane_sums(y) * inv_n
        var = _head_lane_sums(yc * yc) * inv_n
        ya = (yc * lax.rsqrt(var + GN_EPS) * lnw_ref[:, cs] + lnb_ref[:, cs] + bv_ref[:, cs]) * og_ref[:, cs]
        part = _dot(ya.astype(BF16), wr_ref[cs, :])
        up_a = part if up_a is None else up_a + part
    gate = [jax.nn.sigmoid(z[i] + bg_ref[:, i * d:(i + 1) * d]) for i in range(3)]
    merged = gate[0] * up_a + gate[1] * up_b + gate[2] * up_c
    out_ref[...] = x + _dot(merged.astype(BF16), wo_ref[...])


def _conv_ffn_kernel(x_ref, g_ref, win_ref, cw_ref, cb_ref, wout_ref, gf_ref, out_ref, tail_ref, act_ref):
    t = pl.program_id(1)
    tm = x_ref.shape[0]
    x = x_ref[...]
    h = _rms_norm(x, g_ref[...]).astype(BF16)
    row = lax.broadcasted_iota(jnp.int32, (tm, 1), 0)
    first = t == 0
    n_chunks = D_FF // FF_CHUNK

    def up_proj(j):
        return (_dot(h, win_ref[:, j * FF_CHUNK:(j + 1) * FF_CHUNK]),
                _dot(h, win_ref[:, D_FF + j * FF_CHUNK:D_FF + (j + 1) * FF_CHUNK]))

    nxt = up_proj(0)
    for j in range(n_chunks):
        cs = slice(j * FF_CHUNK, (j + 1) * FF_CHUNK)
        u, gv = nxt
        if j + 1 < n_chunks:
            nxt = up_proj(j + 1)
        tail = jnp.where(first, 0.0, tail_ref[:, cs])
        tail_ref[:, cs] = u[tm - 8:, :]
        u1 = jnp.where(row == 0, tail[7:8, :], pltpu.roll(u, 1, axis=0))
        u2 = jnp.where(row == 0, tail[6:7, :], jnp.where(row == 1, tail[7:8, :], pltpu.roll(u, 2, axis=0)))
        uc = cw_ref[0:1, cs] * u2 + cw_ref[1:2, cs] * u1 + cw_ref[2:3, cs] * u + cb_ref[:, cs]
        act = 0.5 * uc * (1.0 + lax.erf(uc * (2.0 ** -0.5))) * gv
        act_ref[:, cs] = act.astype(BF16)
    out_ref[...] = _rms_norm(x + _dot(act_ref[...], wout_ref[...]), gf_ref[...])


def _const_spec(shape):
    nd = len(shape)
    return pl.BlockSpec(shape, lambda *_: (0,) * nd)


def kernel(x, mem, norm_mix_g, w_in_mix, mu_shift, w0, w_lora_b, a0, a_lora_b, g_lora_b, k_k, k_a, r_k,
           ln_x_w, ln_x_b, pool_w, pool_scale, norm_mem_g, w_mem_kv, w_up_rwkv, w_up_pool, w_up_mem,
           w_gate, b_gate, w_o, norm_ffn_g, w_ffn_in, ffn_conv_w, ffn_conv_b, w_ffn_out, norm_final_g):
    bsz, seq, d = x.shape
    n_mem = mem.shape[1]
    assert d == D_MODEL and seq % TM_PROJ == 0 and seq % CHUNK == 0 and bsz % SCAN_BATCH == 0
    assert seq % TM_FFN == 0 and (bsz * seq) % TM_MERGE == 0
    assert norm_mix_g.shape[0] == 1, "single-layer block"
    l = 0
    row = lambda a: a.reshape(1, -1).astype(F32)

    wmix = w_in_mix[l].astype(BF16)
    zpad = jnp.zeros((DECAY_LORA, RWKV_DIM), F32)
    wl_pad = jnp.concatenate([w_lora_b[l], zpad], axis=0).astype(BF16)
    al_pad = jnp.concatenate([zpad, a_lora_b[l]], axis=0).astype(BF16)
    gl = g_lora_b[l].astype(BF16)
    pool_bd = jax.scipy.linalg.block_diag(*[pool_w[l, i] for i in range(len(POOL_WINDOWS))]).astype(BF16)

    kv = pl.pallas_call(
        _mem_kv_kernel,
        out_shape=jax.ShapeDtypeStruct((bsz, n_mem, 2 * MEM_DIM), BF16),
        grid=(bsz,),
        in_specs=[pl.BlockSpec((None, n_mem, d), lambda b: (b, 0, 0)),
                  _const_spec((1, d)), _const_spec((d, 2 * MEM_DIM))],
        out_specs=pl.BlockSpec((None, n_mem, 2 * MEM_DIM), lambda b: (b, 0, 0)),
        compiler_params=pltpu.CompilerParams(dimension_semantics=("arbitrary",)),
        name="mem_kv",
    )(mem, row(norm_mem_g[l]), w_mem_kv[l].astype(BF16))

    tok = lambda w: pl.BlockSpec((None, TM_PROJ, w), lambda b, t: (b, t, 0))
    f32_out = lambda w: jax.ShapeDtypeStruct((bsz, seq, w), F32)
    outs = pl.pallas_call(
        _mix_proj_kernel,
        out_shape=[f32_out(RWKV_DIM)] * 8 + [f32_out(POOL_DIM), f32_out(MEM_DIM)],
        grid=(bsz, seq // TM_PROJ),
        in_specs=[tok(d), _const_spec((1, d)), _const_spec((d, MIX_IN)), _const_spec((1, RWKV_IN)),
                  _const_spec((1, RWKV_DIM)), _const_spec((DECAY_LORA + ICLR_LORA, RWKV_DIM)),
                  _const_spec((1, RWKV_DIM)), _const_spec((DECAY_LORA + ICLR_LORA, RWKV_DIM)),
                  _const_spec((GATE_LORA, RWKV_DIM)), _const_spec((1, RWKV_DIM)), _const_spec((1, RWKV_DIM)),
                  _const_spec((1, RWKV_DIM)), _const_spec((POOL_DIM, POOL_DIM)),
                  _const_spec((1, POOL_DIM)),
                  pl.BlockSpec((None, n_mem, 2 * MEM_DIM), lambda b, t: (b, 0, 0))],
        out_specs=[tok(RWKV_DIM)] * 8 + [tok(POOL_DIM), tok(MEM_DIM)],
        scratch_shapes=[pltpu.VMEM((1, RWKV_IN), F32), pltpu.VMEM((POOL_HALO, POOL_DIM), F32)],
        compiler_params=pltpu.CompilerParams(dimension_semantics=("arbitrary", "arbitrary"),
                                             vmem_limit_bytes=V7X_VMEM_LIMIT_BYTES),
        name="mix_proj",
    )(x, row(norm_mix_g[l]), wmix, row(mu_shift[l]), row(w0[l]), wl_pad, row(a0[l]), al_pad, gl,
      row(k_k[l]), row(k_a[l]), row(r_k[l]), pool_bd, row(pool_scale[l]), kv)
    r_s, k_s, v_s, kk_s, b_s, lw_s, g_s, bv_s, y_b, y_c = outs

    sb = SCAN_BATCH
    blk = pl.BlockSpec((sb, CHUNK, RWKV_DIM), lambda b, c: (b, c, 0))
    y_a = pl.pallas_call(
        _rwkv_scan_kernel,
        out_shape=f32_out(RWKV_DIM),
        grid=(bsz // sb, seq // CHUNK),
        in_specs=[blk] * 6,
        out_specs=blk,
        scratch_shapes=[pltpu.VMEM((sb * RWKV_DIM // LANES, LANES, LANES), F32)],
        compiler_params=pltpu.CompilerParams(dimension_semantics=("arbitrary", "arbitrary")),
        name="rwkv_scan",
    )(r_s, k_s, v_s, kk_s, b_s, lw_s)

    n_tok = bsz * seq
    flat = lambda a: a.reshape(n_tok, a.shape[-1])
    tokm = lambda w: pl.BlockSpec((TM_MERGE, w), lambda i: (i, 0))
    wspec = lambda shape: pl.BlockSpec(shape, lambda i: (0, 0), pipeline_mode=pl.Buffered(1))
    x1 = pl.pallas_call(
        _merge_kernel,
        out_shape=jax.ShapeDtypeStruct((n_tok, d), F32),
        grid=(n_tok // TM_MERGE,),
        in_specs=[tokm(d), tokm(RWKV_DIM), tokm(RWKV_DIM), tokm(RWKV_DIM), tokm(POOL_DIM), tokm(MEM_DIM),
                  _const_spec((1, d)), _const_spec((1, RWKV_DIM)), _const_spec((1, RWKV_DIM)),
                  wspec((d, 3 * d)), _const_spec((1, 3 * d)), wspec((RWKV_DIM, d)),
                  wspec((POOL_DIM, d)), wspec((MEM_DIM, d)), wspec((d, d))],
        out_specs=tokm(d),
        compiler_params=pltpu.CompilerParams(dimension_semantics=("arbitrary",),
                                             vmem_limit_bytes=V7X_VMEM_LIMIT_BYTES),
        name="merge",
    )(flat(x), flat(y_a), flat(bv_s), flat(g_s), flat(y_b), flat(y_c), row(norm_mix_g[l]),
      row(ln_x_w[l]), row(ln_x_b[l]), w_gate[l].astype(BF16),
      row(b_gate[l]), w_up_rwkv[l].astype(BF16), w_up_pool[l].astype(BF16), w_up_mem[l].astype(BF16),
      w_o[l].astype(BF16))

    tokf = pl.BlockSpec((None, TM_FFN, d), lambda b, t: (b, t, 0))
    single = pl.Buffered(1)
    out = pl.pallas_call(
        _conv_ffn_kernel,
        out_shape=jax.ShapeDtypeStruct((bsz, seq, d), F32),
        grid=(bsz, seq // TM_FFN),
        in_specs=[tokf, _const_spec((1, d)),
                  pl.BlockSpec((d, 2 * D_FF), lambda b, t: (0, 0), pipeline_mode=single),
                  _const_spec((3, D_FF)), _const_spec((1, D_FF)),
                  pl.BlockSpec((D_FF, d), lambda b, t: (0, 0), pipeline_mode=single),
                  _const_spec((1, d))],
        out_specs=tokf,
        scratch_shapes=[pltpu.VMEM((8, D_FF), F32), pltpu.VMEM((TM_FFN, D_FF), BF16)],
        compiler_params=pltpu.CompilerParams(dimension_semantics=("arbitrary", "arbitrary"),
                                             vmem_limit_bytes=V7X_VMEM_LIMIT_BYTES),
        name="conv_ffn",
    )(x1.reshape(bsz, seq, d), row(norm_ffn_g[l]), w_ffn_in[l].astype(BF16), ffn_conv_w[l].astype(F32),
      row(ffn_conv_b[l]), w_ffn_out[l].astype(BF16), row(norm_final_g))
    return out
```

```python
import functools
import math

import jax
import jax.numpy as jnp
from jax import lax
from jax.experimental import pallas as pl
from jax.experimental.pallas import tpu as pltpu

F32 = jnp.float32
BF16 = jnp.bfloat16

D_MODEL = 1024
HEAD_DIM = 64
RWKV_HEADS = 8
RWKV_DIM = RWKV_HEADS * HEAD_DIM
DECAY_LORA = 64
ICLR_LORA = 64
GATE_LORA = 128
RWKV_IN = 3 * RWKV_DIM + DECAY_LORA + ICLR_LORA + GATE_LORA
POOL_WINDOWS = (2, 4, 8, 16)
POOL_GROUP_DIM = 64
POOL_DIM = len(POOL_WINDOWS) * POOL_GROUP_DIM
POOL_HALO = 16
MEM_HEADS = 4
MEM_DIM = MEM_HEADS * HEAD_DIM
MIX_IN = RWKV_IN + POOL_DIM + MEM_DIM
D_FF = 2816
NORM_EPS = 1e-6
GN_EPS = 64e-5
CHUNK = 64
FF_CHUNK = 256
V7X_VMEM_LIMIT_BYTES = 56 * 1024 * 1024
LANES = 128

TM_PROJ = 512
TM_MERGE = 512
TM_FFN = 1024
SCAN_BATCH = 8


def _dot(a, b):
    return jnp.dot(a, b, preferred_element_type=F32)


def _dot_nt(a, b):
    return lax.dot_general(a, b, (((1,), (1,)), ((), ())), preferred_element_type=F32)


def _dot_tn(a, b):
    return lax.dot_general(a, b, (((0,), (0,)), ((), ())), preferred_element_type=F32)


def _rms_norm(x, g):
    return x * lax.rsqrt(jnp.mean(x * x, axis=-1, keepdims=True) + NORM_EPS) * g


def _split3_dot_left(w_bf16, x):
    hi = x.astype(BF16)
    r1 = x - hi.astype(F32)
    mid = r1.astype(BF16)
    lo = (r1 - mid.astype(F32)).astype(BF16)
    return _dot(w_bf16, hi) + _dot(w_bf16, mid) + _dot(w_bf16, lo)


def _head_lane_sums(x):
    first_head = lax.broadcasted_iota(jnp.int32, (1, LANES), 1) < HEAD_DIM
    cols = []
    for j in range(0, x.shape[1], LANES):
        col = x[:, j:j + LANES]
        sa = jnp.sum(jnp.where(first_head, col, 0.0), axis=-1, keepdims=True)
        sb = jnp.sum(jnp.where(first_head, 0.0, col), axis=-1, keepdims=True)
        cols.append(jnp.where(first_head, sa, sb))
    return cols[0] if len(cols) == 1 else jnp.concatenate(cols, axis=1)


def _mem_kv_kernel(mem_ref, g_ref, w_ref, kv_ref):
    m = _rms_norm(mem_ref[...], g_ref[...])
    kv_ref[...] = _dot(m.astype(BF16), w_ref[...]).astype(BF16)


def _mix_proj_kernel(x_ref, g_ref, wmix_ref, mu_ref, w0_ref, wl_ref, a0_ref, al_ref, gl_ref,
                     kk_ref, ka_ref, rk_ref, poolw_ref, pools_ref, kv_ref,
                     r_out, k_out, v_out, kkn_out, b_out, lw_out, g_out, bv_out, yb_out, yc_out,
                     prev_ref, halo_ref):
    t = pl.program_id(1)
    tm = x_ref.shape[0]
    first = t == 0

    h = _rms_norm(x_ref[...], g_ref[...]).astype(BF16)
    row = lax.broadcasted_iota(jnp.int32, (tm, 1), 0)
    lo0 = 3 * RWKV_DIM

    def proj(c0, c1):
        return _dot(h, wmix_ref[:, c0:c1])

    def token_shift(ps, c0, c1):
        carry = jnp.where(first, 0.0, prev_ref[:, c0:c1])
        prev = jnp.where(row == 0, carry, pltpu.roll(ps, 1, axis=0))
        prev_ref[:, c0:c1] = ps[tm - 1:tm, :]
        return ps + (prev - ps) * mu_ref[:, c0:c1]

    p_pq = proj(RWKV_IN, MIX_IN)
    p_lo = proj(lo0, RWKV_IN)
    p_r = proj(0, RWKV_DIM)
    pp = p_pq[:, :POOL_DIM]
    q = p_pq[:, POOL_DIM:]

    kmem = kv_ref[:, :MEM_DIM]
    vmem = kv_ref[:, MEM_DIM:]
    lane_head = lax.broadcasted_iota(jnp.int32, (1, MEM_DIM), 1) // HEAD_DIM
    scores = [_dot_nt(jnp.where(lane_head == hd, q, 0.0).astype(BF16), kmem) * (HEAD_DIM ** -0.5)
              for hd in range(MEM_HEADS)]

    p_k = proj(RWKV_DIM, 2 * RWKV_DIM)
    p_v = proj(2 * RWKV_DIM, lo0)

    lo = token_shift(p_lo, lo0, RWKV_IN)
    wa = lo[:, :DECAY_LORA + ICLR_LORA]
    gd = lo[:, DECAY_LORA + ICLR_LORA:]
    z = w0_ref[...] + _dot(jnp.tanh(wa).astype(BF16), wl_ref[...])
    lw_out[...] = (-math.exp(-0.5)) * jax.nn.sigmoid(z)
    a = jax.nn.sigmoid(a0_ref[...] + _dot(wa.astype(BF16), al_ref[...]))
    g_out[...] = _dot(jax.nn.sigmoid(gd).astype(BF16), gl_ref[...]).astype(BF16)

    halo = jnp.where(first, 0.0, halo_ref[...])
    halo_ref[...] = pp[tm - POOL_HALO:, :]
    ext = jnp.concatenate([halo, pp], axis=0)
    lane_group = lax.broadcasted_iota(jnp.int32, (1, POOL_DIM), 1) // POOL_GROUP_DIM
    win = jnp.zeros((1, POOL_DIM), jnp.int32)
    sel = jnp.zeros((tm, POOL_DIM), F32)
    acc = ext
    shift = 1
    for gi, w in enumerate(POOL_WINDOWS):
        while shift < w:
            acc = acc + pltpu.roll(acc, shift, axis=0)
            shift *= 2
        sel = jnp.where(lane_group == gi, acc[POOL_HALO:, :], sel)
        win = jnp.where(lane_group == gi, w, win)
    pos = t * tm + row + 1
    cnt = jnp.minimum(pos, win).astype(F32)
    dpool = sel / cnt - pp
    yb_out[...] = _dot(dpool.astype(BF16), poolw_ref[...]) * pools_ref[...]

    cols = []
    for j in range(0, MEM_DIM, LANES):
        col = None
        for hd in range(j // HEAD_DIM, (j + LANES) // HEAD_DIM):
            s = scores[hd]
            e = jnp.exp(s - jnp.max(s, axis=-1, keepdims=True))
            prob = e / jnp.sum(e, axis=-1, keepdims=True)
            vh = jnp.where((lane_head == hd)[:, j:j + LANES], vmem[:, j:j + LANES], 0.0)
            o = _dot(prob.astype(BF16), vh)
            col = o if col is None else col + o
        cols.append(col)
    yc_out[...] = jnp.concatenate(cols, axis=1)

    r = token_shift(p_r, 0, RWKV_DIM)
    k = token_shift(p_k, RWKV_DIM, 2 * RWKV_DIM)
    v = token_shift(p_v, 2 * RWKV_DIM, lo0)
    kkr = k * kk_ref[...]
    kkn = kkr * lax.rsqrt(_head_lane_sums(kkr * kkr) + 1e-12)
    kf = k * (1.0 + (a - 1.0) * ka_ref[...])
    r_out[...] = r
    v_out[...] = v.astype(BF16)
    k_out[...] = kf
    kkn_out[...] = kkn
    b_out[...] = kkn * a
    bv_out[...] = (_head_lane_sums(r * kf * rk_ref[...]) * v).astype(BF16)


def _chunk_masks():
    c = CHUNK
    ri = lax.broadcasted_iota(jnp.int32, (2 * c, 2 * c), 0)
    ci = lax.broadcasted_iota(jnp.int32, (2 * c, 2 * c), 1)
    rt = ri % c
    ct = ci % c
    aa_mask = (rt > ct) | ((ri >= c) & (rt == ct))
    r64 = lax.broadcasted_iota(jnp.int32, (c, 2 * c), 0)
    c64 = lax.broadcasted_iota(jnp.int32, (c, 2 * c), 1) % c
    eye = r64 == c64
    levels = []
    blk = 2
    while blk < c:
        same_outer = (r64 // (2 * blk)) == (c64 // (2 * blk))
        diff_inner = (r64 // blk) != (c64 // blk)
        levels.append(same_outer & diff_inner & (r64 > c64))
        blk *= 2
    diag2 = ((r64 // 2) == (c64 // 2)) & (r64 > c64)
    tri_incl = lax.broadcasted_iota(jnp.int32, (c, c), 0) >= lax.broadcasted_iota(jnp.int32, (c, c), 1)
    block_diag = (ri < c) == (ci < c)
    eye2 = ri == ci
    return aa_mask, eye, diag2, levels, tri_incl, block_diag, eye2


def _rwkv_scan_kernel(r_ref, k_ref, v_ref, kk_ref, b_ref, lw_ref, y_ref, state_ref):
    c = CHUNK
    n = HEAD_DIM
    chunk_idx = pl.program_id(1)

    @pl.when(chunk_idx == 0)
    def _():
        state_ref[...] = jnp.zeros_like(state_ref)

    aa_mask, eye, diag2, levels, tri_incl, block_diag, eye2 = _chunk_masks()
    tri = jnp.where(tri_incl, 1.0, 0.0).astype(BF16)
    eye_f = jnp.where(eye, 1.0, 0.0)
    zeros_b = jnp.zeros((c, LANES), BF16)
    head_a = lax.broadcasted_iota(jnp.int32, (c, LANES), 1) < n
    only_a = jnp.where(head_a, 1.0, 0.0).astype(BF16)
    only_b = jnp.where(head_a, 0.0, 1.0).astype(BF16)
    only_a2 = jnp.concatenate([only_a, only_a], axis=0)
    only_b2 = jnp.concatenate([only_b, only_b], axis=0)
    lvl_a = [jnp.where(lvl & head_a, 1.0, 0.0).astype(BF16) for lvl in levels]
    lvl_b = [jnp.where(lvl & ~head_a, 1.0, 0.0).astype(BF16) for lvl in levels]

    def both(xb, ma=only_a, mb=only_b):
        return jnp.concatenate([xb * ma, xb * mb], axis=0)

    nb = r_ref.shape[0]
    items = [(bi, j) for bi in range(nb) for j in range(RWKV_DIM // LANES)]
    sls = [slice(j * LANES, (j + 1) * LANES) for _, j in items]
    ni = len(items)

    pre = []
    for bi in range(nb):
        lw = lw_ref[bi]
        cum = _split3_dot_left(tri, lw)
        tot = cum[c - 1:c, :]
        e_inv = jnp.exp(-cum)
        e_end = jnp.exp(tot - cum)
        r_all = r_ref[bi]
        k_all = k_ref[bi]
        b_all = b_ref[bi]
        pre.append(dict(
            rt=r_all * jnp.exp(cum), kkt=kk_ref[bi] * jnp.exp(cum - lw), bt=b_all * e_inv, kt=k_all * e_inv,
            bd=b_all * e_end, kd=k_all * e_end, e_tot=jnp.exp(tot), v=v_ref[bi]))

    rt = [pre[bi]["rt"][:, sl] for (bi, _), sl in zip(items, sls)]
    kkt = [pre[bi]["kkt"][:, sl] for (bi, _), sl in zip(items, sls)]
    vh = [pre[bi]["v"][:, sl] for (bi, _), sl in zip(items, sls)]
    aa_a, aa_b = [], []
    for i, ((bi, _), sl) in enumerate(zip(items, sls)):
        lhs = jnp.concatenate([kkt[i], rt[i]], axis=0).astype(BF16)
        rhs = jnp.concatenate([pre[bi]["bt"][:, sl], pre[bi]["kt"][:, sl]], axis=0).astype(BF16)
        aa2 = _dot_nt(jnp.concatenate([lhs * only_a2, lhs * only_b2], axis=0), rhs)
        aa_a.append(jnp.where(aa_mask, aa2[:2 * c], 0.0))
        aa_b.append(jnp.where(aa_mask, aa2[2 * c:], 0.0))
    av = []
    for i in range(ni):
        vb = vh[i].astype(BF16)
        v_rows = jnp.concatenate([zeros_b, vb * only_a, zeros_b, vb * only_b], axis=0)
        av.append(_dot(jnp.concatenate([aa_a[i], aa_b[i]], axis=1).astype(BF16), v_rows))
    a_ab = [jnp.where(head_a, aa_a[i][:c], pltpu.roll(aa_b[i][:c], n, axis=1)).astype(BF16) for i in range(ni)]
    a_rb = [jnp.where(head_a, aa_a[i][c:], pltpu.roll(aa_b[i][c:], n, axis=1)).astype(BF16) for i in range(ni)]

    tinv = [eye_f - jnp.where(diag2, a.astype(F32), 0.0) for a in a_ab]
    for li in range(len(levels)):
        tb = [t_.astype(BF16) for t_ in tinv]
        half = [_dot(tb[i], both(a_ab[i], lvl_a[li], lvl_b[li])).astype(BF16) for i in range(ni)]
        tinv = [tinv[i] - _dot(half[i], both(tb[i])) for i in range(ni)]

    w12 = [_dot(tinv[i].astype(BF16),
                jnp.concatenate([both(kkt[i].astype(BF16)), both(av[i][:c].astype(BF16))], axis=1))
           for i in range(ni)]
    w12_b = [w.astype(BF16) for w in w12]
    pq = [jnp.concatenate([rt[i], av[i][c:]], axis=1)
          - _dot(a_rb[i], jnp.concatenate([both(w12_b[i][:, :LANES]), both(w12_b[i][:, LANES:])], axis=1))
          for i in range(ni)]
    mg = []
    for i, ((bi, _), sl) in enumerate(zip(items, sls)):
        neg_bd_kd = jnp.concatenate([-pre[bi]["bd"][:, sl], pre[bi]["kd"][:, sl]], axis=0).astype(BF16)
        wv = jnp.concatenate([w12_b[i], jnp.concatenate([zeros_b, vh[i].astype(BF16)], axis=1)], axis=0)
        mg.append(_dot_tn(neg_bd_kd, wv))

    for i, ((bi, _), sl) in enumerate(zip(items, sls)):
        m_mat = jnp.where(block_diag, mg[i][:, :LANES], 0.0) + jnp.where(eye2, pre[bi]["e_tot"][:, sl], 0.0)
        pm = jnp.concatenate([pq[i][:, :LANES], m_mat], axis=0).astype(BF16)
        out = _dot(pm, state_ref[i].astype(BF16))
        y_ref[bi, :, sl] = out[:c] + pq[i][:, LANES:]
        state_ref[i] = out[c:] + jnp.where(block_diag, mg[i][:, LANES:], 0.0)


def _merge_kernel(x_ref, y_ref, bv_ref, og_ref, yb_ref, yc_ref, g_ref, lnw_ref, lnb_ref, wg_ref, bg_ref,
                  wr_ref, wp_ref, wm_ref, wo_ref, out_ref):
    x = x_ref[...]
    h = _rms_norm(x, g_ref[...]).astype(BF16)
    d = D_MODEL
    z = [_dot(h, wg_ref[:, i * d:(i + 1) * d]) for i in range(3)]
    up_b = _dot(yb_ref[...].astype(BF16), wp_ref[...])
    up_c = _dot(yc_ref[...].astype(BF16), wm_ref[...])
    inv_n = 1.0 / HEAD_DIM
    up_a = None
    half = RWKV_DIM // 2
    for c0 in (0, half):
        cs = slice(c0, c0 + half)
        y = y_ref[:, cs]
        yc = y - _head_lane_sums(y) * inv_n
        var = _head_lane_sums(yc * yc) * inv_n
        ya = (yc * lax.rsqrt(var + GN_EPS) * lnw_ref[:, cs] + lnb_ref[:, cs] + bv_ref[:, cs]) * og_ref[:, cs]
        part = _dot(ya.astype(BF16), wr_ref[cs, :])
        up_a = part if up_a is None else up_a + part
    gate = [jax.nn.sigmoid(z[i] + bg_ref[:, i * d:(i + 1) * d]) for i in range(3)]
    merged = gate[0] * up_a + gate[1] * up_b + gate[2] * up_c
    out_ref[...] = x + _dot(merged.astype(BF16), wo_ref[...])


def _conv_ffn_kernel(x_ref, g_ref, win_ref, cw_ref, cb_ref, wout_ref, gf_ref, out_ref, tail_ref, act_ref):
    t = pl.program_id(1)
    tm = x_ref.shape[0]
    hm = tm // 2
    halves = (slice(0, hm), slice(hm, tm))
    xs = [x_ref[rs, :] for rs in halves]
    hs = [_rms_norm(x, g_ref[...]).astype(BF16) for x in xs]
    row = lax.broadcasted_iota(jnp.int32, (hm, 1), 0)
    first = t == 0
    n_chunks = D_FF // FF_CHUNK

    def up_proj(j):
        wu = win_ref[:, j * FF_CHUNK:(j + 1) * FF_CHUNK]
        wg = win_ref[:, D_FF + j * FF_CHUNK:D_FF + (j + 1) * FF_CHUNK]
        return [(_dot(h, wu), _dot(h, wg)) for h in hs]

    nxt = up_proj(0)
    for j in range(n_chunks):
        cs = slice(j * FF_CHUNK, (j + 1) * FF_CHUNK)
        cur = nxt
        if j + 1 < n_chunks:
            nxt = up_proj(j + 1)
        tail = jnp.where(first, 0.0, tail_ref[:, cs])
        tail_ref[:, cs] = cur[1][0][hm - 8:, :]
        for (u, gv), rs in zip(cur, halves):
            u1 = jnp.where(row == 0, tail[7:8, :], pltpu.roll(u, 1, axis=0))
            u2 = jnp.where(row == 0, tail[6:7, :], jnp.where(row == 1, tail[7:8, :], pltpu.roll(u, 2, axis=0)))
            uc = cw_ref[0:1, cs] * u2 + cw_ref[1:2, cs] * u1 + cw_ref[2:3, cs] * u + cb_ref[:, cs]
            act_ref[rs, cs] = (0.5 * uc * (1.0 + lax.erf(uc * (2.0 ** -0.5))) * gv).astype(BF16)
            tail = u[hm - 8:, :]
    for x, rs in zip(xs, halves):
        out_ref[rs, :] = _rms_norm(x + _dot(act_ref[rs, :], wout_ref[...]), gf_ref[...])


def _const_spec(shape):
    nd = len(shape)
    return pl.BlockSpec(shape, lambda *_: (0,) * nd)


def kernel(x, mem, norm_mix_g, w_in_mix, mu_shift, w0, w_lora_b, a0, a_lora_b, g_lora_b, k_k, k_a, r_k,
           ln_x_w, ln_x_b, pool_w, pool_scale, norm_mem_g, w_mem_kv, w_up_rwkv, w_up_pool, w_up_mem,
           w_gate, b_gate, w_o, norm_ffn_g, w_ffn_in, ffn_conv_w, ffn_conv_b, w_ffn_out, norm_final_g):
    bsz, seq, d = x.shape
    n_mem = mem.shape[1]
    assert d == D_MODEL and seq % TM_PROJ == 0 and seq % CHUNK == 0 and bsz % SCAN_BATCH == 0
    assert seq % TM_FFN == 0 and (bsz * seq) % TM_MERGE == 0
    assert norm_mix_g.shape[0] == 1, "single-layer block"
    l = 0
    row = lambda a: a.reshape(1, -1).astype(F32)

    wmix = w_in_mix[l].astype(BF16)
    zpad = jnp.zeros((DECAY_LORA, RWKV_DIM), F32)
    wl_pad = jnp.concatenate([w_lora_b[l], zpad], axis=0).astype(BF16)
    al_pad = jnp.concatenate([zpad, a_lora_b[l]], axis=0).astype(BF16)
    gl = g_lora_b[l].astype(BF16)
    pool_bd = jax.scipy.linalg.block_diag(*[pool_w[l, i] for i in range(len(POOL_WINDOWS))]).astype(BF16)

    kv = pl.pallas_call(
        _mem_kv_kernel,
        out_shape=jax.ShapeDtypeStruct((bsz, n_mem, 2 * MEM_DIM), BF16),
        grid=(bsz,),
        in_specs=[pl.BlockSpec((None, n_mem, d), lambda b: (b, 0, 0)),
                  _const_spec((1, d)), _const_spec((d, 2 * MEM_DIM))],
        out_specs=pl.BlockSpec((None, n_mem, 2 * MEM_DIM), lambda b: (b, 0, 0)),
        compiler_params=pltpu.CompilerParams(dimension_semantics=("arbitrary",)),
        name="mem_kv",
    )(mem, row(norm_mem_g[l]), w_mem_kv[l].astype(BF16))

    tok = lambda w: pl.BlockSpec((None, TM_PROJ, w), lambda b, t: (b, t, 0))
    f32_out = lambda w: jax.ShapeDtypeStruct((bsz, seq, w), F32)
    bf16_out = lambda w: jax.ShapeDtypeStruct((bsz, seq, w), BF16)
    outs = pl.pallas_call(
        _mix_proj_kernel,
        out_shape=([f32_out(RWKV_DIM)] * 2 + [bf16_out(RWKV_DIM)] + [f32_out(RWKV_DIM)] * 3
                   + [bf16_out(RWKV_DIM)] * 2 + [f32_out(POOL_DIM), f32_out(MEM_DIM)]),
        grid=(bsz, seq // TM_PROJ),
        in_specs=[tok(d), _const_spec((1, d)), _const_spec((d, MIX_IN)), _const_spec((1, RWKV_IN)),
                  _const_spec((1, RWKV_DIM)), _const_spec((DECAY_LORA + ICLR_LORA, RWKV_DIM)),
                  _const_spec((1, RWKV_DIM)), _const_spec((DECAY_LORA + ICLR_LORA, RWKV_DIM)),
                  _const_spec((GATE_LORA, RWKV_DIM)), _const_spec((1, RWKV_DIM)), _const_spec((1, RWKV_DIM)),
                  _const_spec((1, RWKV_DIM)), _const_spec((POOL_DIM, POOL_DIM)),
                  _const_spec((1, POOL_DIM)),
                  pl.BlockSpec((None, n_mem, 2 * MEM_DIM), lambda b, t: (b, 0, 0))],
        out_specs=[tok(RWKV_DIM)] * 8 + [tok(POOL_DIM), tok(MEM_DIM)],
        scratch_shapes=[pltpu.VMEM((1, RWKV_IN), F32), pltpu.VMEM((POOL_HALO, POOL_DIM), F32)],
        compiler_params=pltpu.CompilerParams(dimension_semantics=("arbitrary", "arbitrary"),
                                             vmem_limit_bytes=V7X_VMEM_LIMIT_BYTES),
        name="mix_proj",
    )(x, row(norm_mix_g[l]), wmix, row(mu_shift[l]), row(w0[l]), wl_pad, row(a0[l]), al_pad, gl,
      row(k_k[l]), row(k_a[l]), row(r_k[l]), pool_bd, row(pool_scale[l]), kv)
    r_s, k_s, v_s, kk_s, b_s, lw_s, g_s, bv_s, y_b, y_c = outs

    sb = SCAN_BATCH
    blk = pl.BlockSpec((sb, CHUNK, RWKV_DIM), lambda b, c: (b, c, 0))
    y_a = pl.pallas_call(
        _rwkv_scan_kernel,
        out_shape=f32_out(RWKV_DIM),
        grid=(bsz // sb, seq // CHUNK),
        in_specs=[blk] * 6,
        out_specs=blk,
        scratch_shapes=[pltpu.VMEM((sb * RWKV_DIM // LANES, LANES, LANES), F32)],
        compiler_params=pltpu.CompilerParams(dimension_semantics=("arbitrary", "arbitrary")),
        name="rwkv_scan",
    )(r_s, k_s, v_s, kk_s, b_s, lw_s)

    n_tok = bsz * seq
    flat = lambda a: a.reshape(n_tok, a.shape[-1])
    tokm = lambda w: pl.BlockSpec((TM_MERGE, w), lambda i: (i, 0))
    wspec = lambda shape: pl.BlockSpec(shape, lambda i: (0, 0), pipeline_mode=pl.Buffered(1))
    x1 = pl.pallas_call(
        _merge_kernel,
        out_shape=jax.ShapeDtypeStruct((n_tok, d), F32),
        grid=(n_tok // TM_MERGE,),
        in_specs=[tokm(d), tokm(RWKV_DIM), tokm(RWKV_DIM), tokm(RWKV_DIM), tokm(POOL_DIM), tokm(MEM_DIM),
                  _const_spec((1, d)), _const_spec((1, RWKV_DIM)), _const_spec((1, RWKV_DIM)),
                  wspec((d, 3 * d)), _const_spec((1, 3 * d)), wspec((RWKV_DIM, d)),
                  wspec((POOL_DIM, d)), wspec((MEM_DIM, d)), wspec((d, d))],
        out_specs=tokm(d),
        compiler_params=pltpu.CompilerParams(dimension_semantics=("arbitrary",),
                                             vmem_limit_bytes=V7X_VMEM_LIMIT_BYTES),
        name="merge",
    )(flat(x), flat(y_a), flat(bv_s), flat(g_s), flat(y_b), flat(y_c), row(norm_mix_g[l]),
      row(ln_x_w[l]), row(ln_x_b[l]), w_gate[l].astype(BF16),
      row(b_gate[l]), w_up_rwkv[l].astype(BF16), w_up_pool[l].astype(BF16), w_up_mem[l].astype(BF16),
      w_o[l].astype(BF16))

    tokf = pl.BlockSpec((None, TM_FFN, d), lambda b, t: (b, t, 0))
    single = pl.Buffered(1)
    out = pl.pallas_call(
        _conv_ffn_kernel,
        out_shape=jax.ShapeDtypeStruct((bsz, seq, d), F32),
        grid=(bsz, seq // TM_FFN),
        in_specs=[tokf, _const_spec((1, d)),
                  pl.BlockSpec((d, 2 * D_FF), lambda b, t: (0, 0), pipeline_mode=single),
                  _const_spec((3, D_FF)), _const_spec((1, D_FF)),
                  pl.BlockSpec((D_FF, d), lambda b, t: (0, 0), pipeline_mode=single),
                  _const_spec((1, d))],
        out_specs=tokf,
        scratch_shapes=[pltpu.VMEM((8, D_FF), F32), pltpu.VMEM((TM_FFN, D_FF), BF16)],
        compiler_params=pltpu.CompilerParams(dimension_semantics=("arbitrary", "arbitrary"),
                                             vmem_limit_bytes=V7X_VMEM_LIMIT_BYTES),
        name="conv_ffn",
    )(x1.reshape(bsz, seq, d), row(norm_ffn_g[l]), w_ffn_in[l].astype(BF16), ffn_conv_w[l].astype(F32),
      row(ffn_conv_b[l]), w_ffn_out[l].astype(BF16), row(norm_final_g))
    return out
```

```python
import functools
import math

import jax
import jax.numpy as jnp
from jax import lax
from jax.experimental import pallas as pl
from jax.experimental.pallas import tpu as pltpu

F32 = jnp.float32
BF16 = jnp.bfloat16

D_MODEL = 1024
HEAD_DIM = 64
RWKV_HEADS = 8
RWKV_DIM = RWKV_HEADS * HEAD_DIM
DECAY_LORA = 64
ICLR_LORA = 64
GATE_LORA = 128
RWKV_IN = 3 * RWKV_DIM + DECAY_LORA + ICLR_LORA + GATE_LORA
POOL_WINDOWS = (2, 4, 8, 16)
POOL_GROUP_DIM = 64
POOL_DIM = len(POOL_WINDOWS) * POOL_GROUP_DIM
POOL_HALO = 16
MEM_HEADS = 4
MEM_DIM = MEM_HEADS * HEAD_DIM
MIX_IN = RWKV_IN + POOL_DIM + MEM_DIM
D_FF = 2816
NORM_EPS = 1e-6
GN_EPS = 64e-5
CHUNK = 64
FF_CHUNK = 256
V7X_VMEM_LIMIT_BYTES = 56 * 1024 * 1024
LANES = 128
PACK_R, PACK_K, PACK_V, PACK_KK, PACK_B, PACK_LW, PACK_G, PACK_BV = range(8)
PACK_SLOTS = 8

TM_PROJ = 512
TM_MERGE = 512
TM_FFN = 1024
SCAN_BATCH = 8


def _dot(a, b):
    return jnp.dot(a, b, preferred_element_type=F32)


def _dot_nt(a, b):
    return lax.dot_general(a, b, (((1,), (1,)), ((), ())), preferred_element_type=F32)


def _dot_tn(a, b):
    return lax.dot_general(a, b, (((0,), (0,)), ((), ())), preferred_element_type=F32)


def _rms_norm(x, g):
    return x * lax.rsqrt(jnp.mean(x * x, axis=-1, keepdims=True) + NORM_EPS) * g


def _split3_dot_left(w_bf16, x):
    hi = x.astype(BF16)
    r1 = x - hi.astype(F32)
    mid = r1.astype(BF16)
    lo = (r1 - mid.astype(F32)).astype(BF16)
    return _dot(w_bf16, hi) + _dot(w_bf16, mid) + _dot(w_bf16, lo)


def _head_lane_sums(x):
    first_head = lax.broadcasted_iota(jnp.int32, (1, LANES), 1) < HEAD_DIM
    cols = []
    for j in range(0, x.shape[1], LANES):
        col = x[:, j:j + LANES]
        sa = jnp.sum(jnp.where(first_head, col, 0.0), axis=-1, keepdims=True)
        sb = jnp.sum(jnp.where(first_head, 0.0, col), axis=-1, keepdims=True)
        cols.append(jnp.where(first_head, sa, sb))
    return cols[0] if len(cols) == 1 else jnp.concatenate(cols, axis=1)


def _mem_kv_kernel(mem_ref, g_ref, w_ref, kv_ref):
    m = _rms_norm(mem_ref[...], g_ref[...])
    kv_ref[...] = _dot(m.astype(BF16), w_ref[...]).astype(BF16)


def _mix_proj_kernel(x_ref, g_ref, wmix_ref, mu_ref, w0_ref, wl_ref, a0_ref, al_ref, gl_ref,
                     kk_ref, ka_ref, rk_ref, poolw_ref, pools_ref, kv_ref,
                     pack_out, bc_out,
                     prev_ref, halo_ref):
    t = pl.program_id(1)
    tm = x_ref.shape[0]
    first = t == 0

    def put(slot, val):
        pack_out[:, slot * RWKV_DIM:(slot + 1) * RWKV_DIM] = val

    h = _rms_norm(x_ref[...], g_ref[...]).astype(BF16)
    row = lax.broadcasted_iota(jnp.int32, (tm, 1), 0)
    lo0 = 3 * RWKV_DIM

    def proj(c0, c1):
        return _dot(h, wmix_ref[:, c0:c1])

    def token_shift(ps, c0, c1):
        carry = jnp.where(first, 0.0, prev_ref[:, c0:c1])
        prev = jnp.where(row == 0, carry, pltpu.roll(ps, 1, axis=0))
        prev_ref[:, c0:c1] = ps[tm - 1:tm, :]
        return ps + (prev - ps) * mu_ref[:, c0:c1]

    p_pq = proj(RWKV_IN, MIX_IN)
    p_lo = proj(lo0, RWKV_IN)
    p_r = proj(0, RWKV_DIM)
    pp = p_pq[:, :POOL_DIM]
    q = p_pq[:, POOL_DIM:]

    kmem = kv_ref[:, :MEM_DIM]
    vmem = kv_ref[:, MEM_DIM:]
    lane_head = lax.broadcasted_iota(jnp.int32, (1, MEM_DIM), 1) // HEAD_DIM
    scores = [_dot_nt(jnp.where(lane_head == hd, q, 0.0).astype(BF16), kmem) * (HEAD_DIM ** -0.5)
              for hd in range(MEM_HEADS)]

    p_k = proj(RWKV_DIM, 2 * RWKV_DIM)
    p_v = proj(2 * RWKV_DIM, lo0)

    lo = token_shift(p_lo, lo0, RWKV_IN)
    wa = lo[:, :DECAY_LORA + ICLR_LORA]
    gd = lo[:, DECAY_LORA + ICLR_LORA:]
    z = w0_ref[...] + _dot(jnp.tanh(wa).astype(BF16), wl_ref[...])
    put(PACK_LW, (-math.exp(-0.5)) * jax.nn.sigmoid(z))
    a = jax.nn.sigmoid(a0_ref[...] + _dot(wa.astype(BF16), al_ref[...]))
    put(PACK_G, _dot(jax.nn.sigmoid(gd).astype(BF16), gl_ref[...]))

    halo = jnp.where(first, 0.0, halo_ref[...])
    halo_ref[...] = pp[tm - POOL_HALO:, :]
    ext = jnp.concatenate([halo, pp], axis=0)
    lane_group = lax.broadcasted_iota(jnp.int32, (1, POOL_DIM), 1) // POOL_GROUP_DIM
    win = jnp.zeros((1, POOL_DIM), jnp.int32)
    sel = jnp.zeros((tm, POOL_DIM), F32)
    acc = ext
    shift = 1
    for gi, w in enumerate(POOL_WINDOWS):
        while shift < w:
            acc = acc + pltpu.roll(acc, shift, axis=0)
            shift *= 2
        sel = jnp.where(lane_group == gi, acc[POOL_HALO:, :], sel)
        win = jnp.where(lane_group == gi, w, win)
    pos = t * tm + row + 1
    cnt = jnp.minimum(pos, win).astype(F32)
    dpool = sel / cnt - pp
    bc_out[:, :POOL_DIM] = _dot(dpool.astype(BF16), poolw_ref[...]) * pools_ref[...]

    cols = []
    for j in range(0, MEM_DIM, LANES):
        col = None
        for hd in range(j // HEAD_DIM, (j + LANES) // HEAD_DIM):
            s = scores[hd]
            e = jnp.exp(s - jnp.max(s, axis=-1, keepdims=True))
            prob = e / jnp.sum(e, axis=-1, keepdims=True)
            vh = jnp.where((lane_head == hd)[:, j:j + LANES], vmem[:, j:j + LANES], 0.0)
            o = _dot(prob.astype(BF16), vh)
            col = o if col is None else col + o
        cols.append(col)
    bc_out[:, POOL_DIM:] = jnp.concatenate(cols, axis=1)

    r = token_shift(p_r, 0, RWKV_DIM)
    k = token_shift(p_k, RWKV_DIM, 2 * RWKV_DIM)
    v = token_shift(p_v, 2 * RWKV_DIM, lo0)
    kkr = k * kk_ref[...]
    kkn = kkr * lax.rsqrt(_head_lane_sums(kkr * kkr) + 1e-12)
    kf = k * (1.0 + (a - 1.0) * ka_ref[...])
    put(PACK_R, r)
    put(PACK_K, kf)
    put(PACK_V, v)
    put(PACK_KK, kkn)
    put(PACK_B, kkn * a)
    put(PACK_BV, _head_lane_sums(r * kf * rk_ref[...]) * v)


def _chunk_masks():
    c = CHUNK
    ri = lax.broadcasted_iota(jnp.int32, (2 * c, 2 * c), 0)
    ci = lax.broadcasted_iota(jnp.int32, (2 * c, 2 * c), 1)
    rt = ri % c
    ct = ci % c
    aa_mask = (rt > ct) | ((ri >= c) & (rt == ct))
    r64 = lax.broadcasted_iota(jnp.int32, (c, 2 * c), 0)
    c64 = lax.broadcasted_iota(jnp.int32, (c, 2 * c), 1) % c
    eye = r64 == c64
    levels = []
    blk = 2
    while blk < c:
        same_outer = (r64 // (2 * blk)) == (c64 // (2 * blk))
        diff_inner = (r64 // blk) != (c64 // blk)
        levels.append(same_outer & diff_inner & (r64 > c64))
        blk *= 2
    diag2 = ((r64 // 2) == (c64 // 2)) & (r64 > c64)
    tri_incl = lax.broadcasted_iota(jnp.int32, (c, c), 0) >= lax.broadcasted_iota(jnp.int32, (c, c), 1)
    block_diag = (ri < c) == (ci < c)
    eye2 = ri == ci
    return aa_mask, eye, diag2, levels, tri_incl, block_diag, eye2


def _rwkv_scan_kernel(r_ref, k_ref, v_ref, kk_ref, b_ref, lw_ref, y_ref, state_ref):
    c = CHUNK
    n = HEAD_DIM
    chunk_idx = pl.program_id(1)

    @pl.when(chunk_idx == 0)
    def _():
        state_ref[...] = jnp.zeros_like(state_ref)

    aa_mask, eye, diag2, levels, tri_incl, block_diag, eye2 = _chunk_masks()
    tri = jnp.where(tri_incl, 1.0, 0.0).astype(BF16)
    eye_f = jnp.where(eye, 1.0, 0.0)
    zeros_b = jnp.zeros((c, LANES), BF16)
    head_a = lax.broadcasted_iota(jnp.int32, (c, LANES), 1) < n
    only_a = jnp.where(head_a, 1.0, 0.0).astype(BF16)
    only_b = jnp.where(head_a, 0.0, 1.0).astype(BF16)
    only_a2 = jnp.concatenate([only_a, only_a], axis=0)
    only_b2 = jnp.concatenate([only_b, only_b], axis=0)
    lvl_a = [jnp.where(lvl & head_a, 1.0, 0.0).astype(BF16) for lvl in levels]
    lvl_b = [jnp.where(lvl & ~head_a, 1.0, 0.0).astype(BF16) for lvl in levels]

    def both(xb, ma=only_a, mb=only_b):
        return jnp.concatenate([xb * ma, xb * mb], axis=0)

    nb = r_ref.shape[0]
    items = [(bi, j) for bi in range(nb) for j in range(RWKV_DIM // LANES)]
    sls = [slice(j * LANES, (j + 1) * LANES) for _, j in items]
    ni = len(items)

    pre = []
    for bi in range(nb):
        lw = lw_ref[bi]
        cum = _split3_dot_left(tri, lw)
        tot = cum[c - 1:c, :]
        e_inv = jnp.exp(-cum)
        e_end = jnp.exp(tot - cum)
        r_all = r_ref[bi]
        k_all = k_ref[bi]
        b_all = b_ref[bi]
        pre.append(dict(
            rt=r_all * jnp.exp(cum), kkt=kk_ref[bi] * jnp.exp(cum - lw), bt=b_all * e_inv, kt=k_all * e_inv,
            bd=b_all * e_end, kd=k_all * e_end, e_tot=jnp.exp(tot), v=v_ref[bi]))

    rt = [pre[bi]["rt"][:, sl] for (bi, _), sl in zip(items, sls)]
    kkt = [pre[bi]["kkt"][:, sl] for (bi, _), sl in zip(items, sls)]
    vh = [pre[bi]["v"][:, sl] for (bi, _), sl in zip(items, sls)]
    aa_a, aa_b = [], []
    for i, ((bi, _), sl) in enumerate(zip(items, sls)):
        lhs = jnp.concatenate([kkt[i], rt[i]], axis=0).astype(BF16)
        rhs = jnp.concatenate([pre[bi]["bt"][:, sl], pre[bi]["kt"][:, sl]], axis=0).astype(BF16)
        aa2 = _dot_nt(jnp.concatenate([lhs * only_a2, lhs * only_b2], axis=0), rhs)
        aa_a.append(jnp.where(aa_mask, aa2[:2 * c], 0.0))
        aa_b.append(jnp.where(aa_mask, aa2[2 * c:], 0.0))
    av = []
    for i in range(ni):
        vb = vh[i].astype(BF16)
        v_rows = jnp.concatenate([zeros_b, vb * only_a, zeros_b, vb * only_b], axis=0)
        av.append(_dot(jnp.concatenate([aa_a[i], aa_b[i]], axis=1).astype(BF16), v_rows))
    a_ab = [jnp.where(head_a, aa_a[i][:c], pltpu.roll(aa_b[i][:c], n, axis=1)).astype(BF16) for i in range(ni)]
    a_rb = [jnp.where(head_a, aa_a[i][c:], pltpu.roll(aa_b[i][c:], n, axis=1)).astype(BF16) for i in range(ni)]

    tinv = [eye_f - jnp.where(diag2, a.astype(F32), 0.0) for a in a_ab]
    for li in range(len(levels)):
        tb = [t_.astype(BF16) for t_ in tinv]
        half = [_dot(tb[i], both(a_ab[i], lvl_a[li], lvl_b[li])).astype(BF16) for i in range(ni)]
        tinv = [tinv[i] - _dot(half[i], both(tb[i])) for i in range(ni)]

    w12 = [_dot(tinv[i].astype(BF16),
                jnp.concatenate([both(kkt[i].astype(BF16)), both(av[i][:c].astype(BF16))], axis=1))
           for i in range(ni)]
    w12_b = [w.astype(BF16) for w in w12]
    pq = [jnp.concatenate([rt[i], av[i][c:]], axis=1)
          - _dot(a_rb[i], jnp.concatenate([both(w12_b[i][:, :LANES]), both(w12_b[i][:, LANES:])], axis=1))
          for i in range(ni)]
    mg = []
    for i, ((bi, _), sl) in enumerate(zip(items, sls)):
        neg_bd_kd = jnp.concatenate([-pre[bi]["bd"][:, sl], pre[bi]["kd"][:, sl]], axis=0).astype(BF16)
        wv = jnp.concatenate([w12_b[i], jnp.concatenate([zeros_b, vh[i].astype(BF16)], axis=1)], axis=0)
        mg.append(_dot_tn(neg_bd_kd, wv))

    for i, ((bi, _), sl) in enumerate(zip(items, sls)):
        m_mat = jnp.where(block_diag, mg[i][:, :LANES], 0.0) + jnp.where(eye2, pre[bi]["e_tot"][:, sl], 0.0)
        pm = jnp.concatenate([pq[i][:, :LANES], m_mat], axis=0).astype(BF16)
        out = _dot(pm, state_ref[i].astype(BF16))
        y_ref[bi, :, sl] = out[:c] + pq[i][:, LANES:]
        state_ref[i] = out[c:] + jnp.where(block_diag, mg[i][:, LANES:], 0.0)


def _merge_kernel(x_ref, y_ref, gbv_ref, bc_ref, g_ref, lnw_ref, lnb_ref, wg_ref, bg_ref,
                  wr_ref, wp_ref, wm_ref, wo_ref, out_ref):
    x = x_ref[...]
    h = _rms_norm(x, g_ref[...]).astype(BF16)
    d = D_MODEL
    z = [_dot(h, wg_ref[:, i * d:(i + 1) * d]) for i in range(3)]
    up_b = _dot(bc_ref[:, :POOL_DIM].astype(BF16), wp_ref[...])
    up_c = _dot(bc_ref[:, POOL_DIM:].astype(BF16), wm_ref[...])
    inv_n = 1.0 / HEAD_DIM
    up_a = None
    half = RWKV_DIM // 2
    for c0 in (0, half):
        cs = slice(c0, c0 + half)
        y = y_ref[:, cs]
        yc = y - _head_lane_sums(y) * inv_n
        var = _head_lane_sums(yc * yc) * inv_n
        bonus = gbv_ref[:, RWKV_DIM + c0:RWKV_DIM + c0 + half]
        ya = (yc * lax.rsqrt(var + GN_EPS) * lnw_ref[:, cs] + lnb_ref[:, cs] + bonus) * gbv_ref[:, cs]
        part = _dot(ya.astype(BF16), wr_ref[cs, :])
        up_a = part if up_a is None else up_a + part
    gate = [jax.nn.sigmoid(z[i] + bg_ref[:, i * d:(i + 1) * d]) for i in range(3)]
    merged = gate[0] * up_a + gate[1] * up_b + gate[2] * up_c
    out_ref[...] = x + _dot(merged.astype(BF16), wo_ref[...])


def _conv_ffn_kernel(x_ref, g_ref, win_ref, cw_ref, cb_ref, wout_ref, gf_ref, out_ref, tail_ref, act_ref):
    t = pl.program_id(1)
    tm = x_ref.shape[0]
    x = x_ref[...]
    h = _rms_norm(x, g_ref[...]).astype(BF16)
    row = lax.broadcasted_iota(jnp.int32, (tm, 1), 0)
    first = t == 0
    n_chunks = D_FF // FF_CHUNK

    def up_proj(j):
        return (_dot(h, win_ref[:, j * FF_CHUNK:(j + 1) * FF_CHUNK]),
                _dot(h, win_ref[:, D_FF + j * FF_CHUNK:D_FF + (j + 1) * FF_CHUNK]))

    nxt = up_proj(0)
    for j in range(n_chunks):
        cs = slice(j * FF_CHUNK, (j + 1) * FF_CHUNK)
        u, gv = nxt
        if j + 1 < n_chunks:
            nxt = up_proj(j + 1)
        tail = jnp.where(first, 0.0, tail_ref[:, cs])
        tail_ref[:, cs] = u[tm - 8:, :]
        u1 = jnp.where(row == 0, tail[7:8, :], pltpu.roll(u, 1, axis=0))
        u2 = jnp.where(row == 0, tail[6:7, :], jnp.where(row == 1, tail[7:8, :], pltpu.roll(u, 2, axis=0)))
        uc = cw_ref[0:1, cs] * u2 + cw_ref[1:2, cs] * u1 + cw_ref[2:3, cs] * u + cb_ref[:, cs]
        act = 0.5 * uc * (1.0 + lax.erf(uc * (2.0 ** -0.5))) * gv
        act_ref[:, cs] = act.astype(BF16)
    out_ref[...] = _rms_norm(x + _dot(act_ref[...], wout_ref[...]), gf_ref[...])


def _const_spec(shape):
    nd = len(shape)
    return pl.BlockSpec(shape, lambda *_: (0,) * nd)


def kernel(x, mem, norm_mix_g, w_in_mix, mu_shift, w0, w_lora_b, a0, a_lora_b, g_lora_b, k_k, k_a, r_k,
           ln_x_w, ln_x_b, pool_w, pool_scale, norm_mem_g, w_mem_kv, w_up_rwkv, w_up_pool, w_up_mem,
           w_gate, b_gate, w_o, norm_ffn_g, w_ffn_in, ffn_conv_w, ffn_conv_b, w_ffn_out, norm_final_g):
    bsz, seq, d = x.shape
    n_mem = mem.shape[1]
    assert d == D_MODEL and seq % TM_PROJ == 0 and seq % CHUNK == 0 and bsz % SCAN_BATCH == 0
    assert seq % TM_FFN == 0 and (bsz * seq) % TM_MERGE == 0
    assert norm_mix_g.shape[0] == 1, "single-layer block"
    l = 0
    row = lambda a: a.reshape(1, -1).astype(F32)

    wmix = w_in_mix[l].astype(BF16)
    zpad = jnp.zeros((DECAY_LORA, RWKV_DIM), F32)
    wl_pad = jnp.concatenate([w_lora_b[l], zpad], axis=0).astype(BF16)
    al_pad = jnp.concatenate([zpad, a_lora_b[l]], axis=0).astype(BF16)
    gl = g_lora_b[l].astype(BF16)
    pool_bd = jax.scipy.linalg.block_diag(*[pool_w[l, i] for i in range(len(POOL_WINDOWS))]).astype(BF16)

    kv = pl.pallas_call(
        _mem_kv_kernel,
        out_shape=jax.ShapeDtypeStruct((bsz, n_mem, 2 * MEM_DIM), BF16),
        grid=(bsz,),
        in_specs=[pl.BlockSpec((None, n_mem, d), lambda b: (b, 0, 0)),
                  _const_spec((1, d)), _const_spec((d, 2 * MEM_DIM))],
        out_specs=pl.BlockSpec((None, n_mem, 2 * MEM_DIM), lambda b: (b, 0, 0)),
        compiler_params=pltpu.CompilerParams(dimension_semantics=("arbitrary",)),
        name="mem_kv",
    )(mem, row(norm_mem_g[l]), w_mem_kv[l].astype(BF16))

    tok = lambda w: pl.BlockSpec((None, TM_PROJ, w), lambda b, t: (b, t, 0))
    f32_out = lambda w: jax.ShapeDtypeStruct((bsz, seq, w), F32)
    outs = pl.pallas_call(
        _mix_proj_kernel,
        out_shape=[f32_out(PACK_SLOTS * RWKV_DIM), f32_out(POOL_DIM + MEM_DIM)],
        grid=(bsz, seq // TM_PROJ),
        in_specs=[tok(d), _const_spec((1, d)), _const_spec((d, MIX_IN)), _const_spec((1, RWKV_IN)),
                  _const_spec((1, RWKV_DIM)), _const_spec((DECAY_LORA + ICLR_LORA, RWKV_DIM)),
                  _const_spec((1, RWKV_DIM)), _const_spec((DECAY_LORA + ICLR_LORA, RWKV_DIM)),
                  _const_spec((GATE_LORA, RWKV_DIM)), _const_spec((1, RWKV_DIM)), _const_spec((1, RWKV_DIM)),
                  _const_spec((1, RWKV_DIM)), _const_spec((POOL_DIM, POOL_DIM)),
                  _const_spec((1, POOL_DIM)),
                  pl.BlockSpec((None, n_mem, 2 * MEM_DIM), lambda b, t: (b, 0, 0))],
        out_specs=[tok(PACK_SLOTS * RWKV_DIM), tok(POOL_DIM + MEM_DIM)],
        scratch_shapes=[pltpu.VMEM((1, RWKV_IN), F32), pltpu.VMEM((POOL_HALO, POOL_DIM), F32)],
        compiler_params=pltpu.CompilerParams(dimension_semantics=("arbitrary", "arbitrary"),
                                             vmem_limit_bytes=V7X_VMEM_LIMIT_BYTES),
        name="mix_proj",
    )(x, row(norm_mix_g[l]), wmix, row(mu_shift[l]), row(w0[l]), wl_pad, row(a0[l]), al_pad, gl,
      row(k_k[l]), row(k_a[l]), row(r_k[l]), pool_bd, row(pool_scale[l]), kv)
    packed, y_bc = outs

    sb = SCAN_BATCH
    blk = pl.BlockSpec((sb, CHUNK, RWKV_DIM), lambda b, c: (b, c, 0))
    slot = lambda j: pl.BlockSpec((sb, CHUNK, RWKV_DIM), lambda b, c: (b, c, j))
    y_a = pl.pallas_call(
        _rwkv_scan_kernel,
        out_shape=f32_out(RWKV_DIM),
        grid=(bsz // sb, seq // CHUNK),
        in_specs=[slot(j) for j in (PACK_R, PACK_K, PACK_V, PACK_KK, PACK_B, PACK_LW)],
        out_specs=blk,
        scratch_shapes=[pltpu.VMEM((sb * RWKV_DIM // LANES, LANES, LANES), F32)],
        compiler_params=pltpu.CompilerParams(dimension_semantics=("arbitrary", "arbitrary")),
        name="rwkv_scan",
    )(*([packed] * 6))

    n_tok = bsz * seq
    flat = lambda a: a.reshape(n_tok, a.shape[-1])
    tokm = lambda w: pl.BlockSpec((TM_MERGE, w), lambda i: (i, 0))
    wspec = lambda shape: pl.BlockSpec(shape, lambda i: (0, 0), pipeline_mode=pl.Buffered(1))
    x1 = pl.pallas_call(
        _merge_kernel,
        out_shape=jax.ShapeDtypeStruct((n_tok, d), F32),
        grid=(n_tok // TM_MERGE,),
        in_specs=[tokm(d), tokm(RWKV_DIM),
                  pl.BlockSpec((TM_MERGE, 2 * RWKV_DIM), lambda i: (i, PACK_G // 2)), tokm(POOL_DIM + MEM_DIM),
                  _const_spec((1, d)), _const_spec((1, RWKV_DIM)), _const_spec((1, RWKV_DIM)),
                  wspec((d, 3 * d)), _const_spec((1, 3 * d)), wspec((RWKV_DIM, d)),
                  wspec((POOL_DIM, d)), wspec((MEM_DIM, d)), wspec((d, d))],
        out_specs=tokm(d),
        compiler_params=pltpu.CompilerParams(dimension_semantics=("arbitrary",),
                                             vmem_limit_bytes=V7X_VMEM_LIMIT_BYTES),
        name="merge",
    )(flat(x), flat(y_a), flat(packed), flat(y_bc), row(norm_mix_g[l]),
      row(ln_x_w[l]), row(ln_x_b[l]), w_gate[l].astype(BF16),
      row(b_gate[l]), w_up_rwkv[l].astype(BF16), w_up_pool[l].astype(BF16), w_up_mem[l].astype(BF16),
      w_o[l].astype(BF16))

    tokf = pl.BlockSpec((None, TM_FFN, d), lambda b, t: (b, t, 0))
    single = pl.Buffered(1)
    out = pl.pallas_call(
        _conv_ffn_kernel,
        out_shape=jax.ShapeDtypeStruct((bsz, seq, d), F32),
        grid=(bsz, seq // TM_FFN),
        in_specs=[tokf, _const_spec((1, d)),
                  pl.BlockSpec((d, 2 * D_FF), lambda b, t: (0, 0), pipeline_mode=single),
                  _const_spec((3, D_FF)), _const_spec((1, D_FF)),
                  pl.BlockSpec((D_FF, d), lambda b, t: (0, 0), pipeline_mode=single),
                  _const_spec((1, d))],
        out_specs=tokf,
        scratch_shapes=[pltpu.VMEM((8, D_FF), F32), pltpu.VMEM((TM_FFN, D_FF), BF16)],
        compiler_params=pltpu.CompilerParams(dimension_semantics=("arbitrary", "arbitrary"),
                                             vmem_limit_bytes=V7X_VMEM_LIMIT_BYTES),
        name="conv_ffn",
    )(x1.reshape(bsz, seq, d), row(norm_ffn_g[l]), w_ffn_in[l].astype(BF16), ffn_conv_w[l].astype(F32),
      row(ffn_conv_b[l]), w_ffn_out[l].astype(BF16), row(norm_final_g))
    return out
```

```python
import functools
import math

import jax
import jax.numpy as jnp
from jax import lax
from jax.experimental import pallas as pl
from jax.experimental.pallas import tpu as pltpu

F32 = jnp.float32
BF16 = jnp.bfloat16

D_MODEL = 1024
HEAD_DIM = 64
RWKV_HEADS = 8
RWKV_DIM = RWKV_HEADS * HEAD_DIM
DECAY_LORA = 64
ICLR_LORA = 64
GATE_LORA = 128
RWKV_IN = 3 * RWKV_DIM + DECAY_LORA + ICLR_LORA + GATE_LORA
POOL_WINDOWS = (2, 4, 8, 16)
POOL_GROUP_DIM = 64
POOL_DIM = len(POOL_WINDOWS) * POOL_GROUP_DIM
POOL_HALO = 16
MEM_HEADS = 4
MEM_DIM = MEM_HEADS * HEAD_DIM
MIX_IN = RWKV_IN + POOL_DIM + MEM_DIM
D_FF = 2816
NORM_EPS = 1e-6
GN_EPS = 64e-5
CHUNK = 64
FF_CHUNK = 256
V7X_VMEM_LIMIT_BYTES = 56 * 1024 * 1024
LANES = 128
BF16_SUBLANES = 16
PACK_R, PACK_K, PACK_V, PACK_KK, PACK_B, PACK_LW, PACK_G, PACK_BV = range(8)
PACK_SLOTS = 8

TM_PROJ = 512
TM_MERGE = 512
TM_FFN = 1024
SCAN_BATCH = 8


def _dot(a, b):
    return jnp.dot(a, b, preferred_element_type=F32)


def _dot_nt(a, b):
    return lax.dot_general(a, b, (((1,), (1,)), ((), ())), preferred_element_type=F32)


def _dot_tn(a, b):
    return lax.dot_general(a, b, (((0,), (0,)), ((), ())), preferred_element_type=F32)


def _rms_norm(x, g):
    return x * lax.rsqrt(jnp.mean(x * x, axis=-1, keepdims=True) + NORM_EPS) * g


def _split3_dot_left(w_bf16, x):
    hi = x.astype(BF16)
    r1 = x - hi.astype(F32)
    mid = r1.astype(BF16)
    lo = (r1 - mid.astype(F32)).astype(BF16)
    return _dot(w_bf16, hi) + _dot(w_bf16, mid) + _dot(w_bf16, lo)


def _head_lane_sums(x):
    first_head = lax.broadcasted_iota(jnp.int32, (1, LANES), 1) < HEAD_DIM
    cols = []
    for j in range(0, x.shape[1], LANES):
        col = x[:, j:j + LANES]
        sa = jnp.sum(jnp.where(first_head, col, 0.0), axis=-1, keepdims=True)
        sb = jnp.sum(jnp.where(first_head, 0.0, col), axis=-1, keepdims=True)
        cols.append(jnp.where(first_head, sa, sb))
    return cols[0] if len(cols) == 1 else jnp.concatenate(cols, axis=1)


def _mem_kv_kernel(mem_ref, g_ref, w_ref, kv_ref):
    m = _rms_norm(mem_ref[...], g_ref[...])
    kv_ref[...] = _dot(m.astype(BF16), w_ref[...]).astype(BF16)


def _mix_proj_kernel(x_ref, g_ref, wmix_ref, mu_ref, w0_ref, wl_ref, a0_ref, al_ref, gl_ref,
                     kk_ref, ka_ref, rk_ref, poolw_ref, pools_ref, kv_ref,
                     pack_out, bc_out,
                     prev_ref, halo_ref):
    t = pl.program_id(1)
    tm = x_ref.shape[0]
    first = t == 0

    def put(slot, val):
        pack_out[:, slot * RWKV_DIM:(slot + 1) * RWKV_DIM] = val

    h = _rms_norm(x_ref[...], g_ref[...]).astype(BF16)
    row = lax.broadcasted_iota(jnp.int32, (tm, 1), 0)
    lo0 = 3 * RWKV_DIM

    def proj(c0, c1):
        return _dot(h, wmix_ref[:, c0:c1])

    def token_shift(ps, c0, c1):
        carry = jnp.where(first, 0.0, prev_ref[:, c0:c1])
        prev = jnp.where(row == 0, carry, pltpu.roll(ps, 1, axis=0))
        prev_ref[:, c0:c1] = ps[tm - 1:tm, :]
        return ps + (prev - ps) * mu_ref[:, c0:c1]

    p_pq = proj(RWKV_IN, MIX_IN)
    p_lo = proj(lo0, RWKV_IN)
    p_r = proj(0, RWKV_DIM)
    pp = p_pq[:, :POOL_DIM]
    q = p_pq[:, POOL_DIM:]

    kmem = kv_ref[:, :MEM_DIM]
    vmem = kv_ref[:, MEM_DIM:]
    lane_head = lax.broadcasted_iota(jnp.int32, (1, MEM_DIM), 1) // HEAD_DIM
    scores = [_dot_nt(jnp.where(lane_head == hd, q, 0.0).astype(BF16), kmem) * (HEAD_DIM ** -0.5)
              for hd in range(MEM_HEADS)]

    p_k = proj(RWKV_DIM, 2 * RWKV_DIM)
    p_v = proj(2 * RWKV_DIM, lo0)

    lo = token_shift(p_lo, lo0, RWKV_IN)
    wa = lo[:, :DECAY_LORA + ICLR_LORA]
    gd = lo[:, DECAY_LORA + ICLR_LORA:]
    z = w0_ref[...] + _dot(jnp.tanh(wa).astype(BF16), wl_ref[...])
    put(PACK_LW, (-math.exp(-0.5)) * jax.nn.sigmoid(z))
    a = jax.nn.sigmoid(a0_ref[...] + _dot(wa.astype(BF16), al_ref[...]))
    put(PACK_G, _dot(jax.nn.sigmoid(gd).astype(BF16), gl_ref[...]))

    halo = jnp.where(first, 0.0, halo_ref[...])
    halo_ref[...] = pp[tm - POOL_HALO:, :]
    ext = jnp.concatenate([halo, pp], axis=0)
    lane_group = lax.broadcasted_iota(jnp.int32, (1, POOL_DIM), 1) // POOL_GROUP_DIM
    win = jnp.zeros((1, POOL_DIM), jnp.int32)
    sel = jnp.zeros((tm, POOL_DIM), F32)
    acc = ext
    shift = 1
    for gi, w in enumerate(POOL_WINDOWS):
        while shift < w:
            acc = acc + pltpu.roll(acc, shift, axis=0)
            shift *= 2
        sel = jnp.where(lane_group == gi, acc[POOL_HALO:, :], sel)
        win = jnp.where(lane_group == gi, w, win)
    pos = t * tm + row + 1
    cnt = jnp.minimum(pos, win).astype(F32)
    dpool = sel / cnt - pp
    bc_out[:, :POOL_DIM] = _dot(dpool.astype(BF16), poolw_ref[...]) * pools_ref[...]

    cols = []
    for j in range(0, MEM_DIM, LANES):
        col = None
        for hd in range(j // HEAD_DIM, (j + LANES) // HEAD_DIM):
            s = scores[hd]
            e = jnp.exp(s - jnp.max(s, axis=-1, keepdims=True))
            prob = e / jnp.sum(e, axis=-1, keepdims=True)
            vh = jnp.where((lane_head == hd)[:, j:j + LANES], vmem[:, j:j + LANES], 0.0)
            o = _dot(prob.astype(BF16), vh)
            col = o if col is None else col + o
        cols.append(col)
    bc_out[:, POOL_DIM:] = jnp.concatenate(cols, axis=1)

    r = token_shift(p_r, 0, RWKV_DIM)
    k = token_shift(p_k, RWKV_DIM, 2 * RWKV_DIM)
    v = token_shift(p_v, 2 * RWKV_DIM, lo0)
    kkr = k * kk_ref[...]
    kkn = kkr * lax.rsqrt(_head_lane_sums(kkr * kkr) + 1e-12)
    kf = k * (1.0 + (a - 1.0) * ka_ref[...])
    put(PACK_R, r)
    put(PACK_K, kf)
    put(PACK_V, v)
    put(PACK_KK, kkn)
    put(PACK_B, kkn * a)
    put(PACK_BV, _head_lane_sums(r * kf * rk_ref[...]) * v)


def _chunk_masks():
    c = CHUNK
    ri = lax.broadcasted_iota(jnp.int32, (2 * c, 2 * c), 0)
    ci = lax.broadcasted_iota(jnp.int32, (2 * c, 2 * c), 1)
    rt = ri % c
    ct = ci % c
    aa_mask = (rt > ct) | ((ri >= c) & (rt == ct))
    r64 = lax.broadcasted_iota(jnp.int32, (c, 2 * c), 0)
    c64 = lax.broadcasted_iota(jnp.int32, (c, 2 * c), 1) % c
    eye = r64 == c64
    levels = []
    blk = 2
    while blk < c:
        same_outer = (r64 // (2 * blk)) == (c64 // (2 * blk))
        diff_inner = (r64 // blk) != (c64 // blk)
        levels.append(same_outer & diff_inner & (r64 > c64))
        blk *= 2
    diag2 = ((r64 // 2) == (c64 // 2)) & (r64 > c64)
    tri_incl = lax.broadcasted_iota(jnp.int32, (c, c), 0) >= lax.broadcasted_iota(jnp.int32, (c, c), 1)
    block_diag = (ri < c) == (ci < c)
    eye2 = ri == ci
    return aa_mask, eye, diag2, levels, tri_incl, block_diag, eye2


def _rwkv_scan_kernel(*refs, cast_steps, n_steps):
    r_ref, k_ref, v_ref, kk_ref, b_ref, lw_ref = refs[:6]
    nw = len(cast_steps)
    w_refs = refs[6:6 + nw]
    y_ref = refs[6 + nw]
    wb_refs = refs[7 + nw:7 + 2 * nw]
    state_ref = refs[7 + 2 * nw]
    c = CHUNK
    n = HEAD_DIM
    chunk_idx = pl.program_id(1)

    for w_ref, wb_ref, steps in zip(w_refs, wb_refs, cast_steps):
        if steps == n_steps:
            wb_ref[...] = w_ref[...].astype(BF16)
        else:
            @pl.when(chunk_idx < steps)
            def _(w_ref=w_ref, wb_ref=wb_ref):
                wb_ref[...] = w_ref[...].astype(BF16)

    @pl.when(chunk_idx == 0)
    def _():
        state_ref[...] = jnp.zeros_like(state_ref)

    aa_mask, eye, diag2, levels, tri_incl, block_diag, eye2 = _chunk_masks()
    tri = jnp.where(tri_incl, 1.0, 0.0).astype(BF16)
    eye_f = jnp.where(eye, 1.0, 0.0)
    zeros_b = jnp.zeros((c, LANES), BF16)
    head_a = lax.broadcasted_iota(jnp.int32, (c, LANES), 1) < n
    only_a = jnp.where(head_a, 1.0, 0.0).astype(BF16)
    only_b = jnp.where(head_a, 0.0, 1.0).astype(BF16)
    only_a2 = jnp.concatenate([only_a, only_a], axis=0)
    only_b2 = jnp.concatenate([only_b, only_b], axis=0)
    lvl_a = [jnp.where(lvl & head_a, 1.0, 0.0).astype(BF16) for lvl in levels]
    lvl_b = [jnp.where(lvl & ~head_a, 1.0, 0.0).astype(BF16) for lvl in levels]

    def both(xb, ma=only_a, mb=only_b):
        return jnp.concatenate([xb * ma, xb * mb], axis=0)

    nb = r_ref.shape[0]
    items = [(bi, j) for bi in range(nb) for j in range(RWKV_DIM // LANES)]
    sls = [slice(j * LANES, (j + 1) * LANES) for _, j in items]
    ni = len(items)

    pre = []
    for bi in range(nb):
        lw = lw_ref[bi]
        cum = _split3_dot_left(tri, lw)
        tot = cum[c - 1:c, :]
        e_inv = jnp.exp(-cum)
        e_end = jnp.exp(tot - cum)
        r_all = r_ref[bi]
        k_all = k_ref[bi]
        b_all = b_ref[bi]
        pre.append(dict(
            rt=r_all * jnp.exp(cum), kkt=kk_ref[bi] * jnp.exp(cum - lw), bt=b_all * e_inv, kt=k_all * e_inv,
            bd=b_all * e_end, kd=k_all * e_end, e_tot=jnp.exp(tot), v=v_ref[bi]))

    rt = [pre[bi]["rt"][:, sl] for (bi, _), sl in zip(items, sls)]
    kkt = [pre[bi]["kkt"][:, sl] for (bi, _), sl in zip(items, sls)]
    vh = [pre[bi]["v"][:, sl] for (bi, _), sl in zip(items, sls)]
    aa_a, aa_b = [], []
    for i, ((bi, _), sl) in enumerate(zip(items, sls)):
        lhs = jnp.concatenate([kkt[i], rt[i]], axis=0).astype(BF16)
        rhs = jnp.concatenate([pre[bi]["bt"][:, sl], pre[bi]["kt"][:, sl]], axis=0).astype(BF16)
        aa2 = _dot_nt(jnp.concatenate([lhs * only_a2, lhs * only_b2], axis=0), rhs)
        aa_a.append(jnp.where(aa_mask, aa2[:2 * c], 0.0))
        aa_b.append(jnp.where(aa_mask, aa2[2 * c:], 0.0))
    av = []
    for i in range(ni):
        vb = vh[i].astype(BF16)
        v_rows = jnp.concatenate([zeros_b, vb * only_a, zeros_b, vb * only_b], axis=0)
        av.append(_dot(jnp.concatenate([aa_a[i], aa_b[i]], axis=1).astype(BF16), v_rows))
    a_ab = [jnp.where(head_a, aa_a[i][:c], pltpu.roll(aa_b[i][:c], n, axis=1)).astype(BF16) for i in range(ni)]
    a_rb = [jnp.where(head_a, aa_a[i][c:], pltpu.roll(aa_b[i][c:], n, axis=1)).astype(BF16) for i in range(ni)]

    tinv = [eye_f - jnp.where(diag2, a.astype(F32), 0.0) for a in a_ab]
    for li in range(len(levels)):
        tb = [t_.astype(BF16) for t_ in tinv]
        half = [_dot(tb[i], both(a_ab[i], lvl_a[li], lvl_b[li])).astype(BF16) for i in range(ni)]
        tinv = [tinv[i] - _dot(half[i], both(tb[i])) for i in range(ni)]

    w12 = [_dot(tinv[i].astype(BF16),
                jnp.concatenate([both(kkt[i].astype(BF16)), both(av[i][:c].astype(BF16))], axis=1))
           for i in range(ni)]
    w12_b = [w.astype(BF16) for w in w12]
    pq = [jnp.concatenate([rt[i], av[i][c:]], axis=1)
          - _dot(a_rb[i], jnp.concatenate([both(w12_b[i][:, :LANES]), both(w12_b[i][:, LANES:])], axis=1))
          for i in range(ni)]
    mg = []
    for i, ((bi, _), sl) in enumerate(zip(items, sls)):
        neg_bd_kd = jnp.concatenate([-pre[bi]["bd"][:, sl], pre[bi]["kd"][:, sl]], axis=0).astype(BF16)
        wv = jnp.concatenate([w12_b[i], jnp.concatenate([zeros_b, vh[i].astype(BF16)], axis=1)], axis=0)
        mg.append(_dot_tn(neg_bd_kd, wv))

    for i, ((bi, _), sl) in enumerate(zip(items, sls)):
        m_mat = jnp.where(block_diag, mg[i][:, :LANES], 0.0) + jnp.where(eye2, pre[bi]["e_tot"][:, sl], 0.0)
        pm = jnp.concatenate([pq[i][:, :LANES], m_mat], axis=0).astype(BF16)
        out = _dot(pm, state_ref[i].astype(BF16))
        y_ref[bi, :, sl] = out[:c] + pq[i][:, LANES:]
        state_ref[i] = out[c:] + jnp.where(block_diag, mg[i][:, LANES:], 0.0)


def _merge_kernel(x_ref, y_ref, gbv_ref, bc_ref, g_ref, lnw_ref, lnb_ref, wg_ref, bg_ref,
                  wr_ref, wp_ref, wm_ref, wo_ref, out_ref):
    x = x_ref[...]
    h = _rms_norm(x, g_ref[...]).astype(BF16)
    d = D_MODEL
    z = [_dot(h, wg_ref[:, i * d:(i + 1) * d]) for i in range(3)]
    up_b = _dot(bc_ref[:, :POOL_DIM].astype(BF16), wp_ref[...])
    up_c = _dot(bc_ref[:, POOL_DIM:].astype(BF16), wm_ref[...])
    inv_n = 1.0 / HEAD_DIM
    up_a = None
    half = RWKV_DIM // 2
    for c0 in (0, half):
        cs = slice(c0, c0 + half)
        y = y_ref[:, cs]
        yc = y - _head_lane_sums(y) * inv_n
        var = _head_lane_sums(yc * yc) * inv_n
        bonus = gbv_ref[:, RWKV_DIM + c0:RWKV_DIM + c0 + half]
        ya = (yc * lax.rsqrt(var + GN_EPS) * lnw_ref[:, cs] + lnb_ref[:, cs] + bonus) * gbv_ref[:, cs]
        part = _dot(ya.astype(BF16), wr_ref[cs, :])
        up_a = part if up_a is None else up_a + part
    gate = [jax.nn.sigmoid(z[i] + bg_ref[:, i * d:(i + 1) * d]) for i in range(3)]
    merged = gate[0] * up_a + gate[1] * up_b + gate[2] * up_c
    out_ref[...] = x + _dot(merged.astype(BF16), wo_ref[...])


def _conv_ffn_kernel(x_ref, g_ref, win_ref, cw_ref, cb_ref, wout_ref, gf_ref, out_ref, tail_ref, act_ref):
    t = pl.program_id(1)
    tm = x_ref.shape[0]
    x = x_ref[...]
    h = _rms_norm(x, g_ref[...]).astype(BF16)
    row = lax.broadcasted_iota(jnp.int32, (tm, 1), 0)
    first = t == 0
    n_chunks = D_FF // FF_CHUNK

    def up_proj(j):
        return (_dot(h, win_ref[:, j * FF_CHUNK:(j + 1) * FF_CHUNK]),
                _dot(h, win_ref[:, D_FF + j * FF_CHUNK:D_FF + (j + 1) * FF_CHUNK]))

    nxt = up_proj(0)
    for j in range(n_chunks):
        cs = slice(j * FF_CHUNK, (j + 1) * FF_CHUNK)
        u, gv = nxt
        if j + 1 < n_chunks:
            nxt = up_proj(j + 1)
        tail = jnp.where(first, 0.0, tail_ref[:, cs])
        tail_ref[:, cs] = u[tm - 8:, :]
        u1 = jnp.where(row == 0, tail[7:8, :], pltpu.roll(u, 1, axis=0))
        u2 = jnp.where(row == 0, tail[6:7, :], jnp.where(row == 1, tail[7:8, :], pltpu.roll(u, 2, axis=0)))
        uc = cw_ref[0:1, cs] * u2 + cw_ref[1:2, cs] * u1 + cw_ref[2:3, cs] * u + cb_ref[:, cs]
        act = 0.5 * uc * (1.0 + lax.erf(uc * (2.0 ** -0.5))) * gv
        act_ref[:, cs] = act.astype(BF16)
    out_ref[...] = _rms_norm(x + _dot(act_ref[...], wout_ref[...]), gf_ref[...])


def _cast_block_rows(rows, max_steps):
    br = BF16_SUBLANES
    while rows % br or rows // br > max_steps:
        br += BF16_SUBLANES
    return br


def _const_spec(shape):
    nd = len(shape)
    return pl.BlockSpec(shape, lambda *_: (0,) * nd)


def kernel(x, mem, norm_mix_g, w_in_mix, mu_shift, w0, w_lora_b, a0, a_lora_b, g_lora_b, k_k, k_a, r_k,
           ln_x_w, ln_x_b, pool_w, pool_scale, norm_mem_g, w_mem_kv, w_up_rwkv, w_up_pool, w_up_mem,
           w_gate, b_gate, w_o, norm_ffn_g, w_ffn_in, ffn_conv_w, ffn_conv_b, w_ffn_out, norm_final_g):
    bsz, seq, d = x.shape
    n_mem = mem.shape[1]
    assert d == D_MODEL and seq % TM_PROJ == 0 and seq % CHUNK == 0 and bsz % SCAN_BATCH == 0
    assert seq % TM_FFN == 0 and (bsz * seq) % TM_MERGE == 0
    assert norm_mix_g.shape[0] == 1, "single-layer block"
    l = 0
    row = lambda a: a.reshape(1, -1).astype(F32)

    wmix = w_in_mix[l].astype(BF16)
    zpad = jnp.zeros((DECAY_LORA, RWKV_DIM), F32)
    wl_pad = jnp.concatenate([w_lora_b[l], zpad], axis=0).astype(BF16)
    al_pad = jnp.concatenate([zpad, a_lora_b[l]], axis=0).astype(BF16)
    gl = g_lora_b[l].astype(BF16)
    pool_bd = jax.scipy.linalg.block_diag(*[pool_w[l, i] for i in range(len(POOL_WINDOWS))]).astype(BF16)

    kv = pl.pallas_call(
        _mem_kv_kernel,
        out_shape=jax.ShapeDtypeStruct((bsz, n_mem, 2 * MEM_DIM), BF16),
        grid=(bsz,),
        in_specs=[pl.BlockSpec((None, n_mem, d), lambda b: (b, 0, 0)),
                  _const_spec((1, d)), _const_spec((d, 2 * MEM_DIM))],
        out_specs=pl.BlockSpec((None, n_mem, 2 * MEM_DIM), lambda b: (b, 0, 0)),
        compiler_params=pltpu.CompilerParams(dimension_semantics=("arbitrary",)),
        name="mem_kv",
    )(mem, row(norm_mem_g[l]), w_mem_kv[l].astype(BF16))

    tok = lambda w: pl.BlockSpec((None, TM_PROJ, w), lambda b, t: (b, t, 0))
    f32_out = lambda w: jax.ShapeDtypeStruct((bsz, seq, w), F32)
    outs = pl.pallas_call(
        _mix_proj_kernel,
        out_shape=[f32_out(PACK_SLOTS * RWKV_DIM), f32_out(POOL_DIM + MEM_DIM)],
        grid=(bsz, seq // TM_PROJ),
        in_specs=[tok(d), _const_spec((1, d)), _const_spec((d, MIX_IN)), _const_spec((1, RWKV_IN)),
                  _const_spec((1, RWKV_DIM)), _const_spec((DECAY_LORA + ICLR_LORA, RWKV_DIM)),
                  _const_spec((1, RWKV_DIM)), _const_spec((DECAY_LORA + ICLR_LORA, RWKV_DIM)),
                  _const_spec((GATE_LORA, RWKV_DIM)), _const_spec((1, RWKV_DIM)), _const_spec((1, RWKV_DIM)),
                  _const_spec((1, RWKV_DIM)), _const_spec((POOL_DIM, POOL_DIM)),
                  _const_spec((1, POOL_DIM)),
                  pl.BlockSpec((None, n_mem, 2 * MEM_DIM), lambda b, t: (b, 0, 0))],
        out_specs=[tok(PACK_SLOTS * RWKV_DIM), tok(POOL_DIM + MEM_DIM)],
        scratch_shapes=[pltpu.VMEM((1, RWKV_IN), F32), pltpu.VMEM((POOL_HALO, POOL_DIM), F32)],
        compiler_params=pltpu.CompilerParams(dimension_semantics=("arbitrary", "arbitrary"),
                                             vmem_limit_bytes=V7X_VMEM_LIMIT_BYTES),
        name="mix_proj",
    )(x, row(norm_mix_g[l]), wmix, row(mu_shift[l]), row(w0[l]), wl_pad, row(a0[l]), al_pad, gl,
      row(k_k[l]), row(k_a[l]), row(r_k[l]), pool_bd, row(pool_scale[l]), kv)
    packed, y_bc = outs

    sb = SCAN_BATCH
    blk = pl.BlockSpec((sb, CHUNK, RWKV_DIM), lambda b, c: (b, c, 0))
    slot = lambda j: pl.BlockSpec((sb, CHUNK, RWKV_DIM), lambda b, c: (b, c, j))
    n_chunks = seq // CHUNK
    later_w = [w_gate[l], w_o[l], w_ffn_in[l], w_ffn_out[l]]
    cast_rows = [_cast_block_rows(w.shape[0], n_chunks) for w in later_w]
    cast_steps = tuple(w.shape[0] // br for w, br in zip(later_w, cast_rows))
    wblk = lambda w, br: pl.BlockSpec((br, w.shape[1]), lambda b, c: (jnp.minimum(c, w.shape[0] // br - 1), 0))
    scan_out = pl.pallas_call(
        functools.partial(_rwkv_scan_kernel, cast_steps=cast_steps, n_steps=n_chunks),
        out_shape=[f32_out(RWKV_DIM)] + [jax.ShapeDtypeStruct(w.shape, BF16) for w in later_w],
        grid=(bsz // sb, n_chunks),
        in_specs=([slot(j) for j in (PACK_R, PACK_K, PACK_V, PACK_KK, PACK_B, PACK_LW)]
                  + [wblk(w, br) for w, br in zip(later_w, cast_rows)]),
        out_specs=[blk] + [wblk(w, br) for w, br in zip(later_w, cast_rows)],
        scratch_shapes=[pltpu.VMEM((sb * RWKV_DIM // LANES, LANES, LANES), F32)],
        compiler_params=pltpu.CompilerParams(dimension_semantics=("arbitrary", "arbitrary")),
        name="rwkv_scan",
    )(*([packed] * 6), *later_w)
    y_a, wg_b, wo_b, wfi_b, wfo_b = scan_out

    n_tok = bsz * seq
    flat = lambda a: a.reshape(n_tok, a.shape[-1])
    tokm = lambda w: pl.BlockSpec((TM_MERGE, w), lambda i: (i, 0))
    wspec = lambda shape: pl.BlockSpec(shape, lambda i: (0, 0), pipeline_mode=pl.Buffered(1))
    x1 = pl.pallas_call(
        _merge_kernel,
        out_shape=jax.ShapeDtypeStruct((n_tok, d), F32),
        grid=(n_tok // TM_MERGE,),
        in_specs=[tokm(d), tokm(RWKV_DIM),
                  pl.BlockSpec((TM_MERGE, 2 * RWKV_DIM), lambda i: (i, PACK_G // 2)), tokm(POOL_DIM + MEM_DIM),
                  _const_spec((1, d)), _const_spec((1, RWKV_DIM)), _const_spec((1, RWKV_DIM)),
                  wspec((d, 3 * d)), _const_spec((1, 3 * d)), wspec((RWKV_DIM, d)),
                  wspec((POOL_DIM, d)), wspec((MEM_DIM, d)), wspec((d, d))],
        out_specs=tokm(d),
        compiler_params=pltpu.CompilerParams(dimension_semantics=("arbitrary",),
                                             vmem_limit_bytes=V7X_VMEM_LIMIT_BYTES),
        name="merge",
    )(flat(x), flat(y_a), flat(packed), flat(y_bc), row(norm_mix_g[l]),
      row(ln_x_w[l]), row(ln_x_b[l]), wg_b, row(b_gate[l]), w_up_rwkv[l].astype(BF16),
      w_up_pool[l].astype(BF16), w_up_mem[l].astype(BF16), wo_b)

    tokf = pl.BlockSpec((None, TM_FFN, d), lambda b, t: (b, t, 0))
    single = pl.Buffered(1)
    out = pl.pallas_call(
        _conv_ffn_kernel,
        out_shape=jax.ShapeDtypeStruct((bsz, seq, d), F32),
        grid=(bsz, seq // TM_FFN),
        in_specs=[tokf, _const_spec((1, d)),
                  pl.BlockSpec((d, 2 * D_FF), lambda b, t: (0, 0), pipeline_mode=single),
                  _const_spec((3, D_FF)), _const_spec((1, D_FF)),
                  pl.BlockSpec((D_FF, d), lambda b, t: (0, 0), pipeline_mode=single),
                  _const_spec((1, d))],
        out_specs=tokf,
        scratch_shapes=[pltpu.VMEM((8, D_FF), F32), pltpu.VMEM((TM_FFN, D_FF), BF16)],
        compiler_params=pltpu.CompilerParams(dimension_semantics=("arbitrary", "arbitrary"),
                                             vmem_limit_bytes=V7X_VMEM_LIMIT_BYTES),
        name="conv_ffn",
    )(x1.reshape(bsz, seq, d), row(norm_ffn_g[l]), wfi_b, ffn_conv_w[l].astype(F32),
      row(ffn_conv_b[l]), wfo_b, row(norm_final_g))
    return out
```

```python
import functools
import math

import jax
import jax.numpy as jnp
from jax import lax
from jax.experimental import pallas as pl
from jax.experimental.pallas import tpu as pltpu

F32 = jnp.float32
BF16 = jnp.bfloat16

D_MODEL = 1024
HEAD_DIM = 64
RWKV_HEADS = 8
RWKV_DIM = RWKV_HEADS * HEAD_DIM
DECAY_LORA = 64
ICLR_LORA = 64
GATE_LORA = 128
RWKV_IN = 3 * RWKV_DIM + DECAY_LORA + ICLR_LORA + GATE_LORA
POOL_WINDOWS = (2, 4, 8, 16)
POOL_GROUP_DIM = 64
POOL_DIM = len(POOL_WINDOWS) * POOL_GROUP_DIM
POOL_HALO = 16
MEM_HEADS = 4
MEM_DIM = MEM_HEADS * HEAD_DIM
MIX_IN = RWKV_IN + POOL_DIM + MEM_DIM
D_FF = 2816
NORM_EPS = 1e-6
GN_EPS = 64e-5
CHUNK = 64
FF_CHUNK = 256
V7X_VMEM_LIMIT_BYTES = 56 * 1024 * 1024
LANES = 128
BF16_SUBLANES = 16
PACK_R, PACK_K, PACK_V, PACK_KK, PACK_B, PACK_LW, PACK_G, PACK_BV = range(8)
PACK_SLOTS = 8

TM_PROJ = 512
TM_MERGE = 512
TM_FFN = 1024
SCAN_BATCH = 8


def _dot(a, b):
    return jnp.dot(a, b, preferred_element_type=F32)


def _dot_nt(a, b):
    return lax.dot_general(a, b, (((1,), (1,)), ((), ())), preferred_element_type=F32)


def _dot_tn(a, b):
    return lax.dot_general(a, b, (((0,), (0,)), ((), ())), preferred_element_type=F32)


def _sigmoid(x):
    return 0.5 * jnp.tanh(0.5 * x) + 0.5


def _rms_norm(x, g):
    return x * lax.rsqrt(jnp.mean(x * x, axis=-1, keepdims=True) + NORM_EPS) * g


def _split3_dot_left(w_bf16, x):
    hi = x.astype(BF16)
    r1 = x - hi.astype(F32)
    mid = r1.astype(BF16)
    lo = (r1 - mid.astype(F32)).astype(BF16)
    return _dot(w_bf16, hi) + _dot(w_bf16, mid) + _dot(w_bf16, lo)


def _head_lane_sums(x):
    first_head = lax.broadcasted_iota(jnp.int32, (1, LANES), 1) < HEAD_DIM
    cols = []
    for j in range(0, x.shape[1], LANES):
        col = x[:, j:j + LANES]
        sa = jnp.sum(jnp.where(first_head, col, 0.0), axis=-1, keepdims=True)
        sb = jnp.sum(jnp.where(first_head, 0.0, col), axis=-1, keepdims=True)
        cols.append(jnp.where(first_head, sa, sb))
    return cols[0] if len(cols) == 1 else jnp.concatenate(cols, axis=1)


def _mem_kv_kernel(mem_ref, g_ref, w_ref, kv_ref):
    m = _rms_norm(mem_ref[...], g_ref[...])
    kv_ref[...] = _dot(m.astype(BF16), w_ref[...]).astype(BF16)


def _mix_proj_kernel(x_ref, g_ref, wmix_ref, mu_ref, w0_ref, wl_ref, a0_ref, al_ref, gl_ref,
                     kk_ref, ka_ref, rk_ref, poolw_ref, pools_ref, kv_ref,
                     pack_out, bc_out,
                     prev_ref, halo_ref):
    t = pl.program_id(1)
    tm = x_ref.shape[0]
    first = t == 0

    def put(slot, val):
        pack_out[:, slot * RWKV_DIM:(slot + 1) * RWKV_DIM] = val

    h = _rms_norm(x_ref[...], g_ref[...]).astype(BF16)
    row = lax.broadcasted_iota(jnp.int32, (tm, 1), 0)
    lo0 = 3 * RWKV_DIM

    def proj(c0, c1):
        return _dot(h, wmix_ref[:, c0:c1])

    def token_shift(ps, c0, c1):
        carry = jnp.where(first, 0.0, prev_ref[:, c0:c1])
        prev = jnp.where(row == 0, carry, pltpu.roll(ps, 1, axis=0))
        prev_ref[:, c0:c1] = ps[tm - 1:tm, :]
        return ps + (prev - ps) * mu_ref[:, c0:c1]

    p_pq = proj(RWKV_IN, MIX_IN)
    p_lo = proj(lo0, RWKV_IN)
    p_r = proj(0, RWKV_DIM)
    pp = p_pq[:, :POOL_DIM]
    q = p_pq[:, POOL_DIM:]

    kmem = kv_ref[:, :MEM_DIM]
    vmem = kv_ref[:, MEM_DIM:]
    lane_head = lax.broadcasted_iota(jnp.int32, (1, MEM_DIM), 1) // HEAD_DIM
    scores = [_dot_nt(jnp.where(lane_head == hd, q, 0.0).astype(BF16), kmem) * (HEAD_DIM ** -0.5)
              for hd in range(MEM_HEADS)]

    p_k = proj(RWKV_DIM, 2 * RWKV_DIM)
    p_v = proj(2 * RWKV_DIM, lo0)

    lo = token_shift(p_lo, lo0, RWKV_IN)
    wa = lo[:, :DECAY_LORA + ICLR_LORA]
    gd = lo[:, DECAY_LORA + ICLR_LORA:]
    z = w0_ref[...] + _dot(jnp.tanh(wa).astype(BF16), wl_ref[...])
    put(PACK_LW, (-math.exp(-0.5)) * _sigmoid(z))
    a = _sigmoid(a0_ref[...] + _dot(wa.astype(BF16), al_ref[...]))
    put(PACK_G, _dot(_sigmoid(gd).astype(BF16), gl_ref[...]))

    halo = jnp.where(first, 0.0, halo_ref[...])
    halo_ref[...] = pp[tm - POOL_HALO:, :]
    ext = jnp.concatenate([halo, pp], axis=0)
    lane_group = lax.broadcasted_iota(jnp.int32, (1, POOL_DIM), 1) // POOL_GROUP_DIM
    win = jnp.zeros((1, POOL_DIM), jnp.int32)
    sel = jnp.zeros((tm, POOL_DIM), F32)
    acc = ext
    shift = 1
    for gi, w in enumerate(POOL_WINDOWS):
        while shift < w:
            acc = acc + pltpu.roll(acc, shift, axis=0)
            shift *= 2
        sel = jnp.where(lane_group == gi, acc[POOL_HALO:, :], sel)
        win = jnp.where(lane_group == gi, w, win)
    pos = t * tm + row + 1
    cnt = jnp.minimum(pos, win).astype(F32)
    dpool = sel / cnt - pp
    bc_out[:, :POOL_DIM] = _dot(dpool.astype(BF16), poolw_ref[...]) * pools_ref[...]

    cols = []
    for j in range(0, MEM_DIM, LANES):
        col = None
        for hd in range(j // HEAD_DIM, (j + LANES) // HEAD_DIM):
            s = scores[hd]
            e = jnp.exp(s - jnp.max(s, axis=-1, keepdims=True))
            prob = e / jnp.sum(e, axis=-1, keepdims=True)
            vh = jnp.where((lane_head == hd)[:, j:j + LANES], vmem[:, j:j + LANES], 0.0)
            o = _dot(prob.astype(BF16), vh)
            col = o if col is None else col + o
        cols.append(col)
    bc_out[:, POOL_DIM:] = jnp.concatenate(cols, axis=1)

    r = token_shift(p_r, 0, RWKV_DIM)
    k = token_shift(p_k, RWKV_DIM, 2 * RWKV_DIM)
    v = token_shift(p_v, 2 * RWKV_DIM, lo0)
    kkr = k * kk_ref[...]
    kkn = kkr * lax.rsqrt(_head_lane_sums(kkr * kkr) + 1e-12)
    kf = k * (1.0 + (a - 1.0) * ka_ref[...])
    put(PACK_R, r)
    put(PACK_K, kf)
    put(PACK_V, v)
    put(PACK_KK, kkn)
    put(PACK_B, kkn * a)
    put(PACK_BV, _head_lane_sums(r * kf * rk_ref[...]) * v)


def _chunk_masks():
    c = CHUNK
    ri = lax.broadcasted_iota(jnp.int32, (2 * c, 2 * c), 0)
    ci = lax.broadcasted_iota(jnp.int32, (2 * c, 2 * c), 1)
    rt = ri % c
    ct = ci % c
    aa_mask = (rt > ct) | ((ri >= c) & (rt == ct))
    r64 = lax.broadcasted_iota(jnp.int32, (c, 2 * c), 0)
    c64 = lax.broadcasted_iota(jnp.int32, (c, 2 * c), 1) % c
    eye = r64 == c64
    levels = []
    blk = 2
    while blk < c:
        same_outer = (r64 // (2 * blk)) == (c64 // (2 * blk))
        diff_inner = (r64 // blk) != (c64 // blk)
        levels.append(same_outer & diff_inner & (r64 > c64))
        blk *= 2
    diag2 = ((r64 // 2) == (c64 // 2)) & (r64 > c64)
    tri_incl = lax.broadcasted_iota(jnp.int32, (c, c), 0) >= lax.broadcasted_iota(jnp.int32, (c, c), 1)
    block_diag = (ri < c) == (ci < c)
    eye2 = ri == ci
    return aa_mask, eye, diag2, levels, tri_incl, block_diag, eye2


def _rwkv_scan_kernel(*refs, cast_steps, n_steps):
    r_ref, k_ref, v_ref, kk_ref, b_ref, lw_ref = refs[:6]
    nw = len(cast_steps)
    w_refs = refs[6:6 + nw]
    y_ref = refs[6 + nw]
    wb_refs = refs[7 + nw:7 + 2 * nw]
    state_ref = refs[7 + 2 * nw]
    c = CHUNK
    n = HEAD_DIM
    chunk_idx = pl.program_id(1)

    for w_ref, wb_ref, steps in zip(w_refs, wb_refs, cast_steps):
        if steps == n_steps:
            wb_ref[...] = w_ref[...].astype(BF16)
        else:
            @pl.when(chunk_idx < steps)
            def _(w_ref=w_ref, wb_ref=wb_ref):
                wb_ref[...] = w_ref[...].astype(BF16)

    @pl.when(chunk_idx == 0)
    def _():
        state_ref[...] = jnp.zeros_like(state_ref)

    aa_mask, eye, diag2, levels, tri_incl, block_diag, eye2 = _chunk_masks()
    tri = jnp.where(tri_incl, 1.0, 0.0).astype(BF16)
    eye_f = jnp.where(eye, 1.0, 0.0)
    zeros_b = jnp.zeros((c, LANES), BF16)
    head_a = lax.broadcasted_iota(jnp.int32, (c, LANES), 1) < n
    only_a = jnp.where(head_a, 1.0, 0.0).astype(BF16)
    only_b = jnp.where(head_a, 0.0, 1.0).astype(BF16)
    only_a2 = jnp.concatenate([only_a, only_a], axis=0)
    only_b2 = jnp.concatenate([only_b, only_b], axis=0)
    lvl_a = [jnp.where(lvl & head_a, 1.0, 0.0).astype(BF16) for lvl in levels]
    lvl_b = [jnp.where(lvl & ~head_a, 1.0, 0.0).astype(BF16) for lvl in levels]

    def both(xb, ma=only_a, mb=only_b):
        return jnp.concatenate([xb * ma, xb * mb], axis=0)

    nb = r_ref.shape[0]
    items = [(bi, j) for bi in range(nb) for j in range(RWKV_DIM // LANES)]
    sls = [slice(j * LANES, (j + 1) * LANES) for _, j in items]
    ni = len(items)

    pre = []
    for bi in range(nb):
        lw = lw_ref[bi]
        cum = _split3_dot_left(tri, lw)
        tot = cum[c - 1:c, :]
        e_inv = jnp.exp(-cum)
        e_end = jnp.exp(tot - cum)
        r_all = r_ref[bi]
        k_all = k_ref[bi]
        b_all = b_ref[bi]
        pre.append(dict(
            rt=r_all * jnp.exp(cum), kkt=kk_ref[bi] * jnp.exp(cum - lw), bt=b_all * e_inv, kt=k_all * e_inv,
            bd=b_all * e_end, kd=k_all * e_end, e_tot=jnp.exp(tot), v=v_ref[bi]))

    rt = [pre[bi]["rt"][:, sl] for (bi, _), sl in zip(items, sls)]
    kkt = [pre[bi]["kkt"][:, sl] for (bi, _), sl in zip(items, sls)]
    vh = [pre[bi]["v"][:, sl] for (bi, _), sl in zip(items, sls)]
    aa_a, aa_b = [], []
    for i, ((bi, _), sl) in enumerate(zip(items, sls)):
        lhs = jnp.concatenate([kkt[i], rt[i]], axis=0).astype(BF16)
        rhs = jnp.concatenate([pre[bi]["bt"][:, sl], pre[bi]["kt"][:, sl]], axis=0).astype(BF16)
        aa2 = _dot_nt(jnp.concatenate([lhs * only_a2, lhs * only_b2], axis=0), rhs)
        aa_a.append(jnp.where(aa_mask, aa2[:2 * c], 0.0))
        aa_b.append(jnp.where(aa_mask, aa2[2 * c:], 0.0))
    av = []
    for i in range(ni):
        vb = vh[i].astype(BF16)
        v_rows = jnp.concatenate([zeros_b, vb * only_a, zeros_b, vb * only_b], axis=0)
        av.append(_dot(jnp.concatenate([aa_a[i], aa_b[i]], axis=1).astype(BF16), v_rows))
    a_ab = [jnp.where(head_a, aa_a[i][:c], pltpu.roll(aa_b[i][:c], n, axis=1)).astype(BF16) for i in range(ni)]
    a_rb = [jnp.where(head_a, aa_a[i][c:], pltpu.roll(aa_b[i][c:], n, axis=1)).astype(BF16) for i in range(ni)]

    tinv = [eye_f - jnp.where(diag2, a.astype(F32), 0.0) for a in a_ab]
    for li in range(len(levels)):
        tb = [t_.astype(BF16) for t_ in tinv]
        half = [_dot(tb[i], both(a_ab[i], lvl_a[li], lvl_b[li])).astype(BF16) for i in range(ni)]
        tinv = [tinv[i] - _dot(half[i], both(tb[i])) for i in range(ni)]

    w12 = [_dot(tinv[i].astype(BF16),
                jnp.concatenate([both(kkt[i].astype(BF16)), both(av[i][:c].astype(BF16))], axis=1))
           for i in range(ni)]
    w12_b = [w.astype(BF16) for w in w12]
    pq = [jnp.concatenate([rt[i], av[i][c:]], axis=1)
          - _dot(a_rb[i], jnp.concatenate([both(w12_b[i][:, :LANES]), both(w12_b[i][:, LANES:])], axis=1))
          for i in range(ni)]
    mg = []
    for i, ((bi, _), sl) in enumerate(zip(items, sls)):
        neg_bd_kd = jnp.concatenate([-pre[bi]["bd"][:, sl], pre[bi]["kd"][:, sl]], axis=0).astype(BF16)
        wv = jnp.concatenate([w12_b[i], jnp.concatenate([zeros_b, vh[i].astype(BF16)], axis=1)], axis=0)
        mg.append(_dot_tn(neg_bd_kd, wv))

    for i, ((bi, _), sl) in enumerate(zip(items, sls)):
        m_mat = jnp.where(block_diag, mg[i][:, :LANES], 0.0) + jnp.where(eye2, pre[bi]["e_tot"][:, sl], 0.0)
        pm = jnp.concatenate([pq[i][:, :LANES], m_mat], axis=0).astype(BF16)
        out = _dot(pm, state_ref[i].astype(BF16))
        y_ref[bi, :, sl] = out[:c] + pq[i][:, LANES:]
        state_ref[i] = out[c:] + jnp.where(block_diag, mg[i][:, LANES:], 0.0)


def _merge_kernel(x_ref, y_ref, gbv_ref, bc_ref, g_ref, lnw_ref, lnb_ref, wg_ref, bg_ref,
                  wr_ref, wp_ref, wm_ref, wo_ref, out_ref):
    x = x_ref[...]
    h = _rms_norm(x, g_ref[...]).astype(BF16)
    d = D_MODEL
    z = [_dot(h, wg_ref[:, i * d:(i + 1) * d]) for i in range(3)]
    up_b = _dot(bc_ref[:, :POOL_DIM].astype(BF16), wp_ref[...])
    up_c = _dot(bc_ref[:, POOL_DIM:].astype(BF16), wm_ref[...])
    inv_n = 1.0 / HEAD_DIM
    up_a = None
    half = RWKV_DIM // 2
    for c0 in (0, half):
        cs = slice(c0, c0 + half)
        y = y_ref[:, cs]
        yc = y - _head_lane_sums(y) * inv_n
        var = _head_lane_sums(yc * yc) * inv_n
        bonus = gbv_ref[:, RWKV_DIM + c0:RWKV_DIM + c0 + half]
        ya = (yc * lax.rsqrt(var + GN_EPS) * lnw_ref[:, cs] + lnb_ref[:, cs] + bonus) * gbv_ref[:, cs]
        part = _dot(ya.astype(BF16), wr_ref[cs, :])
        up_a = part if up_a is None else up_a + part
    gate = [_sigmoid(z[i] + bg_ref[:, i * d:(i + 1) * d]) for i in range(3)]
    merged = gate[0] * up_a + gate[1] * up_b + gate[2] * up_c
    out_ref[...] = x + _dot(merged.astype(BF16), wo_ref[...])


def _conv_ffn_kernel(x_ref, g_ref, win_ref, cw_ref, cb_ref, wout_ref, gf_ref, out_ref, tail_ref, act_ref):
    t = pl.program_id(1)
    tm = x_ref.shape[0]
    x = x_ref[...]
    h = _rms_norm(x, g_ref[...]).astype(BF16)
    row = lax.broadcasted_iota(jnp.int32, (tm, 1), 0)
    first = t == 0
    n_chunks = D_FF // FF_CHUNK

    def up_proj(j):
        return (_dot(h, win_ref[:, j * FF_CHUNK:(j + 1) * FF_CHUNK]),
                _dot(h, win_ref[:, D_FF + j * FF_CHUNK:D_FF + (j + 1) * FF_CHUNK]))

    nxt = up_proj(0)
    for j in range(n_chunks):
        cs = slice(j * FF_CHUNK, (j + 1) * FF_CHUNK)
        u, gv = nxt
        if j + 1 < n_chunks:
            nxt = up_proj(j + 1)
        tail = jnp.where(first, 0.0, tail_ref[:, cs])
        tail_ref[:, cs] = u[tm - 8:, :]
        u1 = jnp.where(row == 0, tail[7:8, :], pltpu.roll(u, 1, axis=0))
        u2 = jnp.where(row == 0, tail[6:7, :], jnp.where(row == 1, tail[7:8, :], pltpu.roll(u, 2, axis=0)))
        uc = cw_ref[0:1, cs] * u2 + cw_ref[1:2, cs] * u1 + cw_ref[2:3, cs] * u + cb_ref[:, cs]
        act = 0.5 * uc * (1.0 + lax.erf(uc * (2.0 ** -0.5))) * gv
        act_ref[:, cs] = act.astype(BF16)
    out_ref[...] = _rms_norm(x + _dot(act_ref[...], wout_ref[...]), gf_ref[...])


def _cast_block_rows(rows, max_steps):
    br = BF16_SUBLANES
    while rows % br or rows // br > max_steps:
        br += BF16_SUBLANES
    return br


def _const_spec(shape):
    nd = len(shape)
    return pl.BlockSpec(shape, lambda *_: (0,) * nd)


def kernel(x, mem, norm_mix_g, w_in_mix, mu_shift, w0, w_lora_b, a0, a_lora_b, g_lora_b, k_k, k_a, r_k,
           ln_x_w, ln_x_b, pool_w, pool_scale, norm_mem_g, w_mem_kv, w_up_rwkv, w_up_pool, w_up_mem,
           w_gate, b_gate, w_o, norm_ffn_g, w_ffn_in, ffn_conv_w, ffn_conv_b, w_ffn_out, norm_final_g):
    bsz, seq, d = x.shape
    n_mem = mem.shape[1]
    assert d == D_MODEL and seq % TM_PROJ == 0 and seq % CHUNK == 0 and bsz % SCAN_BATCH == 0
    assert seq % TM_FFN == 0 and (bsz * seq) % TM_MERGE == 0
    assert norm_mix_g.shape[0] == 1, "single-layer block"
    l = 0
    row = lambda a: a.reshape(1, -1).astype(F32)

    wmix = w_in_mix[l].astype(BF16)
    zpad = jnp.zeros((DECAY_LORA, RWKV_DIM), F32)
    wl_pad = jnp.concatenate([w_lora_b[l], zpad], axis=0).astype(BF16)
    al_pad = jnp.concatenate([zpad, a_lora_b[l]], axis=0).astype(BF16)
    gl = g_lora_b[l].astype(BF16)
    pool_bd = jax.scipy.linalg.block_diag(*[pool_w[l, i] for i in range(len(POOL_WINDOWS))]).astype(BF16)

    kv = pl.pallas_call(
        _mem_kv_kernel,
        out_shape=jax.ShapeDtypeStruct((bsz, n_mem, 2 * MEM_DIM), BF16),
        grid=(bsz,),
        in_specs=[pl.BlockSpec((None, n_mem, d), lambda b: (b, 0, 0)),
                  _const_spec((1, d)), _const_spec((d, 2 * MEM_DIM))],
        out_specs=pl.BlockSpec((None, n_mem, 2 * MEM_DIM), lambda b: (b, 0, 0)),
        compiler_params=pltpu.CompilerParams(dimension_semantics=("arbitrary",)),
        name="mem_kv",
    )(mem, row(norm_mem_g[l]), w_mem_kv[l].astype(BF16))

    tok = lambda w: pl.BlockSpec((None, TM_PROJ, w), lambda b, t: (b, t, 0))
    f32_out = lambda w: jax.ShapeDtypeStruct((bsz, seq, w), F32)
    outs = pl.pallas_call(
        _mix_proj_kernel,
        out_shape=[f32_out(PACK_SLOTS * RWKV_DIM), f32_out(POOL_DIM + MEM_DIM)],
        grid=(bsz, seq // TM_PROJ),
        in_specs=[tok(d), _const_spec((1, d)), _const_spec((d, MIX_IN)), _const_spec((1, RWKV_IN)),
                  _const_spec((1, RWKV_DIM)), _const_spec((DECAY_LORA + ICLR_LORA, RWKV_DIM)),
                  _const_spec((1, RWKV_DIM)), _const_spec((DECAY_LORA + ICLR_LORA, RWKV_DIM)),
                  _const_spec((GATE_LORA, RWKV_DIM)), _const_spec((1, RWKV_DIM)), _const_spec((1, RWKV_DIM)),
                  _const_spec((1, RWKV_DIM)), _const_spec((POOL_DIM, POOL_DIM)),
                  _const_spec((1, POOL_DIM)),
                  pl.BlockSpec((None, n_mem, 2 * MEM_DIM), lambda b, t: (b, 0, 0))],
        out_specs=[tok(PACK_SLOTS * RWKV_DIM), tok(POOL_DIM + MEM_DIM)],
        scratch_shapes=[pltpu.VMEM((1, RWKV_IN), F32), pltpu.VMEM((POOL_HALO, POOL_DIM), F32)],
        compiler_params=pltpu.CompilerParams(dimension_semantics=("arbitrary", "arbitrary"),
                                             vmem_limit_bytes=V7X_VMEM_LIMIT_BYTES),
        name="mix_proj",
    )(x, row(norm_mix_g[l]), wmix, row(mu_shift[l]), row(w0[l]), wl_pad, row(a0[l]), al_pad, gl,
      row(k_k[l]), row(k_a[l]), row(r_k[l]), pool_bd, row(pool_scale[l]), kv)
    packed, y_bc = outs

    sb = SCAN_BATCH
    blk = pl.BlockSpec((sb, CHUNK, RWKV_DIM), lambda b, c: (b, c, 0))
    slot = lambda j: pl.BlockSpec((sb, CHUNK, RWKV_DIM), lambda b, c: (b, c, j))
    n_chunks = seq // CHUNK
    later_w = [w_gate[l], w_o[l], w_ffn_in[l], w_ffn_out[l]]
    cast_rows = [_cast_block_rows(w.shape[0], n_chunks) for w in later_w]
    cast_steps = tuple(w.shape[0] // br for w, br in zip(later_w, cast_rows))
    wblk = lambda w, br: pl.BlockSpec((br, w.shape[1]), lambda b, c: (jnp.minimum(c, w.shape[0] // br - 1), 0))
    scan_out = pl.pallas_call(
        functools.partial(_rwkv_scan_kernel, cast_steps=cast_steps, n_steps=n_chunks),
        out_shape=[f32_out(RWKV_DIM)] + [jax.ShapeDtypeStruct(w.shape, BF16) for w in later_w],
        grid=(bsz // sb, n_chunks),
        in_specs=([slot(j) for j in (PACK_R, PACK_K, PACK_V, PACK_KK, PACK_B, PACK_LW)]
                  + [wblk(w, br) for w, br in zip(later_w, cast_rows)]),
        out_specs=[blk] + [wblk(w, br) for w, br in zip(later_w, cast_rows)],
        scratch_shapes=[pltpu.VMEM((sb * RWKV_DIM // LANES, LANES, LANES), F32)],
        compiler_params=pltpu.CompilerParams(dimension_semantics=("arbitrary", "arbitrary")),
        name="rwkv_scan",
    )(*([packed] * 6), *later_w)
    y_a, wg_b, wo_b, wfi_b, wfo_b = scan_out

    n_tok = bsz * seq
    flat = lambda a: a.reshape(n_tok, a.shape[-1])
    tokm = lambda w: pl.BlockSpec((TM_MERGE, w), lambda i: (i, 0))
    wspec = lambda shape: pl.BlockSpec(shape, lambda i: (0, 0), pipeline_mode=pl.Buffered(1))
    x1 = pl.pallas_call(
        _merge_kernel,
        out_shape=jax.ShapeDtypeStruct((n_tok, d), F32),
        grid=(n_tok // TM_MERGE,),
        in_specs=[tokm(d), tokm(RWKV_DIM),
                  pl.BlockSpec((TM_MERGE, 2 * RWKV_DIM), lambda i: (i, PACK_G // 2)), tokm(POOL_DIM + MEM_DIM),
                  _const_spec((1, d)), _const_spec((1, RWKV_DIM)), _const_spec((1, RWKV_DIM)),
                  wspec((d, 3 * d)), _const_spec((1, 3 * d)), wspec((RWKV_DIM, d)),
                  wspec((POOL_DIM, d)), wspec((MEM_DIM, d)), wspec((d, d))],
        out_specs=tokm(d),
        compiler_params=pltpu.CompilerParams(dimension_semantics=("arbitrary",),
                                             vmem_limit_bytes=V7X_VMEM_LIMIT_BYTES),
        name="merge",
    )(flat(x), flat(y_a), flat(packed), flat(y_bc), row(norm_mix_g[l]),
      row(ln_x_w[l]), row(ln_x_b[l]), wg_b, row(b_gate[l]), w_up_rwkv[l].astype(BF16),
      w_up_pool[l].astype(BF16), w_up_mem[l].astype(BF16), wo_b)

    tokf = pl.BlockSpec((None, TM_FFN, d), lambda b, t: (b, t, 0))
    single = pl.Buffered(1)
    out = pl.pallas_call(
        _conv_ffn_kernel,
        out_shape=jax.ShapeDtypeStruct((bsz, seq, d), F32),
        grid=(bsz, seq // TM_FFN),
        in_specs=[tokf, _const_spec((1, d)),
                  pl.BlockSpec((d, 2 * D_FF), lambda b, t: (0, 0), pipeline_mode=single),
                  _const_spec((3, D_FF)), _const_spec((1, D_FF)),
                  pl.BlockSpec((D_FF, d), lambda b, t: (0, 0), pipeline_mode=single),
                  _const_spec((1, d))],
        out_specs=tokf,
        scratch_shapes=[pltpu.VMEM((8, D_FF), F32), pltpu.VMEM((TM_FFN, D_FF), BF16)],
        compiler_params=pltpu.CompilerParams(dimension_semantics=("arbitrary", "arbitrary"),
                                             vmem_limit_bytes=V7X_VMEM_LIMIT_BYTES),
        name="conv_ffn",
    )(x1.reshape(bsz, seq, d), row(norm_ffn_g[l]), wfi_b, ffn_conv_w[l].astype(F32),
      row(ffn_conv_b[l]), wfo_b, row(norm_final_g))
    return out
```

```python
import functools
import math

import jax
import jax.numpy as jnp
from jax import lax
from jax.experimental import pallas as pl
from jax.experimental.pallas import tpu as pltpu

F32 = jnp.float32
BF16 = jnp.bfloat16

D_MODEL = 1024
HEAD_DIM = 64
RWKV_HEADS = 8
RWKV_DIM = RWKV_HEADS * HEAD_DIM
DECAY_LORA = 64
ICLR_LORA = 64
GATE_LORA = 128
RWKV_IN = 3 * RWKV_DIM + DECAY_LORA + ICLR_LORA + GATE_LORA
POOL_WINDOWS = (2, 4, 8, 16)
POOL_GROUP_DIM = 64
POOL_DIM = len(POOL_WINDOWS) * POOL_GROUP_DIM
POOL_HALO = 16
MEM_HEADS = 4
MEM_DIM = MEM_HEADS * HEAD_DIM
MIX_IN = RWKV_IN + POOL_DIM + MEM_DIM
D_FF = 2816
NORM_EPS = 1e-6
GN_EPS = 64e-5
CHUNK = 64
FF_CHUNK = 1024
V7X_VMEM_LIMIT_BYTES = 56 * 1024 * 1024
LANES = 128
BF16_SUBLANES = 16
PACK_R, PACK_K, PACK_V, PACK_KK, PACK_B, PACK_LW, PACK_G, PACK_BV = range(8)
PACK_SLOTS = 8

TM_PROJ = 512
TM_MERGE = 512
TM_FFN = 1024
SCAN_BATCH = 8


def _dot(a, b):
    return jnp.dot(a, b, preferred_element_type=F32)


def _dot_nt(a, b):
    return lax.dot_general(a, b, (((1,), (1,)), ((), ())), preferred_element_type=F32)


def _dot_tn(a, b):
    return lax.dot_general(a, b, (((0,), (0,)), ((), ())), preferred_element_type=F32)


def _sigmoid(x):
    return 0.5 * jnp.tanh(0.5 * x) + 0.5


def _rms_norm(x, g):
    return x * lax.rsqrt(jnp.mean(x * x, axis=-1, keepdims=True) + NORM_EPS) * g


def _split3_dot_left(w_bf16, x):
    hi = x.astype(BF16)
    r1 = x - hi.astype(F32)
    mid = r1.astype(BF16)
    lo = (r1 - mid.astype(F32)).astype(BF16)
    return _dot(w_bf16, hi) + _dot(w_bf16, mid) + _dot(w_bf16, lo)


def _head_lane_sums(x):
    first_head = lax.broadcasted_iota(jnp.int32, (1, LANES), 1) < HEAD_DIM
    cols = []
    for j in range(0, x.shape[1], LANES):
        col = x[:, j:j + LANES]
        sa = jnp.sum(jnp.where(first_head, col, 0.0), axis=-1, keepdims=True)
        sb = jnp.sum(jnp.where(first_head, 0.0, col), axis=-1, keepdims=True)
        cols.append(jnp.where(first_head, sa, sb))
    return cols[0] if len(cols) == 1 else jnp.concatenate(cols, axis=1)


def _mem_kv_kernel(mem_ref, g_ref, w_ref, kv_ref):
    m = _rms_norm(mem_ref[...], g_ref[...])
    kv_ref[...] = _dot(m.astype(BF16), w_ref[...]).astype(BF16)


def _mix_proj_kernel(x_ref, g_ref, wmix_ref, mu_ref, w0_ref, wl_ref, a0_ref, al_ref, gl_ref,
                     kk_ref, ka_ref, rk_ref, poolw_ref, pools_ref, kv_ref,
                     pack_out, bc_out,
                     prev_ref, halo_ref):
    t = pl.program_id(1)
    tm = x_ref.shape[0]
    first = t == 0

    def put(slot, val):
        pack_out[:, slot * RWKV_DIM:(slot + 1) * RWKV_DIM] = val

    h = _rms_norm(x_ref[...], g_ref[...]).astype(BF16)
    row = lax.broadcasted_iota(jnp.int32, (tm, 1), 0)
    lo0 = 3 * RWKV_DIM

    def proj(c0, c1):
        return _dot(h, wmix_ref[:, c0:c1])

    def token_shift(ps, c0, c1):
        carry = jnp.where(first, 0.0, prev_ref[:, c0:c1])
        prev = jnp.where(row == 0, carry, pltpu.roll(ps, 1, axis=0))
        prev_ref[:, c0:c1] = ps[tm - 1:tm, :]
        return ps + (prev - ps) * mu_ref[:, c0:c1]

    p_pq = proj(RWKV_IN, MIX_IN)
    p_lo = proj(lo0, RWKV_IN)
    p_r = proj(0, RWKV_DIM)
    pp = p_pq[:, :POOL_DIM]
    q = p_pq[:, POOL_DIM:]

    kmem = kv_ref[:, :MEM_DIM]
    vmem = kv_ref[:, MEM_DIM:]
    lane_head = lax.broadcasted_iota(jnp.int32, (1, MEM_DIM), 1) // HEAD_DIM
    scores = [_dot_nt(jnp.where(lane_head == hd, q, 0.0).astype(BF16), kmem) * (HEAD_DIM ** -0.5)
              for hd in range(MEM_HEADS)]

    p_k = proj(RWKV_DIM, 2 * RWKV_DIM)
    p_v = proj(2 * RWKV_DIM, lo0)

    lo = token_shift(p_lo, lo0, RWKV_IN)
    wa = lo[:, :DECAY_LORA + ICLR_LORA]
    gd = lo[:, DECAY_LORA + ICLR_LORA:]
    z = w0_ref[...] + _dot(jnp.tanh(wa).astype(BF16), wl_ref[...])
    put(PACK_LW, (-math.exp(-0.5)) * _sigmoid(z))
    a = _sigmoid(a0_ref[...] + _dot(wa.astype(BF16), al_ref[...]))
    put(PACK_G, _dot(_sigmoid(gd).astype(BF16), gl_ref[...]))

    halo = jnp.where(first, 0.0, halo_ref[...])
    halo_ref[...] = pp[tm - POOL_HALO:, :]
    ext = jnp.concatenate([halo, pp], axis=0)
    lane_group = lax.broadcasted_iota(jnp.int32, (1, POOL_DIM), 1) // POOL_GROUP_DIM
    win = jnp.zeros((1, POOL_DIM), jnp.int32)
    sel = jnp.zeros((tm, POOL_DIM), F32)
    acc = ext
    shift = 1
    for gi, w in enumerate(POOL_WINDOWS):
        while shift < w:
            acc = acc + pltpu.roll(acc, shift, axis=0)
            shift *= 2
        sel = jnp.where(lane_group == gi, acc[POOL_HALO:, :], sel)
        win = jnp.where(lane_group == gi, w, win)
    pos = t * tm + row + 1
    cnt = jnp.minimum(pos, win).astype(F32)
    dpool = sel / cnt - pp
    bc_out[:, :POOL_DIM] = _dot(dpool.astype(BF16), poolw_ref[...]) * pools_ref[...]

    cols = []
    for j in range(0, MEM_DIM, LANES):
        col = None
        for hd in range(j // HEAD_DIM, (j + LANES) // HEAD_DIM):
            s = scores[hd]
            e = jnp.exp(s - jnp.max(s, axis=-1, keepdims=True))
            prob = e / jnp.sum(e, axis=-1, keepdims=True)
            vh = jnp.where((lane_head == hd)[:, j:j + LANES], vmem[:, j:j + LANES], 0.0)
            o = _dot(prob.astype(BF16), vh)
            col = o if col is None else col + o
        cols.append(col)
    bc_out[:, POOL_DIM:] = jnp.concatenate(cols, axis=1)

    r = token_shift(p_r, 0, RWKV_DIM)
    k = token_shift(p_k, RWKV_DIM, 2 * RWKV_DIM)
    v = token_shift(p_v, 2 * RWKV_DIM, lo0)
    kkr = k * kk_ref[...]
    kkn = kkr * lax.rsqrt(_head_lane_sums(kkr * kkr) + 1e-12)
    kf = k * (1.0 + (a - 1.0) * ka_ref[...])
    put(PACK_R, r)
    put(PACK_K, kf)
    put(PACK_V, v)
    put(PACK_KK, kkn)
    put(PACK_B, kkn * a)
    put(PACK_BV, _head_lane_sums(r * kf * rk_ref[...]) * v)


def _chunk_masks():
    c = CHUNK
    ri = lax.broadcasted_iota(jnp.int32, (2 * c, 2 * c), 0)
    ci = lax.broadcasted_iota(jnp.int32, (2 * c, 2 * c), 1)
    rt = ri % c
    ct = ci % c
    aa_mask = (rt > ct) | ((ri >= c) & (rt == ct))
    r64 = lax.broadcasted_iota(jnp.int32, (c, 2 * c), 0)
    c64 = lax.broadcasted_iota(jnp.int32, (c, 2 * c), 1) % c
    eye = r64 == c64
    levels = []
    blk = 2
    while blk < c:
        same_outer = (r64 // (2 * blk)) == (c64 // (2 * blk))
        diff_inner = (r64 // blk) != (c64 // blk)
        levels.append(same_outer & diff_inner & (r64 > c64))
        blk *= 2
    diag2 = ((r64 // 2) == (c64 // 2)) & (r64 > c64)
    tri_incl = lax.broadcasted_iota(jnp.int32, (c, c), 0) >= lax.broadcasted_iota(jnp.int32, (c, c), 1)
    block_diag = (ri < c) == (ci < c)
    eye2 = ri == ci
    return aa_mask, eye, diag2, levels, tri_incl, block_diag, eye2


def _rwkv_scan_kernel(*refs, cast_steps, n_steps):
    r_ref, k_ref, v_ref, kk_ref, b_ref, lw_ref = refs[:6]
    nw = len(cast_steps)
    w_refs = refs[6:6 + nw]
    y_ref = refs[6 + nw]
    wb_refs = refs[7 + nw:7 + 2 * nw]
    state_ref = refs[7 + 2 * nw]
    c = CHUNK
    n = HEAD_DIM
    chunk_idx = pl.program_id(1)

    for w_ref, wb_ref, steps in zip(w_refs, wb_refs, cast_steps):
        if steps == n_steps:
            wb_ref[...] = w_ref[...].astype(BF16)
        else:
            @pl.when(chunk_idx < steps)
            def _(w_ref=w_ref, wb_ref=wb_ref):
                wb_ref[...] = w_ref[...].astype(BF16)

    @pl.when(chunk_idx == 0)
    def _():
        state_ref[...] = jnp.zeros_like(state_ref)

    aa_mask, eye, diag2, levels, tri_incl, block_diag, eye2 = _chunk_masks()
    tri = jnp.where(tri_incl, 1.0, 0.0).astype(BF16)
    eye_f = jnp.where(eye, 1.0, 0.0)
    zeros_b = jnp.zeros((c, LANES), BF16)
    head_a = lax.broadcasted_iota(jnp.int32, (c, LANES), 1) < n
    only_a = jnp.where(head_a, 1.0, 0.0).astype(BF16)
    only_b = jnp.where(head_a, 0.0, 1.0).astype(BF16)
    only_a2 = jnp.concatenate([only_a, only_a], axis=0)
    only_b2 = jnp.concatenate([only_b, only_b], axis=0)
    lvl_a = [jnp.where(lvl & head_a, 1.0, 0.0).astype(BF16) for lvl in levels]
    lvl_b = [jnp.where(lvl & ~head_a, 1.0, 0.0).astype(BF16) for lvl in levels]

    def both(xb, ma=only_a, mb=only_b):
        return jnp.concatenate([xb * ma, xb * mb], axis=0)

    nb = r_ref.shape[0]
    items = [(bi, j) for bi in range(nb) for j in range(RWKV_DIM // LANES)]
    sls = [slice(j * LANES, (j + 1) * LANES) for _, j in items]
    ni = len(items)

    pre = []
    for bi in range(nb):
        lw = lw_ref[bi]
        cum = _split3_dot_left(tri, lw)
        tot = cum[c - 1:c, :]
        e_inv = jnp.exp(-cum)
        e_end = jnp.exp(tot - cum)
        r_all = r_ref[bi]
        k_all = k_ref[bi]
        b_all = b_ref[bi]
        pre.append(dict(
            rt=r_all * jnp.exp(cum), kkt=kk_ref[bi] * jnp.exp(cum - lw), bt=b_all * e_inv, kt=k_all * e_inv,
            bd=b_all * e_end, kd=k_all * e_end, e_tot=jnp.exp(tot), v=v_ref[bi]))

    rt = [pre[bi]["rt"][:, sl] for (bi, _), sl in zip(items, sls)]
    kkt = [pre[bi]["kkt"][:, sl] for (bi, _), sl in zip(items, sls)]
    vh = [pre[bi]["v"][:, sl] for (bi, _), sl in zip(items, sls)]
    aa_a, aa_b = [], []
    for i, ((bi, _), sl) in enumerate(zip(items, sls)):
        lhs = jnp.concatenate([kkt[i], rt[i]], axis=0).astype(BF16)
        rhs = jnp.concatenate([pre[bi]["bt"][:, sl], pre[bi]["kt"][:, sl]], axis=0).astype(BF16)
        aa2 = _dot_nt(jnp.concatenate([lhs * only_a2, lhs * only_b2], axis=0), rhs)
        aa_a.append(jnp.where(aa_mask, aa2[:2 * c], 0.0))
        aa_b.append(jnp.where(aa_mask, aa2[2 * c:], 0.0))
    av = []
    for i in range(ni):
        vb = vh[i].astype(BF16)
        v_rows = jnp.concatenate([zeros_b, vb * only_a, zeros_b, vb * only_b], axis=0)
        av.append(_dot(jnp.concatenate([aa_a[i], aa_b[i]], axis=1).astype(BF16), v_rows))
    a_ab = [jnp.where(head_a, aa_a[i][:c], pltpu.roll(aa_b[i][:c], n, axis=1)).astype(BF16) for i in range(ni)]
    a_rb = [jnp.where(head_a, aa_a[i][c:], pltpu.roll(aa_b[i][c:], n, axis=1)).astype(BF16) for i in range(ni)]

    tinv = [eye_f - jnp.where(diag2, a.astype(F32), 0.0) for a in a_ab]
    for li in range(len(levels)):
        tb = [t_.astype(BF16) for t_ in tinv]
        half = [_dot(tb[i], both(a_ab[i], lvl_a[li], lvl_b[li])).astype(BF16) for i in range(ni)]
        tinv = [tinv[i] - _dot(half[i], both(tb[i])) for i in range(ni)]

    w12 = [_dot(tinv[i].astype(BF16),
                jnp.concatenate([both(kkt[i].astype(BF16)), both(av[i][:c].astype(BF16))], axis=1))
           for i in range(ni)]
    w12_b = [w.astype(BF16) for w in w12]
    pq = [jnp.concatenate([rt[i], av[i][c:]], axis=1)
          - _dot(a_rb[i], jnp.concatenate([both(w12_b[i][:, :LANES]), both(w12_b[i][:, LANES:])], axis=1))
          for i in range(ni)]
    mg = []
    for i, ((bi, _), sl) in enumerate(zip(items, sls)):
        neg_bd_kd = jnp.concatenate([-pre[bi]["bd"][:, sl], pre[bi]["kd"][:, sl]], axis=0).astype(BF16)
        wv = jnp.concatenate([w12_b[i], jnp.concatenate([zeros_b, vh[i].astype(BF16)], axis=1)], axis=0)
        mg.append(_dot_tn(neg_bd_kd, wv))

    for i, ((bi, _), sl) in enumerate(zip(items, sls)):
        m_mat = jnp.where(block_diag, mg[i][:, :LANES], 0.0) + jnp.where(eye2, pre[bi]["e_tot"][:, sl], 0.0)
        pm = jnp.concatenate([pq[i][:, :LANES], m_mat], axis=0).astype(BF16)
        out = _dot(pm, state_ref[i].astype(BF16))
        y_ref[bi, :, sl] = out[:c] + pq[i][:, LANES:]
        state_ref[i] = out[c:] + jnp.where(block_diag, mg[i][:, LANES:], 0.0)


def _merge_kernel(x_ref, y_ref, gbv_ref, bc_ref, g_ref, lnw_ref, lnb_ref, wg_ref, bg_ref,
                  wr_ref, wp_ref, wm_ref, wo_ref, out_ref):
    x = x_ref[...]
    h = _rms_norm(x, g_ref[...]).astype(BF16)
    d = D_MODEL
    z = [_dot(h, wg_ref[:, i * d:(i + 1) * d]) for i in range(3)]
    up_b = _dot(bc_ref[:, :POOL_DIM].astype(BF16), wp_ref[...])
    up_c = _dot(bc_ref[:, POOL_DIM:].astype(BF16), wm_ref[...])
    inv_n = 1.0 / HEAD_DIM
    up_a = None
    half = RWKV_DIM // 2
    for c0 in (0, half):
        cs = slice(c0, c0 + half)
        y = y_ref[:, cs]
        yc = y - _head_lane_sums(y) * inv_n
        var = _head_lane_sums(yc * yc) * inv_n
        bonus = gbv_ref[:, RWKV_DIM + c0:RWKV_DIM + c0 + half]
        ya = (yc * lax.rsqrt(var + GN_EPS) * lnw_ref[:, cs] + lnb_ref[:, cs] + bonus) * gbv_ref[:, cs]
        part = _dot(ya.astype(BF16), wr_ref[cs, :])
        up_a = part if up_a is None else up_a + part
    gate = [_sigmoid(z[i] + bg_ref[:, i * d:(i + 1) * d]) for i in range(3)]
    merged = gate[0] * up_a + gate[1] * up_b + gate[2] * up_c
    out_ref[...] = x + _dot(merged.astype(BF16), wo_ref[...])


def _conv_ffn_kernel(x_ref, g_ref, win_ref, cw_ref, cb_ref, wout_ref, gf_ref, out_ref, tail_ref, act_ref):
    t = pl.program_id(1)
    tm = x_ref.shape[0]
    x = x_ref[...]
    h = _rms_norm(x, g_ref[...]).astype(BF16)
    row = lax.broadcasted_iota(jnp.int32, (tm, 1), 0)
    first = t == 0
    bounds = list(range(0, D_FF, FF_CHUNK)) + [D_FF]
    n_chunks = len(bounds) - 1

    def up_proj(j):
        return (_dot(h, win_ref[:, bounds[j]:bounds[j + 1]]),
                _dot(h, win_ref[:, D_FF + bounds[j]:D_FF + bounds[j + 1]]))

    nxt = up_proj(0)
    for j in range(n_chunks):
        cs = slice(bounds[j], bounds[j + 1])
        u, gv = nxt
        if j + 1 < n_chunks:
            nxt = up_proj(j + 1)
        tail = jnp.where(first, 0.0, tail_ref[:, cs])
        tail_ref[:, cs] = u[tm - 8:, :]
        u1 = jnp.where(row == 0, tail[7:8, :], pltpu.roll(u, 1, axis=0))
        u2 = jnp.where(row == 0, tail[6:7, :], jnp.where(row == 1, tail[7:8, :], pltpu.roll(u, 2, axis=0)))
        uc = cw_ref[0:1, cs] * u2 + cw_ref[1:2, cs] * u1 + cw_ref[2:3, cs] * u + cb_ref[:, cs]
        act = 0.5 * uc * (1.0 + lax.erf(uc * (2.0 ** -0.5))) * gv
        act_ref[:, cs] = act.astype(BF16)
    out_ref[...] = _rms_norm(x + _dot(act_ref[...], wout_ref[...]), gf_ref[...])


def _cast_block_rows(rows, max_steps):
    br = BF16_SUBLANES
    while rows % br or rows // br > max_steps:
        br += BF16_SUBLANES
    return br


def _const_spec(shape):
    nd = len(shape)
    return pl.BlockSpec(shape, lambda *_: (0,) * nd)


def kernel(x, mem, norm_mix_g, w_in_mix, mu_shift, w0, w_lora_b, a0, a_lora_b, g_lora_b, k_k, k_a, r_k,
           ln_x_w, ln_x_b, pool_w, pool_scale, norm_mem_g, w_mem_kv, w_up_rwkv, w_up_pool, w_up_mem,
           w_gate, b_gate, w_o, norm_ffn_g, w_ffn_in, ffn_conv_w, ffn_conv_b, w_ffn_out, norm_final_g):
    bsz, seq, d = x.shape
    n_mem = mem.shape[1]
    assert d == D_MODEL and seq % TM_PROJ == 0 and seq % CHUNK == 0 and bsz % SCAN_BATCH == 0
    assert seq % TM_FFN == 0 and (bsz * seq) % TM_MERGE == 0
    assert norm_mix_g.shape[0] == 1, "single-layer block"
    l = 0
    row = lambda a: a.reshape(1, -1).astype(F32)

    wmix = w_in_mix[l].astype(BF16)
    zpad = jnp.zeros((DECAY_LORA, RWKV_DIM), F32)
    wl_pad = jnp.concatenate([w_lora_b[l], zpad], axis=0).astype(BF16)
    al_pad = jnp.concatenate([zpad, a_lora_b[l]], axis=0).astype(BF16)
    gl = g_lora_b[l].astype(BF16)
    pool_bd = jax.scipy.linalg.block_diag(*[pool_w[l, i] for i in range(len(POOL_WINDOWS))]).astype(BF16)

    kv = pl.pallas_call(
        _mem_kv_kernel,
        out_shape=jax.ShapeDtypeStruct((bsz, n_mem, 2 * MEM_DIM), BF16),
        grid=(bsz,),
        in_specs=[pl.BlockSpec((None, n_mem, d), lambda b: (b, 0, 0)),
                  _const_spec((1, d)), _const_spec((d, 2 * MEM_DIM))],
        out_specs=pl.BlockSpec((None, n_mem, 2 * MEM_DIM), lambda b: (b, 0, 0)),
        compiler_params=pltpu.CompilerParams(dimension_semantics=("arbitrary",)),
        name="mem_kv",
    )(mem, row(norm_mem_g[l]), w_mem_kv[l].astype(BF16))

    tok = lambda w: pl.BlockSpec((None, TM_PROJ, w), lambda b, t: (b, t, 0))
    f32_out = lambda w: jax.ShapeDtypeStruct((bsz, seq, w), F32)
    outs = pl.pallas_call(
        _mix_proj_kernel,
        out_shape=[f32_out(PACK_SLOTS * RWKV_DIM), f32_out(POOL_DIM + MEM_DIM)],
        grid=(bsz, seq // TM_PROJ),
        in_specs=[tok(d), _const_spec((1, d)), _const_spec((d, MIX_IN)), _const_spec((1, RWKV_IN)),
                  _const_spec((1, RWKV_DIM)), _const_spec((DECAY_LORA + ICLR_LORA, RWKV_DIM)),
                  _const_spec((1, RWKV_DIM)), _const_spec((DECAY_LORA + ICLR_LORA, RWKV_DIM)),
                  _const_spec((GATE_LORA, RWKV_DIM)), _const_spec((1, RWKV_DIM)), _const_spec((1, RWKV_DIM)),
                  _const_spec((1, RWKV_DIM)), _const_spec((POOL_DIM, POOL_DIM)),
                  _const_spec((1, POOL_DIM)),
                  pl.BlockSpec((None, n_mem, 2 * MEM_DIM), lambda b, t: (b, 0, 0))],
        out_specs=[tok(PACK_SLOTS * RWKV_DIM), tok(POOL_DIM + MEM_DIM)],
        scratch_shapes=[pltpu.VMEM((1, RWKV_IN), F32), pltpu.VMEM((POOL_HALO, POOL_DIM), F32)],
        compiler_params=pltpu.CompilerParams(dimension_semantics=("arbitrary", "arbitrary"),
                                             vmem_limit_bytes=V7X_VMEM_LIMIT_BYTES),
        name="mix_proj",
    )(x, row(norm_mix_g[l]), wmix, row(mu_shift[l]), row(w0[l]), wl_pad, row(a0[l]), al_pad, gl,
      row(k_k[l]), row(k_a[l]), row(r_k[l]), pool_bd, row(pool_scale[l]), kv)
    packed, y_bc = outs

    sb = SCAN_BATCH
    blk = pl.BlockSpec((sb, CHUNK, RWKV_DIM), lambda b, c: (b, c, 0))
    slot = lambda j: pl.BlockSpec((sb, CHUNK, RWKV_DIM), lambda b, c: (b, c, j))
    n_chunks = seq // CHUNK
    later_w = [w_gate[l], w_o[l], w_ffn_in[l], w_ffn_out[l]]
    cast_rows = [_cast_block_rows(w.shape[0], n_chunks) for w in later_w]
    cast_steps = tuple(w.shape[0] // br for w, br in zip(later_w, cast_rows))
    wblk = lambda w, br: pl.BlockSpec((br, w.shape[1]), lambda b, c: (jnp.minimum(c, w.shape[0] // br - 1), 0))
    scan_out = pl.pallas_call(
        functools.partial(_rwkv_scan_kernel, cast_steps=cast_steps, n_steps=n_chunks),
        out_shape=[f32_out(RWKV_DIM)] + [jax.ShapeDtypeStruct(w.shape, BF16) for w in later_w],
        grid=(bsz // sb, n_chunks),
        in_specs=([slot(j) for j in (PACK_R, PACK_K, PACK_V, PACK_KK, PACK_B, PACK_LW)]
                  + [wblk(w, br) for w, br in zip(later_w, cast_rows)]),
        out_specs=[blk] + [wblk(w, br) for w, br in zip(later_w, cast_rows)],
        scratch_shapes=[pltpu.VMEM((sb * RWKV_DIM // LANES, LANES, LANES), F32)],
        compiler_params=pltpu.CompilerParams(dimension_semantics=("arbitrary", "arbitrary")),
        name="rwkv_scan",
    )(*([packed] * 6), *later_w)
    y_a, wg_b, wo_b, wfi_b, wfo_b = scan_out

    n_tok = bsz * seq
    flat = lambda a: a.reshape(n_tok, a.shape[-1])
    tokm = lambda w: pl.BlockSpec((TM_MERGE, w), lambda i: (i, 0))
    wspec = lambda shape: pl.BlockSpec(shape, lambda i: (0, 0), pipeline_mode=pl.Buffered(1))
    x1 = pl.pallas_call(
        _merge_kernel,
        out_shape=jax.ShapeDtypeStruct((n_tok, d), F32),
        grid=(n_tok // TM_MERGE,),
        in_specs=[tokm(d), tokm(RWKV_DIM),
                  pl.BlockSpec((TM_MERGE, 2 * RWKV_DIM), lambda i: (i, PACK_G // 2)), tokm(POOL_DIM + MEM_DIM),
                  _const_spec((1, d)), _const_spec((1, RWKV_DIM)), _const_spec((1, RWKV_DIM)),
                  wspec((d, 3 * d)), _const_spec((1, 3 * d)), wspec((RWKV_DIM, d)),
                  wspec((POOL_DIM, d)), wspec((MEM_DIM, d)), wspec((d, d))],
        out_specs=tokm(d),
        compiler_params=pltpu.CompilerParams(dimension_semantics=("arbitrary",),
                                             vmem_limit_bytes=V7X_VMEM_LIMIT_BYTES),
        name="merge",
    )(flat(x), flat(y_a), flat(packed), flat(y_bc), row(norm_mix_g[l]),
      row(ln_x_w[l]), row(ln_x_b[l]), wg_b, row(b_gate[l]), w_up_rwkv[l].astype(BF16),
      w_up_pool[l].astype(BF16), w_up_mem[l].astype(BF16), wo_b)

    tokf = pl.BlockSpec((None, TM_FFN, d), lambda b, t: (b, t, 0))
    single = pl.Buffered(1)
    out = pl.pallas_call(
        _conv_ffn_kernel,
        out_shape=jax.ShapeDtypeStruct((bsz, seq, d), F32),
        grid=(bsz, seq // TM_FFN),
        in_specs=[tokf, _const_spec((1, d)),
                  pl.BlockSpec((d, 2 * D_FF), lambda b, t: (0, 0), pipeline_mode=single),
                  _const_spec((3, D_FF)), _const_spec((1, D_FF)),
                  pl.BlockSpec((D_FF, d), lambda b, t: (0, 0), pipeline_mode=single),
                  _const_spec((1, d))],
        out_specs=tokf,
        scratch_shapes=[pltpu.VMEM((8, D_FF), F32), pltpu.VMEM((TM_FFN, D_FF), BF16)],
        compiler_params=pltpu.CompilerParams(dimension_semantics=("arbitrary", "arbitrary"),
                                             vmem_limit_bytes=V7X_VMEM_LIMIT_BYTES),
        name="conv_ffn",
    )(x1.reshape(bsz, seq, d), row(norm_ffn_g[l]), wfi_b, ffn_conv_w[l].astype(F32),
      row(ffn_conv_b[l]), wfo_b, row(norm_final_g))
    return out
```

```python
import functools
import math

import jax
import jax.numpy as jnp
from jax import lax
from jax.experimental import pallas as pl
from jax.experimental.pallas import tpu as pltpu

F32 = jnp.float32
BF16 = jnp.bfloat16

D_MODEL = 1024
HEAD_DIM = 64
RWKV_HEADS = 8
RWKV_DIM = RWKV_HEADS * HEAD_DIM
DECAY_LORA = 64
ICLR_LORA = 64
GATE_LORA = 128
RWKV_IN = 3 * RWKV_DIM + DECAY_LORA + ICLR_LORA + GATE_LORA
POOL_WINDOWS = (2, 4, 8, 16)
POOL_GROUP_DIM = 64
POOL_DIM = len(POOL_WINDOWS) * POOL_GROUP_DIM
POOL_HALO = 16
MEM_HEADS = 4
MEM_DIM = MEM_HEADS * HEAD_DIM
MIX_IN = RWKV_IN + POOL_DIM + MEM_DIM
D_FF = 2816
NORM_EPS = 1e-6
GN_EPS = 64e-5
CHUNK = 64
FF_CHUNK = 1024
V7X_VMEM_LIMIT_BYTES = 56 * 1024 * 1024
LANES = 128
BF16_SUBLANES = 16
PACK_R, PACK_K, PACK_V, PACK_KK, PACK_B, PACK_LW, PACK_G, PACK_BV = range(8)
PACK_SLOTS = 8

TM_PROJ = 512
TM_MERGE = 512
TM_FFN = 1024
SCAN_BATCH = 8


def _dot(a, b):
    return jnp.dot(a, b, preferred_element_type=F32)


def _dot_nt(a, b):
    return lax.dot_general(a, b, (((1,), (1,)), ((), ())), preferred_element_type=F32)


def _dot_tn(a, b):
    return lax.dot_general(a, b, (((0,), (0,)), ((), ())), preferred_element_type=F32)


def _sigmoid(x):
    return 0.5 * jnp.tanh(0.5 * x) + 0.5


def _rms_norm(x, g):
    return x * lax.rsqrt(jnp.mean(x * x, axis=-1, keepdims=True) + NORM_EPS) * g


def _split3_dot_left(w_bf16, x):
    hi = x.astype(BF16)
    r1 = x - hi.astype(F32)
    mid = r1.astype(BF16)
    lo = (r1 - mid.astype(F32)).astype(BF16)
    return _dot(w_bf16, hi) + _dot(w_bf16, mid) + _dot(w_bf16, lo)


def _head_lane_sums(x):
    first_head = lax.broadcasted_iota(jnp.int32, (1, LANES), 1) < HEAD_DIM
    cols = []
    for j in range(0, x.shape[1], LANES):
        col = x[:, j:j + LANES]
        sa = jnp.sum(jnp.where(first_head, col, 0.0), axis=-1, keepdims=True)
        sb = jnp.sum(jnp.where(first_head, 0.0, col), axis=-1, keepdims=True)
        cols.append(jnp.where(first_head, sa, sb))
    return cols[0] if len(cols) == 1 else jnp.concatenate(cols, axis=1)


def _mem_kv_kernel(mem_ref, g_ref, w_ref, kv_ref):
    m = _rms_norm(mem_ref[...], g_ref[...])
    kv_ref[...] = _dot(m.astype(BF16), w_ref[...]).astype(BF16)


def _mix_proj_kernel(x_ref, g_ref, wmix_ref, mu_ref, w0_ref, wl_ref, a0_ref, al_ref, gl_ref,
                     kk_ref, ka_ref, rk_ref, poolw_ref, pools_ref, kv_ref,
                     pack_out, bc_out,
                     prev_ref, halo_ref):
    t = pl.program_id(1)
    tm = x_ref.shape[0]
    first = t == 0

    def put(slot, val):
        pack_out[:, slot * RWKV_DIM:(slot + 1) * RWKV_DIM] = val

    h = _rms_norm(x_ref[...], g_ref[...]).astype(BF16)
    row = lax.broadcasted_iota(jnp.int32, (tm, 1), 0)
    lo0 = 3 * RWKV_DIM

    def proj(c0, c1):
        return _dot(h, wmix_ref[:, c0:c1])

    def token_shift(ps, c0, c1):
        carry = jnp.where(first, 0.0, prev_ref[:, c0:c1])
        prev = jnp.where(row == 0, carry, pltpu.roll(ps, 1, axis=0))
        prev_ref[:, c0:c1] = ps[tm - 1:tm, :]
        return ps + (prev - ps) * mu_ref[:, c0:c1]

    p_pq = proj(RWKV_IN, MIX_IN)
    p_lo = proj(lo0, RWKV_IN)
    p_r = proj(0, RWKV_DIM)
    pp = p_pq[:, :POOL_DIM]
    q = p_pq[:, POOL_DIM:]

    kmem = kv_ref[:, :MEM_DIM]
    vmem = kv_ref[:, MEM_DIM:]
    lane_head = lax.broadcasted_iota(jnp.int32, (1, MEM_DIM), 1) // HEAD_DIM
    scores = [_dot_nt(jnp.where(lane_head == hd, q, 0.0).astype(BF16), kmem) * (HEAD_DIM ** -0.5)
              for hd in range(MEM_HEADS)]

    p_k = proj(RWKV_DIM, 2 * RWKV_DIM)
    p_v = proj(2 * RWKV_DIM, lo0)

    lo = token_shift(p_lo, lo0, RWKV_IN)
    wa = lo[:, :DECAY_LORA + ICLR_LORA]
    gd = lo[:, DECAY_LORA + ICLR_LORA:]
    z = w0_ref[...] + _dot(jnp.tanh(wa).astype(BF16), wl_ref[...])
    put(PACK_LW, (-math.exp(-0.5)) * _sigmoid(z))
    a = _sigmoid(a0_ref[...] + _dot(wa.astype(BF16), al_ref[...]))
    put(PACK_G, _dot(_sigmoid(gd).astype(BF16), gl_ref[...]))

    halo = jnp.where(first, 0.0, halo_ref[...])
    halo_ref[...] = pp[tm - POOL_HALO:, :]
    ext = jnp.concatenate([halo, pp], axis=0)
    lane_group = lax.broadcasted_iota(jnp.int32, (1, POOL_DIM), 1) // POOL_GROUP_DIM
    win = jnp.zeros((1, POOL_DIM), jnp.int32)
    sel = jnp.zeros((tm, POOL_DIM), F32)
    acc = ext
    shift = 1
    for gi, w in enumerate(POOL_WINDOWS):
        while shift < w:
            acc = acc + pltpu.roll(acc, shift, axis=0)
            shift *= 2
        sel = jnp.where(lane_group == gi, acc[POOL_HALO:, :], sel)
        win = jnp.where(lane_group == gi, w, win)
    pos = t * tm + row + 1
    cnt = jnp.minimum(pos, win).astype(F32)
    dpool = sel / cnt - pp
    bc_out[:, :POOL_DIM] = _dot(dpool.astype(BF16), poolw_ref[...]) * pools_ref[...]

    cols = []
    for j in range(0, MEM_DIM, LANES):
        col = None
        for hd in range(j // HEAD_DIM, (j + LANES) // HEAD_DIM):
            s = scores[hd]
            e = jnp.exp(s - jnp.max(s, axis=-1, keepdims=True))
            prob = e / jnp.sum(e, axis=-1, keepdims=True)
            vh = jnp.where((lane_head == hd)[:, j:j + LANES], vmem[:, j:j + LANES], 0.0)
            o = _dot(prob.astype(BF16), vh)
            col = o if col is None else col + o
        cols.append(col)
    bc_out[:, POOL_DIM:] = jnp.concatenate(cols, axis=1)

    r = token_shift(p_r, 0, RWKV_DIM)
    k = token_shift(p_k, RWKV_DIM, 2 * RWKV_DIM)
    v = token_shift(p_v, 2 * RWKV_DIM, lo0)
    kkr = k * kk_ref[...]
    kkn = kkr * lax.rsqrt(_head_lane_sums(kkr * kkr) + 1e-12)
    kf = k * (1.0 + (a - 1.0) * ka_ref[...])
    put(PACK_R, r)
    put(PACK_K, kf)
    put(PACK_V, v)
    put(PACK_KK, kkn)
    put(PACK_B, kkn * a)
    put(PACK_BV, _head_lane_sums(r * kf * rk_ref[...]) * v)


def _chunk_masks():
    c = CHUNK
    ri = lax.broadcasted_iota(jnp.int32, (2 * c, 2 * c), 0)
    ci = lax.broadcasted_iota(jnp.int32, (2 * c, 2 * c), 1)
    rt = ri % c
    ct = ci % c
    aa_mask = (rt > ct) | ((ri >= c) & (rt == ct))
    r64 = lax.broadcasted_iota(jnp.int32, (c, 2 * c), 0)
    c64 = lax.broadcasted_iota(jnp.int32, (c, 2 * c), 1) % c
    eye = r64 == c64
    levels = []
    blk = 2
    while blk < c:
        same_outer = (r64 // (2 * blk)) == (c64 // (2 * blk))
        diff_inner = (r64 // blk) != (c64 // blk)
        levels.append(same_outer & diff_inner & (r64 > c64))
        blk *= 2
    diag2 = ((r64 // 2) == (c64 // 2)) & (r64 > c64)
    tri_incl = lax.broadcasted_iota(jnp.int32, (c, c), 0) >= lax.broadcasted_iota(jnp.int32, (c, c), 1)
    block_diag = (ri < c) == (ci < c)
    eye2 = ri == ci
    return aa_mask, eye, diag2, levels, tri_incl, block_diag, eye2


def _rwkv_scan_kernel(*refs, cast_steps, n_steps):
    r_ref, k_ref, v_ref, kk_ref, b_ref, lw_ref = refs[:6]
    nw = len(cast_steps)
    w_refs = refs[6:6 + nw]
    y_ref = refs[6 + nw]
    wb_refs = refs[7 + nw:7 + 2 * nw]
    state_ref = refs[7 + 2 * nw]
    c = CHUNK
    n = HEAD_DIM
    chunk_idx = pl.program_id(1)

    for w_ref, wb_ref, steps in zip(w_refs, wb_refs, cast_steps):
        if steps == n_steps:
            wb_ref[...] = w_ref[...].astype(BF16)
        else:
            @pl.when(chunk_idx < steps)
            def _(w_ref=w_ref, wb_ref=wb_ref):
                wb_ref[...] = w_ref[...].astype(BF16)

    @pl.when(chunk_idx == 0)
    def _():
        state_ref[...] = jnp.zeros_like(state_ref)

    aa_mask, eye, diag2, levels, tri_incl, block_diag, eye2 = _chunk_masks()
    tri = jnp.where(tri_incl, 1.0, 0.0).astype(BF16)
    eye_f = jnp.where(eye, 1.0, 0.0)
    zeros_b = jnp.zeros((c, LANES), BF16)
    head_a = lax.broadcasted_iota(jnp.int32, (c, LANES), 1) < n
    only_a = jnp.where(head_a, 1.0, 0.0).astype(BF16)
    only_b = jnp.where(head_a, 0.0, 1.0).astype(BF16)
    only_a2 = jnp.concatenate([only_a, only_a], axis=0)
    only_b2 = jnp.concatenate([only_b, only_b], axis=0)
    lvl_a = [jnp.where(lvl & head_a, 1.0, 0.0).astype(BF16) for lvl in levels]
    lvl_b = [jnp.where(lvl & ~head_a, 1.0, 0.0).astype(BF16) for lvl in levels]

    def both(xb, ma=only_a, mb=only_b):
        return jnp.concatenate([xb * ma, xb * mb], axis=0)

    nb = r_ref.shape[0]
    items = [(bi, j) for bi in range(nb) for j in range(RWKV_DIM // LANES)]
    sls = [slice(j * LANES, (j + 1) * LANES) for _, j in items]
    ni = len(items)

    pre = []
    for bi in range(nb):
        lw = lw_ref[bi]
        cum = _split3_dot_left(tri, lw)
        tot = cum[c - 1:c, :]
        e_inv = jnp.exp(-cum)
        e_tot = jnp.exp(tot)
        bt = b_ref[bi] * e_inv
        kt = k_ref[bi] * e_inv
        pre.append(dict(
            rt=r_ref[bi] * jnp.exp(cum), kkt=kk_ref[bi] * jnp.exp(cum - lw), bt=bt, kt=kt,
            bd=bt * e_tot, kd=kt * e_tot, e_tot=e_tot, v=v_ref[bi]))

    rt = [pre[bi]["rt"][:, sl] for (bi, _), sl in zip(items, sls)]
    kkt = [pre[bi]["kkt"][:, sl] for (bi, _), sl in zip(items, sls)]
    vh = [pre[bi]["v"][:, sl] for (bi, _), sl in zip(items, sls)]
    aa_a, aa_b = [], []
    for i, ((bi, _), sl) in enumerate(zip(items, sls)):
        lhs = jnp.concatenate([kkt[i], rt[i]], axis=0).astype(BF16)
        rhs = jnp.concatenate([pre[bi]["bt"][:, sl], pre[bi]["kt"][:, sl]], axis=0).astype(BF16)
        aa2 = _dot_nt(jnp.concatenate([lhs * only_a2, lhs * only_b2], axis=0), rhs)
        aa_a.append(jnp.where(aa_mask, aa2[:2 * c], 0.0))
        aa_b.append(jnp.where(aa_mask, aa2[2 * c:], 0.0))
    av = []
    for i in range(ni):
        vb = vh[i].astype(BF16)
        v_rows = jnp.concatenate([zeros_b, vb * only_a, zeros_b, vb * only_b], axis=0)
        av.append(_dot(jnp.concatenate([aa_a[i], aa_b[i]], axis=1).astype(BF16), v_rows))
    a_ab = [jnp.where(head_a, aa_a[i][:c], pltpu.roll(aa_b[i][:c], n, axis=1)).astype(BF16) for i in range(ni)]
    a_rb = [jnp.where(head_a, aa_a[i][c:], pltpu.roll(aa_b[i][c:], n, axis=1)).astype(BF16) for i in range(ni)]

    tinv = [eye_f - jnp.where(diag2, a.astype(F32), 0.0) for a in a_ab]
    for li in range(len(levels)):
        tb = [t_.astype(BF16) for t_ in tinv]
        half = []
        for i in range(0, ni, 2):
            both2 = _dot(jnp.concatenate([tb[i], tb[i + 1]], axis=0),
                         jnp.concatenate([both(a_ab[i], lvl_a[li], lvl_b[li]),
                                          both(a_ab[i + 1], lvl_a[li], lvl_b[li])], axis=1))
            half += [both2[:c, :LANES].astype(BF16), both2[c:, LANES:].astype(BF16)]
        tinv = [tinv[i] - _dot(half[i], both(tb[i])) for i in range(ni)]

    w12 = [_dot(tinv[i].astype(BF16),
                jnp.concatenate([both(kkt[i].astype(BF16)), both(av[i][:c].astype(BF16))], axis=1))
           for i in range(ni)]
    w12_b = [w.astype(BF16) for w in w12]
    pq = [jnp.concatenate([rt[i], av[i][c:]], axis=1)
          - _dot(a_rb[i], jnp.concatenate([both(w12_b[i][:, :LANES]), both(w12_b[i][:, LANES:])], axis=1))
          for i in range(ni)]
    mg = []
    for i, ((bi, _), sl) in enumerate(zip(items, sls)):
        neg_bd_kd = jnp.concatenate([-pre[bi]["bd"][:, sl], pre[bi]["kd"][:, sl]], axis=0).astype(BF16)
        wv = jnp.concatenate([w12_b[i], jnp.concatenate([zeros_b, vh[i].astype(BF16)], axis=1)], axis=0)
        mg.append(_dot_tn(neg_bd_kd, wv))

    for i, ((bi, _), sl) in enumerate(zip(items, sls)):
        m_mat = jnp.where(block_diag, mg[i][:, :LANES], 0.0) + jnp.where(eye2, pre[bi]["e_tot"][:, sl], 0.0)
        pm = jnp.concatenate([pq[i][:, :LANES], m_mat], axis=0).astype(BF16)
        out = _dot(pm, state_ref[i].astype(BF16))
        y_ref[bi, :, sl] = out[:c] + pq[i][:, LANES:]
        state_ref[i] = out[c:] + jnp.where(block_diag, mg[i][:, LANES:], 0.0)


def _merge_kernel(x_ref, y_ref, gbv_ref, bc_ref, g_ref, lnw_ref, lnb_ref, wg_ref, bg_ref,
                  wr_ref, wp_ref, wm_ref, wo_ref, out_ref):
    x = x_ref[...]
    h = _rms_norm(x, g_ref[...]).astype(BF16)
    d = D_MODEL
    z = [_dot(h, wg_ref[:, i * d:(i + 1) * d]) for i in range(3)]
    up_b = _dot(bc_ref[:, :POOL_DIM].astype(BF16), wp_ref[...])
    up_c = _dot(bc_ref[:, POOL_DIM:].astype(BF16), wm_ref[...])
    inv_n = 1.0 / HEAD_DIM
    up_a = None
    half = RWKV_DIM // 2
    for c0 in (0, half):
        cs = slice(c0, c0 + half)
        y = y_ref[:, cs]
        yc = y - _head_lane_sums(y) * inv_n
        var = _head_lane_sums(yc * yc) * inv_n
        bonus = gbv_ref[:, RWKV_DIM + c0:RWKV_DIM + c0 + half]
        ya = (yc * lax.rsqrt(var + GN_EPS) * lnw_ref[:, cs] + lnb_ref[:, cs] + bonus) * gbv_ref[:, cs]
        part = _dot(ya.astype(BF16), wr_ref[cs, :])
        up_a = part if up_a is None else up_a + part
    gate = [_sigmoid(z[i] + bg_ref[:, i * d:(i + 1) * d]) for i in range(3)]
    merged = gate[0] * up_a + gate[1] * up_b + gate[2] * up_c
    out_ref[...] = x + _dot(merged.astype(BF16), wo_ref[...])


def _conv_ffn_kernel(x_ref, g_ref, win_ref, cw_ref, cb_ref, wout_ref, gf_ref, out_ref, tail_ref, act_ref):
    t = pl.program_id(1)
    tm = x_ref.shape[0]
    x = x_ref[...]
    h = _rms_norm(x, g_ref[...]).astype(BF16)
    row = lax.broadcasted_iota(jnp.int32, (tm, 1), 0)
    first = t == 0
    bounds = list(range(0, D_FF, FF_CHUNK)) + [D_FF]
    n_chunks = len(bounds) - 1

    def up_proj(j):
        return (_dot(h, win_ref[:, bounds[j]:bounds[j + 1]]),
                _dot(h, win_ref[:, D_FF + bounds[j]:D_FF + bounds[j + 1]]))

    nxt = up_proj(0)
    for j in range(n_chunks):
        cs = slice(bounds[j], bounds[j + 1])
        u, gv = nxt
        if j + 1 < n_chunks:
            nxt = up_proj(j + 1)
        tail = jnp.where(first, 0.0, tail_ref[:, cs])
        tail_ref[:, cs] = u[tm - 8:, :]
        u1 = jnp.where(row == 0, tail[7:8, :], pltpu.roll(u, 1, axis=0))
        u2 = jnp.where(row == 0, tail[6:7, :], jnp.where(row == 1, tail[7:8, :], pltpu.roll(u, 2, axis=0)))
        uc = cw_ref[0:1, cs] * u2 + cw_ref[1:2, cs] * u1 + cw_ref[2:3, cs] * u + cb_ref[:, cs]
        act = 0.5 * uc * (1.0 + lax.erf(uc * (2.0 ** -0.5))) * gv
        act_ref[:, cs] = act.astype(BF16)
    out_ref[...] = _rms_norm(x + _dot(act_ref[...], wout_ref[...]), gf_ref[...])


def _cast_block_rows(rows, max_steps):
    br = BF16_SUBLANES
    while rows % br or rows // br > max_steps:
        br += BF16_SUBLANES
    return br


def _const_spec(shape):
    nd = len(shape)
    return pl.BlockSpec(shape, lambda *_: (0,) * nd)


def kernel(x, mem, norm_mix_g, w_in_mix, mu_shift, w0, w_lora_b, a0, a_lora_b, g_lora_b, k_k, k_a, r_k,
           ln_x_w, ln_x_b, pool_w, pool_scale, norm_mem_g, w_mem_kv, w_up_rwkv, w_up_pool, w_up_mem,
           w_gate, b_gate, w_o, norm_ffn_g, w_ffn_in, ffn_conv_w, ffn_conv_b, w_ffn_out, norm_final_g):
    bsz, seq, d = x.shape
    n_mem = mem.shape[1]
    assert d == D_MODEL and seq % TM_PROJ == 0 and seq % CHUNK == 0 and bsz % SCAN_BATCH == 0
    assert seq % TM_FFN == 0 and (bsz * seq) % TM_MERGE == 0
    assert norm_mix_g.shape[0] == 1, "single-layer block"
    l = 0
    row = lambda a: a.reshape(1, -1).astype(F32)

    wmix = w_in_mix[l].astype(BF16)
    zpad = jnp.zeros((DECAY_LORA, RWKV_DIM), F32)
    wl_pad = jnp.concatenate([w_lora_b[l], zpad], axis=0).astype(BF16)
    al_pad = jnp.concatenate([zpad, a_lora_b[l]], axis=0).astype(BF16)
    gl = g_lora_b[l].astype(BF16)
    pool_bd = jax.scipy.linalg.block_diag(*[pool_w[l, i] for i in range(len(POOL_WINDOWS))]).astype(BF16)

    kv = pl.pallas_call(
        _mem_kv_kernel,
        out_shape=jax.ShapeDtypeStruct((bsz, n_mem, 2 * MEM_DIM), BF16),
        grid=(bsz,),
        in_specs=[pl.BlockSpec((None, n_mem, d), lambda b: (b, 0, 0)),
                  _const_spec((1, d)), _const_spec((d, 2 * MEM_DIM))],
        out_specs=pl.BlockSpec((None, n_mem, 2 * MEM_DIM), lambda b: (b, 0, 0)),
        compiler_params=pltpu.CompilerParams(dimension_semantics=("arbitrary",)),
        name="mem_kv",
    )(mem, row(norm_mem_g[l]), w_mem_kv[l].astype(BF16))

    tok = lambda w: pl.BlockSpec((None, TM_PROJ, w), lambda b, t: (b, t, 0))
    f32_out = lambda w: jax.ShapeDtypeStruct((bsz, seq, w), F32)
    outs = pl.pallas_call(
        _mix_proj_kernel,
        out_shape=[f32_out(PACK_SLOTS * RWKV_DIM), f32_out(POOL_DIM + MEM_DIM)],
        grid=(bsz, seq // TM_PROJ),
        in_specs=[tok(d), _const_spec((1, d)), _const_spec((d, MIX_IN)), _const_spec((1, RWKV_IN)),
                  _const_spec((1, RWKV_DIM)), _const_spec((DECAY_LORA + ICLR_LORA, RWKV_DIM)),
                  _const_spec((1, RWKV_DIM)), _const_spec((DECAY_LORA + ICLR_LORA, RWKV_DIM)),
                  _const_spec((GATE_LORA, RWKV_DIM)), _const_spec((1, RWKV_DIM)), _const_spec((1, RWKV_DIM)),
                  _const_spec((1, RWKV_DIM)), _const_spec((POOL_DIM, POOL_DIM)),
                  _const_spec((1, POOL_DIM)),
                  pl.BlockSpec((None, n_mem, 2 * MEM_DIM), lambda b, t: (b, 0, 0))],
        out_specs=[tok(PACK_SLOTS * RWKV_DIM), tok(POOL_DIM + MEM_DIM)],
        scratch_shapes=[pltpu.VMEM((1, RWKV_IN), F32), pltpu.VMEM((POOL_HALO, POOL_DIM), F32)],
        compiler_params=pltpu.CompilerParams(dimension_semantics=("arbitrary", "arbitrary"),
                                             vmem_limit_bytes=V7X_VMEM_LIMIT_BYTES),
        name="mix_proj",
    )(x, row(norm_mix_g[l]), wmix, row(mu_shift[l]), row(w0[l]), wl_pad, row(a0[l]), al_pad, gl,
      row(k_k[l]), row(k_a[l]), row(r_k[l]), pool_bd, row(pool_scale[l]), kv)
    packed, y_bc = outs

    sb = SCAN_BATCH
    blk = pl.BlockSpec((sb, CHUNK, RWKV_DIM), lambda b, c: (b, c, 0))
    slot = lambda j: pl.BlockSpec((sb, CHUNK, RWKV_DIM), lambda b, c: (b, c, j))
    n_chunks = seq // CHUNK
    later_w = [w_gate[l], w_o[l], w_ffn_in[l], w_ffn_out[l]]
    cast_rows = [_cast_block_rows(w.shape[0], n_chunks) for w in later_w]
    cast_steps = tuple(w.shape[0] // br for w, br in zip(later_w, cast_rows))
    wblk = lambda w, br: pl.BlockSpec((br, w.shape[1]), lambda b, c: (jnp.minimum(c, w.shape[0] // br - 1), 0))
    scan_out = pl.pallas_call(
        functools.partial(_rwkv_scan_kernel, cast_steps=cast_steps, n_steps=n_chunks),
        out_shape=[f32_out(RWKV_DIM)] + [jax.ShapeDtypeStruct(w.shape, BF16) for w in later_w],
        grid=(bsz // sb, n_chunks),
        in_specs=([slot(j) for j in (PACK_R, PACK_K, PACK_V, PACK_KK, PACK_B, PACK_LW)]
                  + [wblk(w, br) for w, br in zip(later_w, cast_rows)]),
        out_specs=[blk] + [wblk(w, br) for w, br in zip(later_w, cast_rows)],
        scratch_shapes=[pltpu.VMEM((sb * RWKV_DIM // LANES, LANES, LANES), F32)],
        compiler_params=pltpu.CompilerParams(dimension_semantics=("arbitrary", "arbitrary")),
        name="rwkv_scan",
    )(*([packed] * 6), *later_w)
    y_a, wg_b, wo_b, wfi_b, wfo_b = scan_out

    n_tok = bsz * seq
    flat = lambda a: a.reshape(n_tok, a.shape[-1])
    tokm = lambda w: pl.BlockSpec((TM_MERGE, w), lambda i: (i, 0))
    wspec = lambda shape: pl.BlockSpec(shape, lambda i: (0, 0), pipeline_mode=pl.Buffered(1))
    x1 = pl.pallas_call(
        _merge_kernel,
        out_shape=jax.ShapeDtypeStruct((n_tok, d), F32),
        grid=(n_tok // TM_MERGE,),
        in_specs=[tokm(d), tokm(RWKV_DIM),
                  pl.BlockSpec((TM_MERGE, 2 * RWKV_DIM), lambda i: (i, PACK_G // 2)), tokm(POOL_DIM + MEM_DIM),
                  _const_spec((1, d)), _const_spec((1, RWKV_DIM)), _const_spec((1, RWKV_DIM)),
                  wspec((d, 3 * d)), _const_spec((1, 3 * d)), wspec((RWKV_DIM, d)),
                  wspec((POOL_DIM, d)), wspec((MEM_DIM, d)), wspec((d, d))],
        out_specs=tokm(d),
        compiler_params=pltpu.CompilerParams(dimension_semantics=("arbitrary",),
                                             vmem_limit_bytes=V7X_VMEM_LIMIT_BYTES),
        name="merge",
    )(flat(x), flat(y_a), flat(packed), flat(y_bc), row(norm_mix_g[l]),
      row(ln_x_w[l]), row(ln_x_b[l]), wg_b, row(b_gate[l]), w_up_rwkv[l].astype(BF16),
      w_up_pool[l].astype(BF16), w_up_mem[l].astype(BF16), wo_b)

    tokf = pl.BlockSpec((None, TM_FFN, d), lambda b, t: (b, t, 0))
    single = pl.Buffered(1)
    out = pl.pallas_call(
        _conv_ffn_kernel,
        out_shape=jax.ShapeDtypeStruct((bsz, seq, d), F32),
        grid=(bsz, seq // TM_FFN),
        in_specs=[tokf, _const_spec((1, d)),
                  pl.BlockSpec((d, 2 * D_FF), lambda b, t: (0, 0), pipeline_mode=single),
                  _const_spec((3, D_FF)), _const_spec((1, D_FF)),
                  pl.BlockSpec((D_FF, d), lambda b, t: (0, 0), pipeline_mode=single),
                  _const_spec((1, d))],
        out_specs=tokf,
        scratch_shapes=[pltpu.VMEM((8, D_FF), F32), pltpu.VMEM((TM_FFN, D_FF), BF16)],
        compiler_params=pltpu.CompilerParams(dimension_semantics=("arbitrary", "arbitrary"),
                                             vmem_limit_bytes=V7X_VMEM_LIMIT_BYTES),
        name="conv_ffn",
    )(x1.reshape(bsz, seq, d), row(norm_ffn_g[l]), wfi_b, ffn_conv_w[l].astype(F32),
      row(ffn_conv_b[l]), wfo_b, row(norm_final_g))
    return out
```

```python
import functools
import math

import jax
import jax.numpy as jnp
from jax import lax
from jax.experimental import pallas as pl
from jax.experimental.pallas import tpu as pltpu

F32 = jnp.float32
BF16 = jnp.bfloat16

D_MODEL = 1024
HEAD_DIM = 64
RWKV_HEADS = 8
RWKV_DIM = RWKV_HEADS * HEAD_DIM
DECAY_LORA = 64
ICLR_LORA = 64
GATE_LORA = 128
RWKV_IN = 3 * RWKV_DIM + DECAY_LORA + ICLR_LORA + GATE_LORA
POOL_WINDOWS = (2, 4, 8, 16)
POOL_GROUP_DIM = 64
POOL_DIM = len(POOL_WINDOWS) * POOL_GROUP_DIM
POOL_HALO = 16
MEM_HEADS = 4
MEM_DIM = MEM_HEADS * HEAD_DIM
MIX_IN = RWKV_IN + POOL_DIM + MEM_DIM
D_FF = 2816
NORM_EPS = 1e-6
GN_EPS = 64e-5
CHUNK = 64
FF_CHUNK = 1024
V7X_VMEM_LIMIT_BYTES = 56 * 1024 * 1024
LANES = 128
BF16_SUBLANES = 16
PACK_R, PACK_K, PACK_V, PACK_KK, PACK_B, PACK_LW, PACK_G, PACK_BV = range(8)
PACK_SLOTS = 8

TM_PROJ = 512
TM_MERGE = 512
TM_FFN = 1024
SCAN_BATCH = 8


def _dot(a, b):
    return jnp.dot(a, b, preferred_element_type=F32)


def _dot_nt(a, b):
    return lax.dot_general(a, b, (((1,), (1,)), ((), ())), preferred_element_type=F32)


def _dot_tn(a, b):
    return lax.dot_general(a, b, (((0,), (0,)), ((), ())), preferred_element_type=F32)


def _sigmoid(x):
    return 0.5 * jnp.tanh(0.5 * x) + 0.5


def _rms_norm(x, g):
    return x * lax.rsqrt(jnp.mean(x * x, axis=-1, keepdims=True) + NORM_EPS) * g


def _split3_dot_left(w_bf16, x):
    hi = x.astype(BF16)
    r1 = x - hi.astype(F32)
    mid = r1.astype(BF16)
    lo = (r1 - mid.astype(F32)).astype(BF16)
    return _dot(w_bf16, hi) + _dot(w_bf16, mid) + _dot(w_bf16, lo)


def _head_lane_sums(x):
    first_head = lax.broadcasted_iota(jnp.int32, (1, LANES), 1) < HEAD_DIM
    cols = []
    for j in range(0, x.shape[1], LANES):
        col = x[:, j:j + LANES]
        sa = jnp.sum(jnp.where(first_head, col, 0.0), axis=-1, keepdims=True)
        sb = jnp.sum(jnp.where(first_head, 0.0, col), axis=-1, keepdims=True)
        cols.append(jnp.where(first_head, sa, sb))
    return cols[0] if len(cols) == 1 else jnp.concatenate(cols, axis=1)


def _mix_proj_kernel(x_ref, g_ref, wmix_ref, mu_ref, w0_ref, wl_ref, a0_ref, al_ref, gl_ref,
                     kk_ref, ka_ref, rk_ref, poolw_ref, pools_ref, mem_ref, gmem_ref, wkv_ref,
                     pack_out, bc_out,
                     prev_ref, halo_ref, kv_ref):
    t = pl.program_id(1)
    tm = x_ref.shape[0]
    first = t == 0

    @pl.when(first)
    def _():
        m = _rms_norm(mem_ref[...], gmem_ref[...])
        kv_ref[...] = _dot(m.astype(BF16), wkv_ref[...]).astype(BF16)

    def put(slot, val):
        pack_out[:, slot * RWKV_DIM:(slot + 1) * RWKV_DIM] = val

    h = _rms_norm(x_ref[...], g_ref[...]).astype(BF16)
    row = lax.broadcasted_iota(jnp.int32, (tm, 1), 0)
    lo0 = 3 * RWKV_DIM

    def proj(c0, c1):
        return _dot(h, wmix_ref[:, c0:c1])

    def token_shift(ps, c0, c1):
        carry = jnp.where(first, 0.0, prev_ref[:, c0:c1])
        prev = jnp.where(row == 0, carry, pltpu.roll(ps, 1, axis=0))
        prev_ref[:, c0:c1] = ps[tm - 1:tm, :]
        return ps + (prev - ps) * mu_ref[:, c0:c1]

    p_pq = proj(RWKV_IN, MIX_IN)
    p_lo = proj(lo0, RWKV_IN)
    p_r = proj(0, RWKV_DIM)
    pp = p_pq[:, :POOL_DIM]
    q = p_pq[:, POOL_DIM:]

    kmem = kv_ref[:, :MEM_DIM]
    vmem = kv_ref[:, MEM_DIM:]
    lane_head = lax.broadcasted_iota(jnp.int32, (1, MEM_DIM), 1) // HEAD_DIM
    scores = [_dot_nt(jnp.where(lane_head == hd, q, 0.0).astype(BF16), kmem) * (HEAD_DIM ** -0.5)
              for hd in range(MEM_HEADS)]

    p_k = proj(RWKV_DIM, 2 * RWKV_DIM)
    p_v = proj(2 * RWKV_DIM, lo0)

    lo = token_shift(p_lo, lo0, RWKV_IN)
    wa = lo[:, :DECAY_LORA + ICLR_LORA]
    gd = lo[:, DECAY_LORA + ICLR_LORA:]
    z = w0_ref[...] + _dot(jnp.tanh(wa).astype(BF16), wl_ref[...])
    put(PACK_LW, (-math.exp(-0.5)) * _sigmoid(z))
    a = _sigmoid(a0_ref[...] + _dot(wa.astype(BF16), al_ref[...]))
    put(PACK_G, _dot(_sigmoid(gd).astype(BF16), gl_ref[...]))

    halo = jnp.where(first, 0.0, halo_ref[...])
    halo_ref[...] = pp[tm - POOL_HALO:, :]
    ext = jnp.concatenate([halo, pp], axis=0)
    lane_group = lax.broadcasted_iota(jnp.int32, (1, POOL_DIM), 1) // POOL_GROUP_DIM
    win = jnp.zeros((1, POOL_DIM), jnp.int32)
    sel = jnp.zeros((tm, POOL_DIM), F32)
    acc = ext
    shift = 1
    for gi, w in enumerate(POOL_WINDOWS):
        while shift < w:
            acc = acc + pltpu.roll(acc, shift, axis=0)
            shift *= 2
        sel = jnp.where(lane_group == gi, acc[POOL_HALO:, :], sel)
        win = jnp.where(lane_group == gi, w, win)
    pos = t * tm + row + 1
    cnt = jnp.minimum(pos, win).astype(F32)
    dpool = sel / cnt - pp
    bc_out[:, :POOL_DIM] = _dot(dpool.astype(BF16), poolw_ref[...]) * pools_ref[...]

    cols = []
    for j in range(0, MEM_DIM, LANES):
        col = None
        for hd in range(j // HEAD_DIM, (j + LANES) // HEAD_DIM):
            s = scores[hd]
            e = jnp.exp(s - jnp.max(s, axis=-1, keepdims=True))
            prob = e / jnp.sum(e, axis=-1, keepdims=True)
            vh = jnp.where((lane_head == hd)[:, j:j + LANES], vmem[:, j:j + LANES], 0.0)
            o = _dot(prob.astype(BF16), vh)
            col = o if col is None else col + o
        cols.append(col)
    bc_out[:, POOL_DIM:] = jnp.concatenate(cols, axis=1)

    r = token_shift(p_r, 0, RWKV_DIM)
    k = token_shift(p_k, RWKV_DIM, 2 * RWKV_DIM)
    v = token_shift(p_v, 2 * RWKV_DIM, lo0)
    kkr = k * kk_ref[...]
    kkn = kkr * lax.rsqrt(_head_lane_sums(kkr * kkr) + 1e-12)
    kf = k * (1.0 + (a - 1.0) * ka_ref[...])
    put(PACK_R, r)
    put(PACK_K, kf)
    put(PACK_V, v)
    put(PACK_KK, kkn)
    put(PACK_B, kkn * a)
    put(PACK_BV, _head_lane_sums(r * kf * rk_ref[...]) * v)


def _chunk_masks():
    c = CHUNK
    ri = lax.broadcasted_iota(jnp.int32, (2 * c, 2 * c), 0)
    ci = lax.broadcasted_iota(jnp.int32, (2 * c, 2 * c), 1)
    rt = ri % c
    ct = ci % c
    aa_mask = (rt > ct) | ((ri >= c) & (rt == ct))
    r64 = lax.broadcasted_iota(jnp.int32, (c, 2 * c), 0)
    c64 = lax.broadcasted_iota(jnp.int32, (c, 2 * c), 1) % c
    eye = r64 == c64
    levels = []
    blk = 2
    while blk < c:
        same_outer = (r64 // (2 * blk)) == (c64 // (2 * blk))
        diff_inner = (r64 // blk) != (c64 // blk)
        levels.append(same_outer & diff_inner & (r64 > c64))
        blk *= 2
    diag2 = ((r64 // 2) == (c64 // 2)) & (r64 > c64)
    tri_incl = lax.broadcasted_iota(jnp.int32, (c, c), 0) >= lax.broadcasted_iota(jnp.int32, (c, c), 1)
    block_diag = (ri < c) == (ci < c)
    eye2 = ri == ci
    return aa_mask, eye, diag2, levels, tri_incl, block_diag, eye2


def _rwkv_scan_kernel(*refs, cast_steps, n_steps):
    r_ref, k_ref, v_ref, kk_ref, b_ref, lw_ref = refs[:6]
    nw = len(cast_steps)
    w_refs = refs[6:6 + nw]
    y_ref = refs[6 + nw]
    wb_refs = refs[7 + nw:7 + 2 * nw]
    state_ref = refs[7 + 2 * nw]
    c = CHUNK
    n = HEAD_DIM
    chunk_idx = pl.program_id(1)

    for w_ref, wb_ref, steps in zip(w_refs, wb_refs, cast_steps):
        if steps == n_steps:
            wb_ref[...] = w_ref[...].astype(BF16)
        else:
            @pl.when(chunk_idx < steps)
            def _(w_ref=w_ref, wb_ref=wb_ref):
                wb_ref[...] = w_ref[...].astype(BF16)

    @pl.when(chunk_idx == 0)
    def _():
        state_ref[...] = jnp.zeros_like(state_ref)

    aa_mask, eye, diag2, levels, tri_incl, block_diag, eye2 = _chunk_masks()
    tri = jnp.where(tri_incl, 1.0, 0.0).astype(BF16)
    eye_f = jnp.where(eye, 1.0, 0.0)
    zeros_b = jnp.zeros((c, LANES), BF16)
    head_a = lax.broadcasted_iota(jnp.int32, (c, LANES), 1) < n
    only_a = jnp.where(head_a, 1.0, 0.0).astype(BF16)
    only_b = jnp.where(head_a, 0.0, 1.0).astype(BF16)
    only_a2 = jnp.concatenate([only_a, only_a], axis=0)
    only_b2 = jnp.concatenate([only_b, only_b], axis=0)
    lvl_a = [jnp.where(lvl & head_a, 1.0, 0.0).astype(BF16) for lvl in levels]
    lvl_b = [jnp.where(lvl & ~head_a, 1.0, 0.0).astype(BF16) for lvl in levels]

    def both(xb, ma=only_a, mb=only_b):
        return jnp.concatenate([xb * ma, xb * mb], axis=0)

    nb = r_ref.shape[0]
    items = [(bi, j) for bi in range(nb) for j in range(RWKV_DIM // LANES)]
    sls = [slice(j * LANES, (j + 1) * LANES) for _, j in items]
    ni = len(items)

    pre = []
    for bi in range(nb):
        lw = lw_ref[bi]
        cum = _split3_dot_left(tri, lw)
        tot = cum[c - 1:c, :]
        e_inv = jnp.exp(-cum)
        e_tot = jnp.exp(tot)
        bt = b_ref[bi] * e_inv
        kt = k_ref[bi] * e_inv
        pre.append(dict(
            rt=r_ref[bi] * jnp.exp(cum), kkt=kk_ref[bi] * jnp.exp(cum - lw), bt=bt, kt=kt,
            bd=bt * e_tot, kd=kt * e_tot, e_tot=e_tot, v=v_ref[bi]))

    rt = [pre[bi]["rt"][:, sl] for (bi, _), sl in zip(items, sls)]
    kkt = [pre[bi]["kkt"][:, sl] for (bi, _), sl in zip(items, sls)]
    vh = [pre[bi]["v"][:, sl] for (bi, _), sl in zip(items, sls)]
    aa_a, aa_b = [], []
    for i, ((bi, _), sl) in enumerate(zip(items, sls)):
        lhs = jnp.concatenate([kkt[i], rt[i]], axis=0).astype(BF16)
        rhs = jnp.concatenate([pre[bi]["bt"][:, sl], pre[bi]["kt"][:, sl]], axis=0).astype(BF16)
        aa2 = _dot_nt(jnp.concatenate([lhs * only_a2, lhs * only_b2], axis=0), rhs)
        aa_a.append(jnp.where(aa_mask, aa2[:2 * c], 0.0))
        aa_b.append(jnp.where(aa_mask, aa2[2 * c:], 0.0))
    av = []
    for i in range(ni):
        vb = vh[i].astype(BF16)
        v_rows = jnp.concatenate([zeros_b, vb * only_a, zeros_b, vb * only_b], axis=0)
        av.append(_dot(jnp.concatenate([aa_a[i], aa_b[i]], axis=1).astype(BF16), v_rows))
    a_ab = [jnp.where(head_a, aa_a[i][:c], pltpu.roll(aa_b[i][:c], n, axis=1)).astype(BF16) for i in range(ni)]
    a_rb = [jnp.where(head_a, aa_a[i][c:], pltpu.roll(aa_b[i][c:], n, axis=1)).astype(BF16) for i in range(ni)]

    tinv = [eye_f - jnp.where(diag2, a.astype(F32), 0.0) for a in a_ab]
    for li in range(len(levels)):
        tb = [t_.astype(BF16) for t_ in tinv]
        half = []
        for i in range(0, ni, 2):
            both2 = _dot(jnp.concatenate([tb[i], tb[i + 1]], axis=0),
                         jnp.concatenate([both(a_ab[i], lvl_a[li], lvl_b[li]),
                                          both(a_ab[i + 1], lvl_a[li], lvl_b[li])], axis=1))
            half += [both2[:c, :LANES].astype(BF16), both2[c:, LANES:].astype(BF16)]
        tinv = [tinv[i] - _dot(half[i], both(tb[i])) for i in range(ni)]

    w12 = [_dot(tinv[i].astype(BF16),
                jnp.concatenate([both(kkt[i].astype(BF16)), both(av[i][:c].astype(BF16))], axis=1))
           for i in range(ni)]
    w12_b = [w.astype(BF16) for w in w12]
    pq = [jnp.concatenate([rt[i], av[i][c:]], axis=1)
          - _dot(a_rb[i], jnp.concatenate([both(w12_b[i][:, :LANES]), both(w12_b[i][:, LANES:])], axis=1))
          for i in range(ni)]
    mg = []
    for i, ((bi, _), sl) in enumerate(zip(items, sls)):
        neg_bd_kd = jnp.concatenate([-pre[bi]["bd"][:, sl], pre[bi]["kd"][:, sl]], axis=0).astype(BF16)
        wv = jnp.concatenate([w12_b[i], jnp.concatenate([zeros_b, vh[i].astype(BF16)], axis=1)], axis=0)
        mg.append(_dot_tn(neg_bd_kd, wv))

    for i, ((bi, _), sl) in enumerate(zip(items, sls)):
        m_mat = jnp.where(block_diag, mg[i][:, :LANES], 0.0) + jnp.where(eye2, pre[bi]["e_tot"][:, sl], 0.0)
        pm = jnp.concatenate([pq[i][:, :LANES], m_mat], axis=0).astype(BF16)
        out = _dot(pm, state_ref[i].astype(BF16))
        y_ref[bi, :, sl] = out[:c] + pq[i][:, LANES:]
        state_ref[i] = out[c:] + jnp.where(block_diag, mg[i][:, LANES:], 0.0)


def _merge_kernel(x_ref, y_ref, gbv_ref, bc_ref, g_ref, lnw_ref, lnb_ref, wg_ref, bg_ref,
                  wr_ref, wp_ref, wm_ref, wo_ref, out_ref):
    x = x_ref[...]
    h = _rms_norm(x, g_ref[...]).astype(BF16)
    d = D_MODEL
    z = [_dot(h, wg_ref[:, i * d:(i + 1) * d]) for i in range(3)]
    up_b = _dot(bc_ref[:, :POOL_DIM].astype(BF16), wp_ref[...])
    up_c = _dot(bc_ref[:, POOL_DIM:].astype(BF16), wm_ref[...])
    inv_n = 1.0 / HEAD_DIM
    up_a = None
    half = RWKV_DIM // 2
    for c0 in (0, half):
        cs = slice(c0, c0 + half)
        y = y_ref[:, cs]
        yc = y - _head_lane_sums(y) * inv_n
        var = _head_lane_sums(yc * yc) * inv_n
        bonus = gbv_ref[:, RWKV_DIM + c0:RWKV_DIM + c0 + half]
        ya = (yc * lax.rsqrt(var + GN_EPS) * lnw_ref[:, cs] + lnb_ref[:, cs] + bonus) * gbv_ref[:, cs]
        part = _dot(ya.astype(BF16), wr_ref[cs, :])
        up_a = part if up_a is None else up_a + part
    gate = [_sigmoid(z[i] + bg_ref[:, i * d:(i + 1) * d]) for i in range(3)]
    merged = gate[0] * up_a + gate[1] * up_b + gate[2] * up_c
    out_ref[...] = x + _dot(merged.astype(BF16), wo_ref[...])


def _conv_ffn_kernel(x_ref, g_ref, win_ref, cw_ref, cb_ref, wout_ref, gf_ref, out_ref, tail_ref, act_ref):
    t = pl.program_id(1)
    tm = x_ref.shape[0]
    x = x_ref[...]
    h = _rms_norm(x, g_ref[...]).astype(BF16)
    row = lax.broadcasted_iota(jnp.int32, (tm, 1), 0)
    first = t == 0
    bounds = list(range(0, D_FF, FF_CHUNK)) + [D_FF]
    n_chunks = len(bounds) - 1

    def up_proj(j):
        return (_dot(h, win_ref[:, bounds[j]:bounds[j + 1]]),
                _dot(h, win_ref[:, D_FF + bounds[j]:D_FF + bounds[j + 1]]))

    nxt = up_proj(0)
    for j in range(n_chunks):
        cs = slice(bounds[j], bounds[j + 1])
        u, gv = nxt
        if j + 1 < n_chunks:
            nxt = up_proj(j + 1)
        tail = jnp.where(first, 0.0, tail_ref[:, cs])
        tail_ref[:, cs] = u[tm - 8:, :]
        u1 = jnp.where(row == 0, tail[7:8, :], pltpu.roll(u, 1, axis=0))
        u2 = jnp.where(row == 0, tail[6:7, :], jnp.where(row == 1, tail[7:8, :], pltpu.roll(u, 2, axis=0)))
        uc = cw_ref[0:1, cs] * u2 + cw_ref[1:2, cs] * u1 + cw_ref[2:3, cs] * u + cb_ref[:, cs]
        act = 0.5 * uc * (1.0 + lax.erf(uc * (2.0 ** -0.5))) * gv
        act_ref[:, cs] = act.astype(BF16)
    out_ref[...] = _rms_norm(x + _dot(act_ref[...], wout_ref[...]), gf_ref[...])


def _cast_block_rows(rows, max_steps):
    br = BF16_SUBLANES
    while rows % br or rows // br > max_steps:
        br += BF16_SUBLANES
    return br


def _const_spec(shape):
    nd = len(shape)
    return pl.BlockSpec(shape, lambda *_: (0,) * nd)


def kernel(x, mem, norm_mix_g, w_in_mix, mu_shift, w0, w_lora_b, a0, a_lora_b, g_lora_b, k_k, k_a, r_k,
           ln_x_w, ln_x_b, pool_w, pool_scale, norm_mem_g, w_mem_kv, w_up_rwkv, w_up_pool, w_up_mem,
           w_gate, b_gate, w_o, norm_ffn_g, w_ffn_in, ffn_conv_w, ffn_conv_b, w_ffn_out, norm_final_g):
    bsz, seq, d = x.shape
    n_mem = mem.shape[1]
    assert d == D_MODEL and seq % TM_PROJ == 0 and seq % CHUNK == 0 and bsz % SCAN_BATCH == 0
    assert seq % TM_FFN == 0 and (bsz * seq) % TM_MERGE == 0
    assert norm_mix_g.shape[0] == 1, "single-layer block"
    l = 0
    row = lambda a: a.reshape(1, -1).astype(F32)

    wmix = w_in_mix[l].astype(BF16)
    zpad = jnp.zeros((DECAY_LORA, RWKV_DIM), F32)
    wl_pad = jnp.concatenate([w_lora_b[l], zpad], axis=0).astype(BF16)
    al_pad = jnp.concatenate([zpad, a_lora_b[l]], axis=0).astype(BF16)
    gl = g_lora_b[l].astype(BF16)
    pool_bd = jax.scipy.linalg.block_diag(*[pool_w[l, i] for i in range(len(POOL_WINDOWS))]).astype(BF16)

    tok = lambda w: pl.BlockSpec((None, TM_PROJ, w), lambda b, t: (b, t, 0))
    f32_out = lambda w: jax.ShapeDtypeStruct((bsz, seq, w), F32)
    outs = pl.pallas_call(
        _mix_proj_kernel,
        out_shape=[f32_out(PACK_SLOTS * RWKV_DIM), f32_out(POOL_DIM + MEM_DIM)],
        grid=(bsz, seq // TM_PROJ),
        in_specs=[tok(d), _const_spec((1, d)), _const_spec((d, MIX_IN)), _const_spec((1, RWKV_IN)),
                  _const_spec((1, RWKV_DIM)), _const_spec((DECAY_LORA + ICLR_LORA, RWKV_DIM)),
                  _const_spec((1, RWKV_DIM)), _const_spec((DECAY_LORA + ICLR_LORA, RWKV_DIM)),
                  _const_spec((GATE_LORA, RWKV_DIM)), _const_spec((1, RWKV_DIM)), _const_spec((1, RWKV_DIM)),
                  _const_spec((1, RWKV_DIM)), _const_spec((POOL_DIM, POOL_DIM)),
                  _const_spec((1, POOL_DIM)),
                  pl.BlockSpec((None, n_mem, d), lambda b, t: (b, 0, 0)),
                  _const_spec((1, d)), _const_spec((d, 2 * MEM_DIM))],
        out_specs=[tok(PACK_SLOTS * RWKV_DIM), tok(POOL_DIM + MEM_DIM)],
        scratch_shapes=[pltpu.VMEM((1, RWKV_IN), F32), pltpu.VMEM((POOL_HALO, POOL_DIM), F32),
                        pltpu.VMEM((n_mem, 2 * MEM_DIM), BF16)],
        compiler_params=pltpu.CompilerParams(dimension_semantics=("arbitrary", "arbitrary"),
                                             vmem_limit_bytes=V7X_VMEM_LIMIT_BYTES),
        name="mix_proj",
    )(x, row(norm_mix_g[l]), wmix, row(mu_shift[l]), row(w0[l]), wl_pad, row(a0[l]), al_pad, gl,
      row(k_k[l]), row(k_a[l]), row(r_k[l]), pool_bd, row(pool_scale[l]),
      mem, row(norm_mem_g[l]), w_mem_kv[l].astype(BF16))
    packed, y_bc = outs

    sb = SCAN_BATCH
    blk = pl.BlockSpec((sb, CHUNK, RWKV_DIM), lambda b, c: (b, c, 0))
    slot = lambda j: pl.BlockSpec((sb, CHUNK, RWKV_DIM), lambda b, c: (b, c, j))
    n_chunks = seq // CHUNK
    later_w = [w_gate[l], w_o[l], w_ffn_in[l], w_ffn_out[l]]
    cast_rows = [_cast_block_rows(w.shape[0], n_chunks) for w in later_w]
    cast_steps = tuple(w.shape[0] // br for w, br in zip(later_w, cast_rows))
    wblk = lambda w, br: pl.BlockSpec((br, w.shape[1]), lambda b, c: (jnp.minimum(c, w.shape[0] // br - 1), 0))
    scan_out = pl.pallas_call(
        functools.partial(_rwkv_scan_kernel, cast_steps=cast_steps, n_steps=n_chunks),
        out_shape=[f32_out(RWKV_DIM)] + [jax.ShapeDtypeStruct(w.shape, BF16) for w in later_w],
        grid=(bsz // sb, n_chunks),
        in_specs=([slot(j) for j in (PACK_R, PACK_K, PACK_V, PACK_KK, PACK_B, PACK_LW)]
                  + [wblk(w, br) for w, br in zip(later_w, cast_rows)]),
        out_specs=[blk] + [wblk(w, br) for w, br in zip(later_w, cast_rows)],
        scratch_shapes=[pltpu.VMEM((sb * RWKV_DIM // LANES, LANES, LANES), F32)],
        compiler_params=pltpu.CompilerParams(dimension_semantics=("arbitrary", "arbitrary")),
        name="rwkv_scan",
    )(*([packed] * 6), *later_w)
    y_a, wg_b, wo_b, wfi_b, wfo_b = scan_out

    n_tok = bsz * seq
    flat = lambda a: a.reshape(n_tok, a.shape[-1])
    tokm = lambda w: pl.BlockSpec((TM_MERGE, w), lambda i: (i, 0))
    wspec = lambda shape: pl.BlockSpec(shape, lambda i: (0, 0), pipeline_mode=pl.Buffered(1))
    x1 = pl.pallas_call(
        _merge_kernel,
        out_shape=jax.ShapeDtypeStruct((n_tok, d), F32),
        grid=(n_tok // TM_MERGE,),
        in_specs=[tokm(d), tokm(RWKV_DIM),
                  pl.BlockSpec((TM_MERGE, 2 * RWKV_DIM), lambda i: (i, PACK_G // 2)), tokm(POOL_DIM + MEM_DIM),
                  _const_spec((1, d)), _const_spec((1, RWKV_DIM)), _const_spec((1, RWKV_DIM)),
                  wspec((d, 3 * d)), _const_spec((1, 3 * d)), wspec((RWKV_DIM, d)),
                  wspec((POOL_DIM, d)), wspec((MEM_DIM, d)), wspec((d, d))],
        out_specs=tokm(d),
        compiler_params=pltpu.CompilerParams(dimension_semantics=("arbitrary",),
                                             vmem_limit_bytes=V7X_VMEM_LIMIT_BYTES),
        name="merge",
    )(flat(x), flat(y_a), flat(packed), flat(y_bc), row(norm_mix_g[l]),
      row(ln_x_w[l]), row(ln_x_b[l]), wg_b, row(b_gate[l]), w_up_rwkv[l].astype(BF16),
      w_up_pool[l].astype(BF16), w_up_mem[l].astype(BF16), wo_b)

    tokf = pl.BlockSpec((None, TM_FFN, d), lambda b, t: (b, t, 0))
    single = pl.Buffered(1)
    out = pl.pallas_call(
        _conv_ffn_kernel,
        out_shape=jax.ShapeDtypeStruct((bsz, seq, d), F32),
        grid=(bsz, seq // TM_FFN),
        in_specs=[tokf, _const_spec((1, d)),
                  pl.BlockSpec((d, 2 * D_FF), lambda b, t: (0, 0), pipeline_mode=single),
                  _const_spec((3, D_FF)), _const_spec((1, D_FF)),
                  pl.BlockSpec((D_FF, d), lambda b, t: (0, 0), pipeline_mode=single),
                  _const_spec((1, d))],
        out_specs=tokf,
        scratch_shapes=[pltpu.VMEM((8, D_FF), F32), pltpu.VMEM((TM_FFN, D_FF), BF16)],
        compiler_params=pltpu.CompilerParams(dimension_semantics=("arbitrary", "arbitrary"),
                                             vmem_limit_bytes=V7X_VMEM_LIMIT_BYTES),
        name="conv_ffn",
    )(x1.reshape(bsz, seq, d), row(norm_ffn_g[l]), wfi_b, ffn_conv_w[l].astype(F32),
      row(ffn_conv_b[l]), wfo_b, row(norm_final_g))
    return out
```
